```python
import jax, jax.numpy as jnp
from jax import lax
import numpy as np

D_MODEL = 1024
BATCH = 16
SEQ = 2048
DEPTH = 1

CTX_LEN = 256
GRID_W = 64
EPS = 1e-6

M_HEADS = 8
M_HEAD_DIM = 128
M_WIDTH = M_HEADS * M_HEAD_DIM
M_CHUNK = 64
M_INIT = -1e30

A_HEADS = 8
A_KV_HEADS = 2
A_GROUP = A_HEADS // A_KV_HEADS
A_HEAD_DIM = 128
A_Q_WIDTH = A_HEADS * A_HEAD_DIM
A_KV_WIDTH = A_KV_HEADS * A_HEAD_DIM
Q_BLOCK = 128
ROPE_THETA = 10000.0

N_BRANCH = 2

N_GROUPS = 4
EXPERTS_PER_GROUP = 8
N_EXPERTS = N_GROUPS * EXPERTS_PER_GROUP
TOP_K = 2
D_EXPERT = 256

SPLITS = (M_WIDTH, M_WIDTH, M_WIDTH, M_WIDTH, 4 * M_HEADS, A_Q_WIDTH, A_KV_WIDTH, A_KV_WIDTH, N_BRANCH * D_MODEL)
IN_WIDTH = 4 * M_WIDTH + 4 * M_HEADS + A_Q_WIDTH + 2 * A_KV_WIDTH + N_BRANCH * D_MODEL

kernel_name = "hybrid_mlstm_gqa_hmoe_dit_layer"


def rms(x, w=None):
    xf = x.astype(jnp.float32)
    y = (xf * lax.rsqrt(jnp.mean(xf * xf, axis=-1, keepdims=True) + EPS)).astype(x.dtype)
    return y if w is None else y * w


def ada_mod(cvec, w, b):
    mod = jax.nn.silu(cvec) @ w + b
    return jnp.split(mod[..., None, :], 6, axis=-1)


def modulate(xn, shift, scale):
    return xn * (1.0 + scale) + shift


def split_proj(p):
    points = [int(s) for s in np.cumsum(SPLITS)[:-1]]
    return jnp.split(p, points, axis=-1)


def axial_rope_tables(n_tokens):
    rows = n_tokens // GRID_W
    row = jnp.repeat(jnp.arange(rows), GRID_W).astype(jnp.float32)
    col = jnp.tile(jnp.arange(GRID_W), rows).astype(jnp.float32)
    half = A_HEAD_DIM // 2
    inv = ROPE_THETA ** (-jnp.arange(0, half, 2, dtype=jnp.float32) / half)
    ang_r = row[:, None] * inv[None, :]
    ang_c = col[:, None] * inv[None, :]
    return (jnp.cos(ang_r), jnp.sin(ang_r), jnp.cos(ang_c), jnp.sin(ang_c))


def rope_axis(x, cos, sin):
    d = x.shape[-1] // 2
    x1, x2 = x[..., :d], x[..., d:]
    return jnp.concatenate([x1 * cos - x2 * sin, x2 * cos + x1 * sin], axis=-1)


def apply_axial_rope(x, tabs):
    cr, sr, cc, sc = tabs
    half = A_HEAD_DIM // 2
    y = jnp.concatenate([rope_axis(x[..., :half], cr, sr), rope_axis(x[..., half:], cc, sc)], axis=-1)
    return y.astype(x.dtype)


def flip(t):
    return jnp.flip(t, axis=2)


def mlstm_inputs(p, b_gate):
    mq, mk, mv, mg = p[0], p[1], p[2], p[4]
    B, L, _ = mq.shape

    def to_heads(t):
        return t.reshape(B, L, M_HEADS, M_HEAD_DIM).transpose(0, 2, 1, 3).astype(jnp.float32)

    g = (mg.astype(jnp.float32) + b_gate.astype(jnp.float32)).reshape(B, L, 4, M_HEADS).transpose(2, 0, 3, 1)
    fwd = (g[0], jax.nn.log_sigmoid(g[1]))
    bwd = (g[2], jax.nn.log_sigmoid(g[3]))
    return to_heads(mq), to_heads(mk), to_heads(mv), fwd, bwd


def mlstm_init(B):
    return (jnp.zeros((B, M_HEADS, M_HEAD_DIM, M_HEAD_DIM), jnp.float32),
            jnp.zeros((B, M_HEADS, M_HEAD_DIM), jnp.float32),
            jnp.full((B, M_HEADS), M_INIT, jnp.float32))


def mlstm_scan(q, k, v, i_pre, logf, state):
    B, H, L, dh = q.shape
    nc = L // M_CHUNK

    def to_chunks(a):
        a = a.reshape(B, H, nc, M_CHUNK, *a.shape[3:])
        return jnp.moveaxis(a, 2, 0)

    xs = (to_chunks(q * (dh ** -0.5)), to_chunks(k), to_chunks(v), to_chunks(i_pre), to_chunks(logf))
    lower = jnp.tril(jnp.ones((M_CHUNK, M_CHUNK), dtype=bool))

    def step(carry, chunk):
        C, n, m = carry
        qc, kc, vc, ic, fc = chunk
        b = jnp.cumsum(fc, axis=-1)
        dmat = jnp.where(lower, b[..., :, None] - b[..., None, :] + ic[..., None, :], -jnp.inf)
        inter = b + m[..., None]
        m_q = jnp.maximum(inter, jnp.max(dmat, axis=-1))
        w_intra = jnp.exp(dmat - m_q[..., None])
        w_inter = jnp.exp(inter - m_q)
        s = jnp.einsum('bhsd,bhrd->bhsr', qc, kc) * w_intra
        num = w_inter[..., None] * jnp.einsum('bhsd,bhde->bhse', qc, C) + jnp.einsum('bhsr,bhre->bhse', s, vc)
        den = w_inter * jnp.einsum('bhsd,bhd->bhs', qc, n) + jnp.sum(s, axis=-1)
        h = num / jnp.maximum(jnp.abs(den), jnp.exp(-m_q))[..., None]
        b_last = b[..., -1]
        dec_in = b_last[..., None] - b + ic
        m_new = jnp.maximum(b_last + m, jnp.max(dec_in, axis=-1))
        w_old = jnp.exp(b_last + m - m_new)
        w_in = jnp.exp(dec_in - m_new[..., None])
        C_new = w_old[..., None, None] * C + jnp.einsum('bhr,bhrd,bhre->bhde', w_in, kc, vc)
        n_new = w_old[..., None] * n + jnp.einsum('bhr,bhrd->bhd', w_in, kc)
        return (C_new, n_new, m_new), h

    final, hs = lax.scan(step, state, xs)
    h = jnp.moveaxis(hs, 0, 2).reshape(B, H, L, dh)
    return h, final


def attn_inputs(p, q_norm_w, k_norm_w):
    aq, ak, av = p[5], p[6], p[7]
    B, L, _ = aq.shape
    q = rms(aq.reshape(B, L, A_KV_HEADS, A_GROUP, A_HEAD_DIM), q_norm_w).transpose(0, 2, 3, 1, 4)
    k = rms(ak.reshape(B, L, A_KV_HEADS, A_HEAD_DIM), k_norm_w).transpose(0, 2, 1, 3)
    v = av.reshape(B, L, A_KV_HEADS, A_HEAD_DIM).transpose(0, 2, 1, 3)
    return q, k, v


def block_attention(q, k, v):
    B, Hk, G, Lq, dh = q.shape
    nb = Lq // Q_BLOCK
    qb = jnp.moveaxis(q.reshape(B, Hk, G, nb, Q_BLOCK, dh), 3, 0)
    scale = dh ** -0.5

    def attend(qblk):
        s = jnp.einsum('bkgqd,bksd->bkgqs', qblk, k).astype(jnp.float32) * scale
        pr = jax.nn.softmax(s, axis=-1).astype(v.dtype)
        return jnp.einsum('bkgqs,bksd->bkgqd', pr, v)

    o = lax.map(attend, qb)
    o = jnp.moveaxis(o, 0, 3).reshape(B, Hk, G, Lq, dh)
    return o.transpose(0, 3, 1, 2, 4).reshape(B, Lq, A_Q_WIDTH)


def merge_branches(p, h_m, o_attn, mh_norm_w, w_bm, w_ba, w_o):
    dtype = o_attn.dtype
    B, L, _ = o_attn.shape
    hm = rms(h_m.transpose(0, 2, 1, 3)).reshape(B, L, M_WIDTH).astype(dtype) * mh_norm_w
    y_m = (jax.nn.sigmoid(p[3]) * hm) @ w_bm
    y_a = o_attn @ w_ba
    g = jax.nn.sigmoid(p[8]).reshape(B, L, N_BRANCH, D_MODEL)
    return (g[:, :, 0] * y_m + g[:, :, 1] * y_a) @ w_o


def hier_moe(h, w_rg, b_rg, w_re, b_re, w_g, w_u, w_d):
    B, L, D = h.shape
    hf = h.reshape(B * L, D)
    g_logits = (hf @ w_rg).astype(jnp.float32) + b_rg.astype(jnp.float32)
    g_sel = jnp.argmax(g_logits, axis=-1)
    p_grp = jnp.max(jax.nn.softmax(g_logits, axis=-1), axis=-1, keepdims=True)
    e_logits = ((hf @ w_re).astype(jnp.float32) + b_re.astype(jnp.float32)).reshape(-1, N_GROUPS, EXPERTS_PER_GROUP)
    e_grp = jnp.einsum('nge,ng->ne', e_logits, jax.nn.one_hot(g_sel, N_GROUPS, dtype=jnp.float32))
    top_v, top_i = lax.top_k(e_grp, TOP_K)
    w_top = jax.nn.softmax(top_v, axis=-1) * p_grp
    eid = g_sel[:, None] * EXPERTS_PER_GROUP + top_i
    combine = jnp.einsum('nk,nke->ne', w_top, jax.nn.one_hot(eid, N_EXPERTS, dtype=jnp.float32)).astype(h.dtype)

    def expert(acc, xs):
        wg, wu, wd, col = xs
        y = (jax.nn.silu(hf @ wg) * (hf @ wu)) @ wd
        return acc + col[:, None] * y, None

    out, _ = lax.scan(expert, jnp.zeros_like(hf), (w_g, w_u, w_d, combine.T))
    return out.reshape(B, L, D)


def setup_inputs(seed: int = 0) -> dict:
    key = jax.random.key(seed)
    ks = jax.random.split(key, 23)
    D = D_MODEL

    def nrm(k, shape, scale):
        return jax.random.normal(k, shape, jnp.float32) * scale

    f_base = jnp.linspace(3.0, 6.0, M_HEADS, dtype=jnp.float32)
    gate_base = jnp.stack([jnp.zeros_like(f_base), f_base, jnp.zeros_like(f_base), f_base]).reshape(-1)
    return {
        "x": nrm(ks[0], (BATCH, SEQ, D), 1.0),
        "c": nrm(ks[1], (BATCH, D), 1.0),
        "ctx": nrm(ks[2], (BATCH, CTX_LEN, D), 1.0),
        "c_ctx": nrm(ks[3], (D,), 1.0),
        "w_ada": nrm(ks[4], (DEPTH, D, 6 * D), 0.5 * D ** -0.5),
        "b_ada": nrm(ks[5], (DEPTH, 6 * D), 0.02),
        "norm1_w": 1.0 + nrm(ks[6], (DEPTH, D), 0.02),
        "w_in": nrm(ks[7], (DEPTH, D, IN_WIDTH), D ** -0.5),
        "b_mgate": gate_base[None, :] + nrm(ks[8], (DEPTH, 4 * M_HEADS), 0.1),
        "q_norm_w": 1.0 + nrm(ks[9], (DEPTH, A_HEAD_DIM), 0.02),
        "k_norm_w": 1.0 + nrm(ks[10], (DEPTH, A_HEAD_DIM), 0.02),
        "mh_norm_w": 1.0 + nrm(ks[11], (DEPTH, M_WIDTH), 0.02),
        "w_branch_m": nrm(ks[12], (DEPTH, M_WIDTH, D), M_WIDTH ** -0.5),
        "w_branch_a": nrm(ks[13], (DEPTH, A_Q_WIDTH, D), A_Q_WIDTH ** -0.5),
        "w_out": nrm(ks[14], (DEPTH, D, D), D ** -0.5),
        "norm2_w": 1.0 + nrm(ks[15], (DEPTH, D), 0.02),
        "w_rg": nrm(ks[16], (DEPTH, D, N_GROUPS), D ** -0.5),
        "b_rg": nrm(ks[17], (DEPTH, N_GROUPS), 0.01),
        "w_re": nrm(ks[18], (DEPTH, D, N_EXPERTS), D ** -0.5),
        "b_re": nrm(ks[19], (DEPTH, N_EXPERTS), 0.01),
        "w_e_gate": nrm(ks[20], (DEPTH, N_EXPERTS, D, D_EXPERT), D ** -0.5),
        "w_e_up": nrm(ks[21], (DEPTH, N_EXPERTS, D, D_EXPERT), D ** -0.5),
        "w_e_down": nrm(ks[22], (DEPTH, N_EXPERTS, D_EXPERT, D), D_EXPERT ** -0.5),
    }


def reference(x, c, ctx, c_ctx, w_ada, b_ada, norm1_w, w_in, b_mgate, q_norm_w, k_norm_w, mh_norm_w,
              w_branch_m, w_branch_a, w_out, norm2_w, w_rg, b_rg, w_re, b_re, w_e_gate, w_e_up, w_e_down):
    B, L, _ = x.shape
    rope = axial_rope_tables(L)
    h_lat, h_ctx = x, ctx
    for l in range(DEPTH):
        update_ctx = l < DEPTH - 1
        sh1, sc1, g1, sh2, sc2, g2 = ada_mod(c, w_ada[l], b_ada[l])
        csh1, csc1, cg1, csh2, csc2, cg2 = ada_mod(c_ctx, w_ada[l], b_ada[l])

        p_lat = split_proj(modulate(rms(h_lat, norm1_w[l]), sh1, sc1) @ w_in[l])
        p_ctx = split_proj(modulate(rms(h_ctx, norm1_w[l]), csh1, csc1) @ w_in[l])

        init = mlstm_init(B)
        cq, ck, cv, (ci_f, cf_f), (ci_b, cf_b) = mlstm_inputs(p_ctx, b_mgate[l])
        hc_f, st_f = mlstm_scan(cq, ck, cv, ci_f, cf_f, init)
        hc_b, st_b = mlstm_scan(flip(cq), flip(ck), flip(cv), flip(ci_b), flip(cf_b), init)
        lq, lk, lv, (li_f, lf_f), (li_b, lf_b) = mlstm_inputs(p_lat, b_mgate[l])
        hl_f, _ = mlstm_scan(lq, lk, lv, li_f, lf_f, st_f)
        hl_b, _ = mlstm_scan(flip(lq), flip(lk), flip(lv), flip(li_b), flip(lf_b), st_b)
        hm_lat = hl_f + flip(hl_b)

        cq_a, ck_a, cv_a = attn_inputs(p_ctx, q_norm_w[l], k_norm_w[l])
        lq_a, lk_a, lv_a = attn_inputs(p_lat, q_norm_w[l], k_norm_w[l])
        lq_a = apply_axial_rope(lq_a, rope)
        lk_a = apply_axial_rope(lk_a, rope)
        ha_lat = block_attention(lq_a, jnp.concatenate([lk_a, ck_a], axis=2), jnp.concatenate([lv_a, cv_a], axis=2))

        mix_lat = merge_branches(p_lat, hm_lat, ha_lat, mh_norm_w[l], w_branch_m[l], w_branch_a[l], w_out[l])
        h_lat_mid = h_lat + g1 * mix_lat

        if update_ctx:
            hm_ctx = hc_f + flip(hc_b)
            ha_ctx = block_attention(cq_a, ck_a, cv_a)
            mix_ctx = merge_branches(p_ctx, hm_ctx, ha_ctx, mh_norm_w[l], w_branch_m[l], w_branch_a[l], w_out[l])
            h_ctx = h_ctx + cg1 * mix_ctx
            f_ctx = modulate(rms(h_ctx, norm2_w[l]), csh2, csc2)
            h_ctx = h_ctx + cg2 * hier_moe(f_ctx, w_rg[l], b_rg[l], w_re[l], b_re[l], w_e_gate[l], w_e_up[l], w_e_down[l])

        f_lat = modulate(rms(h_lat_mid, norm2_w[l]), sh2, sc2)
        h_lat = h_lat_mid + g2 * hier_moe(f_lat, w_rg[l], b_rg[l], w_re[l], b_re[l], w_e_gate[l], w_e_up[l], w_e_down[l])
    return h_lat
```

```python
import functools

import jax
import jax.numpy as jnp
from jax import lax
from jax.experimental import pallas as pl
from jax.experimental.pallas import tpu as pltpu

F32 = jnp.float32
BF16 = jnp.bfloat16

EPS = 1e-6
GRID_W = 64
ROPE_THETA = 10000.0
HEAD_DIM = 128
M_HEADS = 8
A_HEADS = 8
A_KV_HEADS = 2
A_GROUP = A_HEADS // A_KV_HEADS
N_GROUPS = 4
EXPERTS_PER_GROUP = 8
N_EXPERTS = N_GROUPS * EXPERTS_PER_GROUP
M_INIT = -1e30
MLSTM_CHUNK = 128
GATE_ROWS = 8
ROUTER_LANES = 128
VMEM_LIMIT_BYTES = 48 * 1024 * 1024

NT_DIMS = (((1,), (1,)), ((), ()))


def _cparams(*sem):
    return pltpu.CompilerParams(dimension_semantics=sem, vmem_limit_bytes=VMEM_LIMIT_BYTES)


def _dot(a, b):
    return jnp.dot(a, b, preferred_element_type=F32)


def _dot_nt(a, b):
    return lax.dot_general(a, b, NT_DIMS, preferred_element_type=F32)


def _split3(x):
    hi = x.astype(BF16)
    r1 = x - hi.astype(F32)
    mid = r1.astype(BF16)
    lo = (r1 - mid.astype(F32)).astype(BF16)
    return hi, mid, lo


def _silu(x):
    return x * jax.nn.sigmoid(x)


def _ada_kernel(c_ref, w_ref, b_ref, o_ref):
    s = _silu(c_ref[...])
    parts = _split3(s)
    w = w_ref[...]
    w_hi = w.astype(BF16)
    w_lo = (w - w_hi.astype(F32)).astype(BF16)
    acc = _dot(parts[0], w_hi) + _dot(parts[1], w_hi) + _dot(parts[0], w_lo)
    o_ref[...] = acc + b_ref[...]


def _ada_mod(cc, w, b):
    rows, d = cc.shape
    n = w.shape[1]
    tn = min(n, 1536)
    return pl.pallas_call(
        _ada_kernel,
        grid=(n // tn,),
        in_specs=[
            pl.BlockSpec((rows, d), lambda j: (0, 0)),
            pl.BlockSpec((d, tn), lambda j: (0, j)),
            pl.BlockSpec((1, tn), lambda j: (0, j)),
        ],
        out_specs=pl.BlockSpec((rows, tn), lambda j: (0, j)),
        out_shape=jax.ShapeDtypeStruct((rows, n), F32),
        compiler_params=_cparams("parallel"),
        name="ada_mod",
    )(cc, w, b.reshape(1, n))


def _norm_mod_kernel(x_ref, mod_ref, nw_ref, o_ref):
    x = x_ref[0]
    var = jnp.mean(x * x, axis=-1, keepdims=True)
    y = x * lax.rsqrt(var + EPS) * nw_ref[...]
    sh = mod_ref[0, 0:1, :]
    sc = mod_ref[0, 1:2, :]
    o_ref[0] = (y * (1.0 + sc) + sh).astype(o_ref.dtype)


def _norm_mod(x, mod, nw, mod_row):
    B, L, D = x.shape
    tm = min(L, 512)
    return pl.pallas_call(
        _norm_mod_kernel,
        grid=(B, L // tm),
        in_specs=[
            pl.BlockSpec((1, tm, D), lambda b, i: (b, i, 0)),
            pl.BlockSpec((1, 6, D), lambda b, i: (mod_row(b), 0, 0)),
            pl.BlockSpec((1, D), lambda b, i: (0, 0)),
        ],
        out_specs=pl.BlockSpec((1, tm, D), lambda b, i: (b, i, 0)),
        out_shape=jax.ShapeDtypeStruct((B, L, D), BF16),
        compiler_params=_cparams("parallel", "parallel"),
        name="norm_mod",
    )(x, mod, nw.reshape(1, D))


def _proj_kernel(x_ref, w_ref, o_ref, *, act, scale):
    acc = _dot(x_ref[0], w_ref[...])
    if act == "sigmoid":
        acc = jax.nn.sigmoid(acc)
    if scale != 1.0:
        acc = acc * scale
    o_ref[0] = acc.astype(o_ref.dtype)


def _proj(xn, w, *, act="none", scale=1.0, name):
    B, L, D = xn.shape
    N = w.shape[1]
    tm = min(L, 512)
    tn = min(N, 1024)
    return pl.pallas_call(
        functools.partial(_proj_kernel, act=act, scale=scale),
        grid=(B, N // tn, L // tm),
        in_specs=[
            pl.BlockSpec((1, tm, D), lambda b, j, i: (b, i, 0)),
            pl.BlockSpec((D, tn), lambda b, j, i: (0, j)),
        ],
        out_specs=pl.BlockSpec((1, tm, tn), lambda b, j, i: (b, i, j)),
        out_shape=jax.ShapeDtypeStruct((B, L, N), BF16),
        compiler_params=_cparams("parallel", "parallel", "parallel"),
        name=name,
    )(xn, w)


def _proj_t_kernel(w_ref, x_ref, b_ref, o_ref, *, nchunk, chunk):
    acc = _dot_nt(w_ref[...], x_ref[0]) + b_ref[...]
    for c in range(nchunk):
        o_ref[0, c] = acc[:, c * chunk:(c + 1) * chunk].astype(o_ref.dtype)


def _proj_t(xn, wt, bias, *, chunk, dtype, name):
    B, L, D = xn.shape
    R = wt.shape[0]
    tm = min(L, 512)
    nchunk = tm // chunk
    return pl.pallas_call(
        functools.partial(_proj_t_kernel, nchunk=nchunk, chunk=chunk),
        grid=(B, L // tm),
        in_specs=[
            pl.BlockSpec((R, D), lambda b, i: (0, 0)),
            pl.BlockSpec((1, tm, D), lambda b, i: (b, i, 0)),
            pl.BlockSpec((R, 1), lambda b, i: (0, 0)),
        ],
        out_specs=pl.BlockSpec((1, nchunk, R, chunk), lambda b, i: (b, i, 0, 0)),
        out_shape=jax.ShapeDtypeStruct((B, L // chunk, R, chunk), dtype),
        compiler_params=_cparams("parallel", "parallel"),
        name=name,
    )(wt, xn, bias.reshape(R, 1))


def _proj_normrope_kernel(x_ref, w_ref, nw_ref, cos_ref, sin_ref, o_ref, *, nheads, scale, rope):
    acc = _dot(x_ref[0], w_ref[...])
    nw = nw_ref[...]
    if rope:
        cos = cos_ref[...]
        sin = sin_ref[...]
        lane = lax.broadcasted_iota(jnp.int32, (acc.shape[0], HEAD_DIM), 1)
        first = (lane % (HEAD_DIM // 2)) < (HEAD_DIM // 4)
    for h in range(nheads):
        a = acc[:, h * HEAD_DIM:(h + 1) * HEAD_DIM]
        var = jnp.mean(a * a, axis=-1, keepdims=True)
        y = a * lax.rsqrt(var + EPS) * nw
        if rope:
            partner = jnp.where(first, pltpu.roll(y, HEAD_DIM - HEAD_DIM // 4, 1), pltpu.roll(y, HEAD_DIM // 4, 1))
            y = y * cos + partner * sin
        if scale != 1.0:
            y = y * scale
        o_ref[0, :, h * HEAD_DIM:(h + 1) * HEAD_DIM] = y.astype(o_ref.dtype)


def _proj_normrope(xn, w, nw, cos, sin, *, scale, rope, name):
    B, L, D = xn.shape
    N = w.shape[1]
    nheads = N // HEAD_DIM
    tm = min(L, 512)
    return pl.pallas_call(
        functools.partial(_proj_normrope_kernel, nheads=nheads, scale=scale, rope=rope),
        grid=(B, L // tm),
        in_specs=[
            pl.BlockSpec((1, tm, D), lambda b, i: (b, i, 0)),
            pl.BlockSpec((D, N), lambda b, i: (0, 0)),
            pl.BlockSpec((1, HEAD_DIM), lambda b, i: (0, 0)),
            pl.BlockSpec((tm, HEAD_DIM), lambda b, i: (i, 0)),
            pl.BlockSpec((tm, HEAD_DIM), lambda b, i: (i, 0)),
        ],
        out_specs=pl.BlockSpec((1, tm, N), lambda b, i: (b, i, 0)),
        out_shape=jax.ShapeDtypeStruct((B, L, N), BF16),
        compiler_params=_cparams("parallel", "parallel"),
        name=name,
    )(xn, w, nw.reshape(1, HEAD_DIM), cos, sin)


def _rope_tables(L):
    rows = L // GRID_W
    row = jnp.repeat(jnp.arange(rows), GRID_W).astype(F32)
    col = jnp.tile(jnp.arange(GRID_W), rows).astype(F32)
    half = HEAD_DIM // 2
    inv = ROPE_THETA ** (-jnp.arange(0, half, 2, dtype=F32) / half)
    ang_r = row[:, None] * inv[None, :]
    ang_c = col[:, None] * inv[None, :]
    cr, sr, cc, sc = jnp.cos(ang_r), jnp.sin(ang_r), jnp.cos(ang_c), jnp.sin(ang_c)
    return jnp.concatenate([cr, cr, cc, cc], axis=-1), jnp.concatenate([-sr, sr, -sc, sc], axis=-1)


def _log_sigmoid(x):
    return jnp.minimum(x, 0.0) - jnp.log1p(jnp.exp(-jnp.abs(x)))


def _mlstm_kernel(q_ref, kt_ref, v_ref, g_ref, og_ref, kct_ref, vc_ref, gc_ref, nw_ref, o_ref,
                  hs_ref, cf_ref, cb_ref, *, nc, nctx):
    C = MLSTM_CHUNK
    D = HEAD_DIM
    row_i = lax.broadcasted_iota(jnp.int32, (C, C), 0)
    col_i = lax.broadcasted_iota(jnp.int32, (C, C), 1)
    le = row_i <= col_i
    ge = row_i >= col_i
    upper = jnp.where(le, 1.0, 0.0).astype(BF16)
    lower = jnp.where(ge, 1.0, 0.0).astype(BF16)
    lane = lax.broadcasted_iota(jnp.int32, (C, D), 1)
    ones_col = jnp.where(lane == 0, 1.0, 0.0).astype(BF16)

    def prep(g8, fwd):
        irow, frow = (0, 1) if fwd else (2, 3)
        parts = _split3(_log_sigmoid(g8))
        tri_row, tri_col = (upper, lower) if fwd else (lower, upper)
        b_rows = _dot(parts[0], tri_row) + _dot(parts[1], tri_row) + _dot(parts[2], tri_row)
        b_cols = _dot_nt(tri_col, parts[0]) + _dot_nt(tri_col, parts[1]) + _dot_nt(tri_col, parts[2])
        b_row = b_rows[frow:frow + 1, :]
        a_row = g8[irow:irow + 1, :] - b_row
        btot = b_row[:, C - 1:C] if fwd else b_row[:, 0:1]
        return a_row, -b_cols[:, frow:frow + 1], btot

    def state_update(c_ref, m_prev, kt, vaug, a_row, btot):
        mc = jnp.maximum(jnp.max(a_row, axis=1, keepdims=True), m_prev)
        w_old = jnp.exp(m_prev - mc)
        w_in = jnp.exp(a_row - mc)
        ktw = (kt.astype(F32) * w_in).astype(BF16)
        c_ref[...] = w_old * c_ref[...] + _dot(ktw, vaug)
        return btot + mc

    def out_step(c_ref, m_prev, q, kt, vaug, a_row, nb_col, mask):
        a_mat = jnp.where(mask, a_row, -jnp.inf)
        m_q = jnp.maximum(jnp.max(a_mat, axis=1, keepdims=True), m_prev)
        w_intra = jnp.exp(a_mat - m_q)
        s = _dot(q, kt)
        p = (s * w_intra).astype(BF16)
        qc = _dot(q, c_ref[...].astype(BF16))
        pv = _dot(p, vaug)
        nd = jnp.exp(m_prev - m_q) * qc + pv
        den = nd[:, D:D + 1]
        return nd[:, :D] / jnp.maximum(jnp.abs(den), jnp.exp(nb_col - m_q))

    def vaug_of(v):
        return jnp.concatenate([v, ones_col], axis=1)

    cf_ref[...] = jnp.zeros_like(cf_ref)
    cb_ref[...] = jnp.zeros_like(cb_ref)
    hs_ref[...] = jnp.zeros_like(hs_ref)
    m0 = jnp.full((1, 1), M_INIT, F32)

    def ctx_body(j, carry):
        mf, mb = carry
        jb = nctx - 1 - j
        a_f, _, bt_f = prep(gc_ref[0, j], True)
        mf = state_update(cf_ref, mf, kct_ref[0, j], vaug_of(vc_ref[0, pl.ds(pl.multiple_of(j * C, C), C), :]), a_f, bt_f)
        a_b, _, bt_b = prep(gc_ref[0, jb], False)
        mb = state_update(cb_ref, mb, kct_ref[0, jb], vaug_of(vc_ref[0, pl.ds(pl.multiple_of(jb * C, C), C), :]), a_b, bt_b)
        return mf, mb

    carry = lax.fori_loop(0, nctx, ctx_body, (m0, m0))

    def lat_body(j, carry):
        mf, mb = carry
        jb = nc - 1 - j
        sl_f = pl.ds(pl.multiple_of(j * C, C), C)
        sl_b = pl.ds(pl.multiple_of(jb * C, C), C)
        a_f, nb_f, bt_f = prep(g_ref[0, j], True)
        kt_f = kt_ref[0, j]
        va_f = vaug_of(v_ref[0, sl_f, :])
        hs_ref[sl_f, :] += out_step(cf_ref, mf, q_ref[0, sl_f, :], kt_f, va_f, a_f, nb_f, ge)
        mf = state_update(cf_ref, mf, kt_f, va_f, a_f, bt_f)
        a_b, nb_b, bt_b = prep(g_ref[0, jb], False)
        kt_b = kt_ref[0, jb]
        va_b = vaug_of(v_ref[0, sl_b, :])
        hs_ref[sl_b, :] += out_step(cb_ref, mb, q_ref[0, sl_b, :], kt_b, va_b, a_b, nb_b, le)
        mb = state_update(cb_ref, mb, kt_b, va_b, a_b, bt_b)
        return mf, mb

    lax.fori_loop(0, nc, lat_body, carry)

    nw = nw_ref[...]

    def norm_body(j, _):
        sl = pl.ds(pl.multiple_of(j * C, C), C)
        h = hs_ref[sl, :]
        var = jnp.mean(h * h, axis=-1, keepdims=True)
        y = h * lax.rsqrt(var + EPS) * nw
        o_ref[0, sl, :] = (og_ref[0, sl, :].astype(F32) * y).astype(o_ref.dtype)
        return 0

    lax.fori_loop(0, nc, norm_body, 0)


def _mlstm(q, kt, v, g, og, kct, vc, gc, mh_norm_w):
    B, L, W = q.shape
    C = MLSTM_CHUNK
    nc = L // C
    lctx = vc.shape[1]
    nctx = lctx // C
    H = W // HEAD_DIM
    return pl.pallas_call(
        functools.partial(_mlstm_kernel, nc=nc, nctx=nctx),
        grid=(B, H),
        in_specs=[
            pl.BlockSpec((1, L, HEAD_DIM), lambda b, h: (b, 0, h)),
            pl.BlockSpec((1, nc, HEAD_DIM, C), lambda b, h: (b, 0, h, 0)),
            pl.BlockSpec((1, L, HEAD_DIM), lambda b, h: (b, 0, h)),
            pl.BlockSpec((1, nc, GATE_ROWS, C), lambda b, h: (b, 0, h, 0)),
            pl.BlockSpec((1, L, HEAD_DIM), lambda b, h: (b, 0, h)),
            pl.BlockSpec((1, nctx, HEAD_DIM, C), lambda b, h: (b, 0, h, 0)),
            pl.BlockSpec((1, lctx, HEAD_DIM), lambda b, h: (b, 0, h)),
            pl.BlockSpec((1, nctx, GATE_ROWS, C), lambda b, h: (b, 0, h, 0)),
            pl.BlockSpec((1, HEAD_DIM), lambda b, h: (0, h)),
        ],
        out_specs=pl.BlockSpec((1, L, HEAD_DIM), lambda b, h: (b, 0, h)),
        out_shape=jax.ShapeDtypeStruct((B, L, W), BF16),
        scratch_shapes=[
            pltpu.VMEM((L, HEAD_DIM), F32),
            pltpu.VMEM((HEAD_DIM, 2 * HEAD_DIM), F32),
            pltpu.VMEM((HEAD_DIM, 2 * HEAD_DIM), F32),
        ],
        compiler_params=_cparams("parallel", "parallel"),
        name="mlstm",
    )(q, kt, v, g, og, kct, vc, gc, mh_norm_w.reshape(1, W))


def _attn_kernel(q_ref, k_ref, v_ref, kc_ref, vc_ref, o_ref):
    k = k_ref[0]
    v = v_ref[0]
    kc = kc_ref[0]
    vc = vc_ref[0]
    for h in range(A_GROUP):
        q = q_ref[0, :, h * HEAD_DIM:(h + 1) * HEAD_DIM]
        s1 = _dot_nt(q, k)
        s2 = _dot_nt(q, kc)
        m = jnp.maximum(jnp.max(s1, axis=1, keepdims=True), jnp.max(s2, axis=1, keepdims=True))
        p1 = jnp.exp(s1 - m)
        p2 = jnp.exp(s2 - m)
        l = jnp.sum(p1, axis=1, keepdims=True) + jnp.sum(p2, axis=1, keepdims=True)
        o = _dot(p1.astype(BF16), v) + _dot(p2.astype(BF16), vc)
        o_ref[0, :, h * HEAD_DIM:(h + 1) * HEAD_DIM] = (o / l).astype(o_ref.dtype)


def _attention(q, k, v, kc, vc):
    B, L, W = q.shape
    lctx = kc.shape[1]
    tq = min(L, 256)
    gw = A_GROUP * HEAD_DIM
    return pl.pallas_call(
        _attn_kernel,
        grid=(B, A_KV_HEADS, L // tq),
        in_specs=[
            pl.BlockSpec((1, tq, gw), lambda b, g, i: (b, i, g)),
            pl.BlockSpec((1, L, HEAD_DIM), lambda b, g, i: (b, 0, g)),
            pl.BlockSpec((1, L, HEAD_DIM), lambda b, g, i: (b, 0, g)),
            pl.BlockSpec((1, lctx, HEAD_DIM), lambda b, g, i: (b, 0, g)),
            pl.BlockSpec((1, lctx, HEAD_DIM), lambda b, g, i: (b, 0, g)),
        ],
        out_specs=pl.BlockSpec((1, tq, gw), lambda b, g, i: (b, i, g)),
        out_shape=jax.ShapeDtypeStruct((B, L, W), BF16),
        compiler_params=_cparams("parallel", "parallel", "parallel"),
        name="attention",
    )(q, k, v, kc, vc)


def _merge_kernel(ym_ref, oa_ref, bg_ref, x_ref, mod_ref, wbm_ref, wba_ref, wo_ref, n2_ref, wr_ref, br_ref,
                  hmid_ref, f_ref, comb_ref):
    D = x_ref.shape[-1]
    ym = _dot(ym_ref[0], wbm_ref[...])
    ya = _dot(oa_ref[0], wba_ref[...])
    g0 = bg_ref[0, :, :D].astype(F32)
    g1 = bg_ref[0, :, D:].astype(F32)
    mix = _dot((g0 * ym + g1 * ya).astype(BF16), wo_ref[...])
    hmid = x_ref[0] + mod_ref[0, 2:3, :] * mix
    hmid_ref[0] = hmid
    var = jnp.mean(hmid * hmid, axis=-1, keepdims=True)
    y = hmid * lax.rsqrt(var + EPS) * n2_ref[...]
    f = y * (1.0 + mod_ref[0, 4:5, :]) + mod_ref[0, 3:4, :]
    f_ref[0] = f.astype(f_ref.dtype)

    fp = _split3(f)
    wr = wr_ref[...]
    wr_hi = wr.astype(BF16)
    wr_lo = (wr - wr_hi.astype(F32)).astype(BF16)
    logits = _dot(fp[0], wr_hi) + _dot(fp[1], wr_hi) + _dot(fp[0], wr_lo) + _dot(fp[2], wr_hi) + _dot(fp[1], wr_lo)
    logits = logits + br_ref[...]
    lane = lax.broadcasted_iota(jnp.int32, logits.shape, 1)
    big = jnp.int32(ROUTER_LANES)
    neg = -jnp.inf
    is_grp = lane < N_GROUPS
    gl = jnp.where(is_grp, logits, neg)
    gmax = jnp.max(gl, axis=1, keepdims=True)
    gsel = jnp.min(jnp.where(gl == gmax, lane, big), axis=1, keepdims=True)
    p_grp = 1.0 / jnp.sum(jnp.where(is_grp, jnp.exp(logits - gmax), 0.0), axis=1, keepdims=True)
    lo = N_GROUPS + gsel * EXPERTS_PER_GROUP
    in_grp = (lane >= lo) & (lane < lo + EXPERTS_PER_GROUP)
    el = jnp.where(in_grp, logits, neg)
    v1 = jnp.max(el, axis=1, keepdims=True)
    i1 = jnp.min(jnp.where(el == v1, lane, big), axis=1, keepdims=True)
    el2 = jnp.where(lane == i1, neg, el)
    v2 = jnp.max(el2, axis=1, keepdims=True)
    i2 = jnp.min(jnp.where(el2 == v2, lane, big), axis=1, keepdims=True)
    e21 = jnp.exp(v2 - v1)
    w1 = p_grp / (1.0 + e21)
    w2 = p_grp * e21 / (1.0 + e21)
    comb_ref[0] = jnp.where(lane == i1, w1, 0.0) + jnp.where(lane == i2, w2, 0.0)


def _merge(ym, oa, bg, x, mod, wbm, wba, wo, n2, wr, br):
    B, L, D = x.shape
    tm = min(L, 512)
    full = lambda b, i: (0, 0)
    tok = lambda b, i: (b, i, 0)
    return pl.pallas_call(
        _merge_kernel,
        grid=(B, L // tm),
        in_specs=[
            pl.BlockSpec((1, tm, D), tok),
            pl.BlockSpec((1, tm, D), tok),
            pl.BlockSpec((1, tm, 2 * D), tok),
            pl.BlockSpec((1, tm, D), tok),
            pl.BlockSpec((1, 6, D), lambda b, i: (b, 0, 0)),
            pl.BlockSpec((D, D), full),
            pl.BlockSpec((D, D), full),
            pl.BlockSpec((D, D), full),
            pl.BlockSpec((1, D), full),
            pl.BlockSpec((D, ROUTER_LANES), full),
            pl.BlockSpec((1, ROUTER_LANES), full),
        ],
        out_specs=[
            pl.BlockSpec((1, tm, D), tok),
            pl.BlockSpec((1, tm, D), tok),
            pl.BlockSpec((1, tm, ROUTER_LANES), tok),
        ],
        out_shape=[
            jax.ShapeDtypeStruct((B, L, D), F32),
            jax.ShapeDtypeStruct((B, L, D), BF16),
            jax.ShapeDtypeStruct((B, L, ROUTER_LANES), F32),
        ],
        compiler_params=_cparams("parallel", "parallel"),
        name="merge_router",
    )(ym, oa, bg, x, mod, wbm, wba, wo, n2.reshape(1, D), wr, br)


def _moe_dense_kernel(f_ref, comb_ref, hmid_ref, mod_ref, wg_ref, wu_ref, wd_ref, o_ref, acc_ref):
    e = pl.program_id(2)

    @pl.when(e == 0)
    def _():
        acc_ref[...] = jnp.zeros_like(acc_ref)

    f = f_ref[0]
    hidden = (_silu(_dot(f, wg_ref[0])) * _dot(f, wu_ref[0])).astype(BF16)
    y = _dot(hidden, wd_ref[0])
    comb = comb_ref[0]
    lane = lax.broadcasted_iota(jnp.int32, comb.shape, 1)
    col = jnp.sum(jnp.where(lane == e + N_GROUPS, comb, 0.0), axis=1, keepdims=True)
    acc_ref[...] += col * y

    @pl.when(e == pl.num_programs(2) - 1)
    def _():
        o_ref[0] = hmid_ref[0] + mod_ref[0, 5:6, :] * acc_ref[...]


def _moe_dense(f, comb, hmid, mod, wg, wu, wd):
    B, L, D = hmid.shape
    E, _, DE = wg.shape
    tm = min(L, 1024)
    tok = lambda b, i, e: (b, i, 0)
    return pl.pallas_call(
        _moe_dense_kernel,
        grid=(B, L // tm, E),
        in_specs=[
            pl.BlockSpec((1, tm, D), tok),
            pl.BlockSpec((1, tm, ROUTER_LANES), tok),
            pl.BlockSpec((1, tm, D), tok),
            pl.BlockSpec((1, 6, D), lambda b, i, e: (b, 0, 0)),
            pl.BlockSpec((1, D, DE), lambda b, i, e: (e, 0, 0)),
            pl.BlockSpec((1, D, DE), lambda b, i, e: (e, 0, 0)),
            pl.BlockSpec((1, DE, D), lambda b, i, e: (e, 0, 0)),
        ],
        out_specs=pl.BlockSpec((1, tm, D), tok),
        out_shape=jax.ShapeDtypeStruct((B, L, D), F32),
        scratch_shapes=[pltpu.VMEM((tm, D), F32)],
        compiler_params=_cparams("parallel", "parallel", "arbitrary"),
        name="moe_dense",
    )(f, comb, hmid, mod, wg, wu, wd)


def _gate_weights(w_gate, b_gate):
    D = w_gate.shape[0]
    wt = w_gate.T.reshape(4, M_HEADS, D).transpose(1, 0, 2)
    wt = jnp.concatenate([wt, jnp.zeros((M_HEADS, GATE_ROWS - 4, D), wt.dtype)], axis=1).reshape(M_HEADS * GATE_ROWS, D)
    bt = b_gate.reshape(4, M_HEADS).T
    bt = jnp.concatenate([bt, jnp.zeros((M_HEADS, GATE_ROWS - 4), bt.dtype)], axis=1).reshape(M_HEADS * GATE_ROWS)
    return wt, bt


def kernel(x, c, ctx, c_ctx, w_ada, b_ada, norm1_w, w_in, b_mgate, q_norm_w, k_norm_w, mh_norm_w, w_branch_m,
           w_branch_a, w_out, norm2_w, w_rg, b_rg, w_re, b_re, w_e_gate, w_e_up, w_e_down):
    B, L, D = x.shape
    depth = w_ada.shape[0]
    assert depth == 1, "context-stream update between layers is not implemented"
    l = 0
    mw = M_HEADS * HEAD_DIM
    aq = A_HEADS * HEAD_DIM
    akv = A_KV_HEADS * HEAD_DIM
    o_mq, o_mk, o_mv, o_og = 0, mw, 2 * mw, 3 * mw
    o_g = 4 * mw
    o_aq = o_g + 4 * M_HEADS
    o_ak = o_aq + aq
    o_av = o_ak + akv
    o_bg = o_av + akv
    scale = HEAD_DIM ** -0.5

    rows = ((B + 1 + 7) // 8) * 8
    cc = jnp.concatenate([c, c_ctx[None, :], jnp.zeros((rows - B - 1, D), F32)], axis=0)
    mod = _ada_mod(cc, w_ada[l], b_ada[l]).reshape(rows, 6, D)

    xn = _norm_mod(x, mod, norm1_w[l], lambda b: b)
    cn = _norm_mod(ctx, mod, norm1_w[l], lambda b: B)

    w = w_in[l]
    wb = lambda a, b: w[:, a:b].astype(BF16)
    cos, sin = _rope_tables(L)
    zero_r = lambda r: jnp.zeros((r,), F32)
    wgt, bgt = _gate_weights(w[:, o_g:o_aq], b_mgate[l])
    wgt = wgt.astype(BF16)
    wkt = w[:, o_mk:o_mv].T.astype(BF16)

    mq = _proj(xn, wb(o_mq, o_mk), scale=scale, name="proj_mq")
    mkt = _proj_t(xn, wkt, zero_r(mw), chunk=MLSTM_CHUNK, dtype=BF16, name="proj_mkt")
    mv = _proj(xn, wb(o_mv, o_og), name="proj_mv")
    og = _proj(xn, wb(o_og, o_g), act="sigmoid", name="proj_og")
    gt = _proj_t(xn, wgt, bgt, chunk=MLSTM_CHUNK, dtype=F32, name="proj_gates")
    qa = _proj_normrope(xn, wb(o_aq, o_ak), q_norm_w[l], cos, sin, scale=scale, rope=True, name="proj_aq")
    ka = _proj_normrope(xn, wb(o_ak, o_av), k_norm_w[l], cos, sin, scale=1.0, rope=True, name="proj_ak")
    va = _proj(xn, wb(o_av, o_bg), name="proj_av")
    bg = _proj(xn, wb(o_bg, o_bg + 2 * D), act="sigmoid", name="proj_bg")

    lctx = ctx.shape[1]
    cmkt = _proj_t(cn, wkt, zero_r(mw), chunk=MLSTM_CHUNK, dtype=BF16, name="proj_cmkt")
    cmv = _proj(cn, wb(o_mv, o_og), name="proj_cmv")
    cgt = _proj_t(cn, wgt, bgt, chunk=MLSTM_CHUNK, dtype=F32, name="proj_cgates")
    cka = _proj_normrope(cn, wb(o_ak, o_av), k_norm_w[l], cos[:lctx], sin[:lctx], scale=1.0, rope=False, name="proj_cak")
    cva = _proj(cn, wb(o_av, o_bg), name="proj_cav")

    ym = _mlstm(mq, mkt, mv, gt, og, cmkt, cmv, cgt, mh_norm_w[l])
    oa = _attention(qa, ka, va, cka, cva)

    wr = jnp.concatenate([w_rg[l], w_re[l], jnp.zeros((D, ROUTER_LANES - N_GROUPS - N_EXPERTS), F32)], axis=1)
    br = jnp.concatenate([b_rg[l], b_re[l], jnp.zeros((ROUTER_LANES - N_GROUPS - N_EXPERTS,), F32)]).reshape(1, -1)
    hmid, f, comb = _merge(ym, oa, bg, x, mod, w_branch_m[l].astype(BF16), w_branch_a[l].astype(BF16),
                           w_out[l].astype(BF16), norm2_w[l], wr, br)

    return _moe_dense(f, comb, hmid, mod, w_e_gate[l].astype(BF16), w_e_up[l].astype(BF16), w_e_down[l].astype(BF16))
```

```python
import functools

import jax
import jax.numpy as jnp
from jax import lax
from jax.experimental import pallas as pl
from jax.experimental.pallas import tpu as pltpu

F32 = jnp.float32
BF16 = jnp.bfloat16

EPS = 1e-6
GRID_W = 64
ROPE_THETA = 10000.0
HEAD_DIM = 128
M_HEADS = 8
A_HEADS = 8
A_KV_HEADS = 2
A_GROUP = A_HEADS // A_KV_HEADS
N_GROUPS = 4
EXPERTS_PER_GROUP = 8
N_EXPERTS = N_GROUPS * EXPERTS_PER_GROUP
M_INIT = -1e30
MLSTM_CHUNK = 128
GATE_ROWS = 8
ROUTER_LANES = 128
VMEM_LIMIT_BYTES = 48 * 1024 * 1024

NT_DIMS = (((1,), (1,)), ((), ()))


def _cparams(*sem):
    return pltpu.CompilerParams(dimension_semantics=sem, vmem_limit_bytes=VMEM_LIMIT_BYTES)


def _dot(a, b):
    return jnp.dot(a, b, preferred_element_type=F32)


def _dot_nt(a, b):
    return lax.dot_general(a, b, NT_DIMS, preferred_element_type=F32)


def _split3(x):
    hi = x.astype(BF16)
    r1 = x - hi.astype(F32)
    mid = r1.astype(BF16)
    lo = (r1 - mid.astype(F32)).astype(BF16)
    return hi, mid, lo


def _silu(x):
    return x * jax.nn.sigmoid(x)


def _ada_kernel(c_ref, w_ref, b_ref, o_ref):
    s = _silu(c_ref[...])
    parts = _split3(s)
    w = w_ref[...]
    w_hi = w.astype(BF16)
    w_lo = (w - w_hi.astype(F32)).astype(BF16)
    acc = _dot(parts[0], w_hi) + _dot(parts[1], w_hi) + _dot(parts[0], w_lo)
    o_ref[...] = acc + b_ref[...]


def _ada_mod(cc, w, b):
    rows, d = cc.shape
    n = w.shape[1]
    tn = min(n, 1536)
    return pl.pallas_call(
        _ada_kernel,
        grid=(n // tn,),
        in_specs=[
            pl.BlockSpec((rows, d), lambda j: (0, 0)),
            pl.BlockSpec((d, tn), lambda j: (0, j)),
            pl.BlockSpec((1, tn), lambda j: (0, j)),
        ],
        out_specs=pl.BlockSpec((rows, tn), lambda j: (0, j)),
        out_shape=jax.ShapeDtypeStruct((rows, n), F32),
        compiler_params=_cparams("parallel"),
        name="ada_mod",
    )(cc, w, b.reshape(1, n))


def _norm_mod_kernel(x_ref, mod_ref, nw_ref, o_ref):
    x = x_ref[0]
    var = jnp.mean(x * x, axis=-1, keepdims=True)
    y = x * lax.rsqrt(var + EPS) * nw_ref[...]
    sh = mod_ref[0, 0:1, :]
    sc = mod_ref[0, 1:2, :]
    o_ref[0] = (y * (1.0 + sc) + sh).astype(o_ref.dtype)


def _norm_mod(x, mod, nw, mod_row):
    B, L, D = x.shape
    tm = min(L, 512)
    return pl.pallas_call(
        _norm_mod_kernel,
        grid=(B, L // tm),
        in_specs=[
            pl.BlockSpec((1, tm, D), lambda b, i: (b, i, 0)),
            pl.BlockSpec((1, 6, D), lambda b, i: (mod_row(b), 0, 0)),
            pl.BlockSpec((1, D), lambda b, i: (0, 0)),
        ],
        out_specs=pl.BlockSpec((1, tm, D), lambda b, i: (b, i, 0)),
        out_shape=jax.ShapeDtypeStruct((B, L, D), BF16),
        compiler_params=_cparams("parallel", "parallel"),
        name="norm_mod",
    )(x, mod, nw.reshape(1, D))


def _proj_kernel(x_ref, w_ref, o_ref, *, act, scale):
    acc = _dot(x_ref[0], w_ref[...])
    if act == "sigmoid":
        acc = jax.nn.sigmoid(acc)
    if scale != 1.0:
        acc = acc * scale
    o_ref[0] = acc.astype(o_ref.dtype)


def _proj(xn, w, *, act="none", scale=1.0, name):
    B, L, D = xn.shape
    N = w.shape[1]
    tm = min(L, 512)
    tn = min(N, 1024)
    return pl.pallas_call(
        functools.partial(_proj_kernel, act=act, scale=scale),
        grid=(B, N // tn, L // tm),
        in_specs=[
            pl.BlockSpec((1, tm, D), lambda b, j, i: (b, i, 0)),
            pl.BlockSpec((D, tn), lambda b, j, i: (0, j)),
        ],
        out_specs=pl.BlockSpec((1, tm, tn), lambda b, j, i: (b, i, j)),
        out_shape=jax.ShapeDtypeStruct((B, L, N), BF16),
        compiler_params=_cparams("parallel", "parallel", "parallel"),
        name=name,
    )(xn, w)


def _proj_t_kernel(w_ref, x_ref, o_ref, *, nchunk, chunk):
    acc = _dot_nt(w_ref[...], x_ref[0])
    for c in range(nchunk):
        o_ref[0, c] = acc[:, c * chunk:(c + 1) * chunk].astype(o_ref.dtype)


def _proj_t(xn, wt, *, chunk, name):
    B, L, D = xn.shape
    R = wt.shape[0]
    tm = min(L, 512)
    nchunk = tm // chunk
    return pl.pallas_call(
        functools.partial(_proj_t_kernel, nchunk=nchunk, chunk=chunk),
        grid=(B, L // tm),
        in_specs=[
            pl.BlockSpec((R, D), lambda b, i: (0, 0)),
            pl.BlockSpec((1, tm, D), lambda b, i: (b, i, 0)),
        ],
        out_specs=pl.BlockSpec((1, nchunk, R, chunk), lambda b, i: (b, i, 0, 0)),
        out_shape=jax.ShapeDtypeStruct((B, L // chunk, R, chunk), BF16),
        compiler_params=_cparams("parallel", "parallel"),
        name=name,
    )(wt, xn)


def _proj_gates_kernel(w_ref, x_ref, b_ref, o_ref, *, nchunk, chunk):
    acc = _dot_nt(w_ref[...], x_ref[0]) + b_ref[...]
    for h in range(M_HEADS):
        for c in range(nchunk):
            o_ref[0, h, c] = acc[h * GATE_ROWS:(h + 1) * GATE_ROWS, c * chunk:(c + 1) * chunk]


def _proj_gates(xn, wt, bias, *, chunk, name):
    B, L, D = xn.shape
    R = wt.shape[0]
    tm = min(L, 512)
    nchunk = tm // chunk
    return pl.pallas_call(
        functools.partial(_proj_gates_kernel, nchunk=nchunk, chunk=chunk),
        grid=(B, L // tm),
        in_specs=[
            pl.BlockSpec((R, D), lambda b, i: (0, 0)),
            pl.BlockSpec((1, tm, D), lambda b, i: (b, i, 0)),
            pl.BlockSpec((R, 1), lambda b, i: (0, 0)),
        ],
        out_specs=pl.BlockSpec((1, M_HEADS, nchunk, GATE_ROWS, chunk), lambda b, i: (b, 0, i, 0, 0)),
        out_shape=jax.ShapeDtypeStruct((B, M_HEADS, L // chunk, GATE_ROWS, chunk), F32),
        compiler_params=_cparams("parallel", "parallel"),
        name=name,
    )(wt, xn, bias.reshape(R, 1))


def _proj_normrope_kernel(x_ref, w_ref, nw_ref, cos_ref, sin_ref, o_ref, *, nheads, scale, rope):
    acc = _dot(x_ref[0], w_ref[...])
    nw = nw_ref[...]
    if rope:
        cos = cos_ref[...]
        sin = sin_ref[...]
        lane = lax.broadcasted_iota(jnp.int32, (acc.shape[0], HEAD_DIM), 1)
        first = (lane % (HEAD_DIM // 2)) < (HEAD_DIM // 4)
    for h in range(nheads):
        a = acc[:, h * HEAD_DIM:(h + 1) * HEAD_DIM]
        var = jnp.mean(a * a, axis=-1, keepdims=True)
        y = a * lax.rsqrt(var + EPS) * nw
        if rope:
            partner = jnp.where(first, pltpu.roll(y, HEAD_DIM - HEAD_DIM // 4, 1), pltpu.roll(y, HEAD_DIM // 4, 1))
            y = y * cos + partner * sin
        if scale != 1.0:
            y = y * scale
        o_ref[0, :, h * HEAD_DIM:(h + 1) * HEAD_DIM] = y.astype(o_ref.dtype)


def _proj_normrope(xn, w, nw, cos, sin, *, scale, rope, name):
    B, L, D = xn.shape
    N = w.shape[1]
    nheads = N // HEAD_DIM
    tm = min(L, 512)
    return pl.pallas_call(
        functools.partial(_proj_normrope_kernel, nheads=nheads, scale=scale, rope=rope),
        grid=(B, L // tm),
        in_specs=[
            pl.BlockSpec((1, tm, D), lambda b, i: (b, i, 0)),
            pl.BlockSpec((D, N), lambda b, i: (0, 0)),
            pl.BlockSpec((1, HEAD_DIM), lambda b, i: (0, 0)),
            pl.BlockSpec((tm, HEAD_DIM), lambda b, i: (i, 0)),
            pl.BlockSpec((tm, HEAD_DIM), lambda b, i: (i, 0)),
        ],
        out_specs=pl.BlockSpec((1, tm, N), lambda b, i: (b, i, 0)),
        out_shape=jax.ShapeDtypeStruct((B, L, N), BF16),
        compiler_params=_cparams("parallel", "parallel"),
        name=name,
    )(xn, w, nw.reshape(1, HEAD_DIM), cos, sin)


def _rope_tables(L):
    rows = L // GRID_W
    row = jnp.repeat(jnp.arange(rows), GRID_W).astype(F32)
    col = jnp.tile(jnp.arange(GRID_W), rows).astype(F32)
    half = HEAD_DIM // 2
    inv = ROPE_THETA ** (-jnp.arange(0, half, 2, dtype=F32) / half)
    ang_r = row[:, None] * inv[None, :]
    ang_c = col[:, None] * inv[None, :]
    cr, sr, cc, sc = jnp.cos(ang_r), jnp.sin(ang_r), jnp.cos(ang_c), jnp.sin(ang_c)
    return jnp.concatenate([cr, cr, cc, cc], axis=-1), jnp.concatenate([-sr, sr, -sc, sc], axis=-1)


def _log_sigmoid(x):
    return jnp.minimum(x, 0.0) - jnp.log1p(jnp.exp(-jnp.abs(x)))


def _tri_masks(n):
    row_i = lax.broadcasted_iota(jnp.int32, (n, n), 0)
    col_i = lax.broadcasted_iota(jnp.int32, (n, n), 1)
    return row_i <= col_i, row_i >= col_i


def _gate_prep_kernel(g_ref, a_ref):
    _, H, nc, R8, C = g_ref.shape
    rows = H * nc * R8
    g = g_ref[0].reshape(rows, C)
    le, ge = _tri_masks(C)
    upper = jnp.where(le, 1.0, 0.0).astype(BF16)
    lower = jnp.where(ge, 1.0, 0.0).astype(BF16)
    lf = _log_sigmoid(g)
    parts = _split3(lf)
    pre = _dot(parts[0], upper) + _dot(parts[1], upper) + _dot(parts[2], upper)
    suf = _dot(parts[0], lower) + _dot(parts[1], lower) + _dot(parts[2], lower)
    rtype = lax.broadcasted_iota(jnp.int32, (rows, C), 0) % R8
    a = jnp.where(rtype == 1, pre, jnp.where(rtype == 3, suf, jnp.where(rtype >= 4, lf, g)))
    a_ref[0] = a.reshape(H, nc, R8, C)


def _gate_prep(g, name):
    B, H, nc, R8, C = g.shape
    blk = pl.BlockSpec((1, H, nc, R8, C), lambda b: (b, 0, 0, 0, 0))
    return pl.pallas_call(
        _gate_prep_kernel,
        grid=(B,),
        in_specs=[blk],
        out_specs=blk,
        out_shape=jax.ShapeDtypeStruct(g.shape, F32),
        compiler_params=_cparams("parallel"),
        name=name,
    )(g)


def _mlstm_kernel(q_ref, kt_ref, v_ref, a_ref, og_ref, kct_ref, vc_ref, ac_ref, nw_ref, o_ref, cin_ref,
                  *, nc, nctx):
    C = MLSTM_CHUNK
    D = HEAD_DIM
    le, ge = _tri_masks(C)
    le_bf = jnp.where(le, 1.0, 0.0).astype(BF16)
    ge_bf = jnp.where(ge, 1.0, 0.0).astype(BF16)
    ones_blk = jnp.ones((C, D), BF16)

    def gates(ref, j, fwd):
        ir, fr = (0, 1) if fwd else (2, 3)
        b_row = ref[0, 0, j, fr:fr + 1, :]
        a_row = ref[0, 0, j, ir:ir + 1, :] - b_row
        return a_row, (b_row[:, C - 1:C] if fwd else b_row[:, 0:1])

    def vaug_of(ref, j):
        return jnp.concatenate([ref[0, j * C:(j + 1) * C, :], ones_blk], axis=1)

    def update(cst, m_prev, kt, vaug, a_row, btot):
        mc = jnp.maximum(jnp.max(a_row, axis=1, keepdims=True), m_prev)
        w_in = jnp.exp(a_row - mc)
        ktw = (kt.astype(F32) * w_in).astype(BF16)
        return jnp.exp(m_prev - mc) * cst + _dot(ktw, vaug), btot + mc

    m_in = {}
    for d, fwd in ((0, True), (1, False)):
        cst = jnp.zeros((D, 2 * D), F32)
        m = jnp.full((1, 1), M_INIT, F32)
        for j in (range(nctx) if fwd else reversed(range(nctx))):
            a_row, btot = gates(ac_ref, j, fwd)
            cst, m = update(cst, m, kct_ref[0, j], vaug_of(vc_ref, j), a_row, btot)
        order = list(range(nc) if fwd else reversed(range(nc)))
        for idx, j in enumerate(order):
            cin_ref[d, j] = cst.astype(BF16)
            m_in[d, j] = m
            if idx + 1 < nc:
                a_row, btot = gates(a_ref, j, fwd)
                cst, m = update(cst, m, kt_ref[0, j], vaug_of(v_ref, j), a_row, btot)

    def out_step(d, j, fwd, q, kt, vaug):
        a_row, _ = gates(a_ref, j, fwd)
        m_prev = m_in[d, j]
        nlf = -a_ref[0, 0, j, (4 if fwd else 5):(5 if fwd else 6), :]
        nlf_hi = nlf.astype(BF16)
        nlf_lo = (nlf - nlf_hi.astype(F32)).astype(BF16)
        vis = ge_bf if fwd else le_bf
        nb = _dot_nt(vis, jnp.broadcast_to(nlf_hi, (D, C))) + _dot_nt(vis, jnp.broadcast_to(nlf_lo, (D, C)))
        a_mat = jnp.where(ge if fwd else le, a_row, -jnp.inf)
        m_q = jnp.maximum(jnp.max(a_mat, axis=1, keepdims=True), m_prev)
        bm = jnp.broadcast_to(m_q, (C, D))
        w_intra = jnp.exp(a_mat - bm)
        p = (_dot(q, kt) * w_intra).astype(BF16)
        qc = _dot(q, cin_ref[d, j])
        pv = _dot(p, vaug)
        w_inter = jnp.exp(m_prev - bm)
        num = w_inter * qc[:, :D] + pv[:, :D]
        den = w_inter * qc[:, D:] + pv[:, D:]
        return num / jnp.maximum(jnp.abs(den), jnp.exp(nb - bm))

    nw = nw_ref[...]
    for j in range(nc):
        sl = slice(j * C, (j + 1) * C)
        q = q_ref[0, sl, :]
        kt = kt_ref[0, j]
        vaug = vaug_of(v_ref, j)
        h = out_step(0, j, True, q, kt, vaug) + out_step(1, j, False, q, kt, vaug)
        var = jnp.mean(h * h, axis=-1, keepdims=True)
        y = h * lax.rsqrt(var + EPS) * nw
        o_ref[0, sl, :] = (og_ref[0, sl, :].astype(F32) * y).astype(o_ref.dtype)


def _mlstm(q, kt, v, a, og, kct, vc, ac, mh_norm_w):
    B, L, W = q.shape
    C = MLSTM_CHUNK
    assert C == HEAD_DIM
    nc = L // C
    lctx = vc.shape[1]
    nctx = lctx // C
    H = W // HEAD_DIM
    return pl.pallas_call(
        functools.partial(_mlstm_kernel, nc=nc, nctx=nctx),
        grid=(B, H),
        in_specs=[
            pl.BlockSpec((1, L, HEAD_DIM), lambda b, h: (b, 0, h)),
            pl.BlockSpec((1, nc, HEAD_DIM, C), lambda b, h: (b, 0, h, 0)),
            pl.BlockSpec((1, L, HEAD_DIM), lambda b, h: (b, 0, h)),
            pl.BlockSpec((1, 1, nc, GATE_ROWS, C), lambda b, h: (b, h, 0, 0, 0)),
            pl.BlockSpec((1, L, HEAD_DIM), lambda b, h: (b, 0, h)),
            pl.BlockSpec((1, nctx, HEAD_DIM, C), lambda b, h: (b, 0, h, 0)),
            pl.BlockSpec((1, lctx, HEAD_DIM), lambda b, h: (b, 0, h)),
            pl.BlockSpec((1, 1, nctx, GATE_ROWS, C), lambda b, h: (b, h, 0, 0, 0)),
            pl.BlockSpec((1, HEAD_DIM), lambda b, h: (0, h)),
        ],
        out_specs=pl.BlockSpec((1, L, HEAD_DIM), lambda b, h: (b, 0, h)),
        out_shape=jax.ShapeDtypeStruct((B, L, W), BF16),
        scratch_shapes=[pltpu.VMEM((2, nc, HEAD_DIM, 2 * HEAD_DIM), BF16)],
        compiler_params=_cparams("parallel", "parallel"),
        name="mlstm",
    )(q, kt, v, a, og, kct, vc, ac, mh_norm_w.reshape(1, W))


def _attn_kernel(q_ref, k_ref, v_ref, kc_ref, vc_ref, o_ref):
    k = k_ref[0]
    v = v_ref[0]
    kc = kc_ref[0]
    vc = vc_ref[0]
    for h in range(A_GROUP):
        q = q_ref[0, :, h * HEAD_DIM:(h + 1) * HEAD_DIM]
        s1 = _dot_nt(q, k)
        s2 = _dot_nt(q, kc)
        m = jnp.maximum(jnp.max(s1, axis=1, keepdims=True), jnp.max(s2, axis=1, keepdims=True))
        p1 = jnp.exp(s1 - m)
        p2 = jnp.exp(s2 - m)
        l = jnp.sum(p1, axis=1, keepdims=True) + jnp.sum(p2, axis=1, keepdims=True)
        o = _dot(p1.astype(BF16), v) + _dot(p2.astype(BF16), vc)
        o_ref[0, :, h * HEAD_DIM:(h + 1) * HEAD_DIM] = (o / l).astype(o_ref.dtype)


def _attention(q, k, v, kc, vc):
    B, L, W = q.shape
    lctx = kc.shape[1]
    tq = min(L, 256)
    gw = A_GROUP * HEAD_DIM
    return pl.pallas_call(
        _attn_kernel,
        grid=(B, A_KV_HEADS, L // tq),
        in_specs=[
            pl.BlockSpec((1, tq, gw), lambda b, g, i: (b, i, g)),
            pl.BlockSpec((1, L, HEAD_DIM), lambda b, g, i: (b, 0, g)),
            pl.BlockSpec((1, L, HEAD_DIM), lambda b, g, i: (b, 0, g)),
            pl.BlockSpec((1, lctx, HEAD_DIM), lambda b, g, i: (b, 0, g)),
            pl.BlockSpec((1, lctx, HEAD_DIM), lambda b, g, i: (b, 0, g)),
        ],
        out_specs=pl.BlockSpec((1, tq, gw), lambda b, g, i: (b, i, g)),
        out_shape=jax.ShapeDtypeStruct((B, L, W), BF16),
        compiler_params=_cparams("parallel", "parallel", "parallel"),
        name="attention",
    )(q, k, v, kc, vc)


def _merge_kernel(ym_ref, oa_ref, bg_ref, x_ref, mod_ref, wbm_ref, wba_ref, wo_ref, n2_ref, wr_ref, br_ref,
                  hmid_ref, f_ref, comb_ref):
    D = x_ref.shape[-1]
    ym = _dot(ym_ref[0], wbm_ref[...])
    ya = _dot(oa_ref[0], wba_ref[...])
    g0 = bg_ref[0, :, :D].astype(F32)
    g1 = bg_ref[0, :, D:].astype(F32)
    mix = _dot((g0 * ym + g1 * ya).astype(BF16), wo_ref[...])
    hmid = x_ref[0] + mod_ref[0, 2:3, :] * mix
    hmid_ref[0] = hmid
    var = jnp.mean(hmid * hmid, axis=-1, keepdims=True)
    y = hmid * lax.rsqrt(var + EPS) * n2_ref[...]
    f = y * (1.0 + mod_ref[0, 4:5, :]) + mod_ref[0, 3:4, :]
    f_ref[0] = f.astype(f_ref.dtype)

    fp = _split3(f)
    wr = wr_ref[...]
    wr_hi = wr.astype(BF16)
    wr_lo = (wr - wr_hi.astype(F32)).astype(BF16)
    logits = _dot(fp[0], wr_hi) + _dot(fp[1], wr_hi) + _dot(fp[0], wr_lo) + _dot(fp[2], wr_hi) + _dot(fp[1], wr_lo)
    logits = logits + br_ref[...]
    lane = lax.broadcasted_iota(jnp.int32, logits.shape, 1)
    big = jnp.int32(ROUTER_LANES)
    neg = -jnp.inf
    is_grp = lane < N_GROUPS
    gl = jnp.where(is_grp, logits, neg)
    gmax = jnp.max(gl, axis=1, keepdims=True)
    gsel = jnp.min(jnp.where(gl == gmax, lane, big), axis=1, keepdims=True)
    p_grp = 1.0 / jnp.sum(jnp.where(is_grp, jnp.exp(logits - gmax), 0.0), axis=1, keepdims=True)
    lo = N_GROUPS + gsel * EXPERTS_PER_GROUP
    in_grp = (lane >= lo) & (lane < lo + EXPERTS_PER_GROUP)
    el = jnp.where(in_grp, logits, neg)
    v1 = jnp.max(el, axis=1, keepdims=True)
    i1 = jnp.min(jnp.where(el == v1, lane, big), axis=1, keepdims=True)
    el2 = jnp.where(lane == i1, neg, el)
    v2 = jnp.max(el2, axis=1, keepdims=True)
    i2 = jnp.min(jnp.where(el2 == v2, lane, big), axis=1, keepdims=True)
    e21 = jnp.exp(v2 - v1)
    w1 = p_grp / (1.0 + e21)
    w2 = p_grp * e21 / (1.0 + e21)
    comb_ref[0] = jnp.where(lane == i1, w1, 0.0) + jnp.where(lane == i2, w2, 0.0)


def _merge(ym, oa, bg, x, mod, wbm, wba, wo, n2, wr, br):
    B, L, D = x.shape
    tm = min(L, 512)
    full = lambda b, i: (0, 0)
    tok = lambda b, i: (b, i, 0)
    return pl.pallas_call(
        _merge_kernel,
        grid=(B, L // tm),
        in_specs=[
            pl.BlockSpec((1, tm, D), tok),
            pl.BlockSpec((1, tm, D), tok),
            pl.BlockSpec((1, tm, 2 * D), tok),
            pl.BlockSpec((1, tm, D), tok),
            pl.BlockSpec((1, 6, D), lambda b, i: (b, 0, 0)),
            pl.BlockSpec((D, D), full),
            pl.BlockSpec((D, D), full),
            pl.BlockSpec((D, D), full),
            pl.BlockSpec((1, D), full),
            pl.BlockSpec((D, ROUTER_LANES), full),
            pl.BlockSpec((1, ROUTER_LANES), full),
        ],
        out_specs=[
            pl.BlockSpec((1, tm, D), tok),
            pl.BlockSpec((1, tm, D), tok),
            pl.BlockSpec((1, tm, ROUTER_LANES), tok),
        ],
        out_shape=[
            jax.ShapeDtypeStruct((B, L, D), F32),
            jax.ShapeDtypeStruct((B, L, D), BF16),
            jax.ShapeDtypeStruct((B, L, ROUTER_LANES), F32),
        ],
        compiler_params=_cparams("parallel", "parallel"),
        name="merge_router",
    )(ym, oa, bg, x, mod, wbm, wba, wo, n2.reshape(1, D), wr, br)


def _moe_dense_kernel(f_ref, comb_ref, hmid_ref, mod_ref, wg_ref, wu_ref, wd_ref, o_ref, acc_ref):
    e = pl.program_id(2)

    @pl.when(e == 0)
    def _():
        acc_ref[...] = jnp.zeros_like(acc_ref)

    f = f_ref[0]
    hidden = (_silu(_dot(f, wg_ref[0])) * _dot(f, wu_ref[0])).astype(BF16)
    y = _dot(hidden, wd_ref[0])
    comb = comb_ref[0]
    lane = lax.broadcasted_iota(jnp.int32, comb.shape, 1)
    col = jnp.sum(jnp.where(lane == e + N_GROUPS, comb, 0.0), axis=1, keepdims=True)
    acc_ref[...] += col * y

    @pl.when(e == pl.num_programs(2) - 1)
    def _():
        o_ref[0] = hmid_ref[0] + mod_ref[0, 5:6, :] * acc_ref[...]


def _moe_dense(f, comb, hmid, mod, wg, wu, wd):
    B, L, D = hmid.shape
    E, _, DE = wg.shape
    tm = min(L, 1024)
    tok = lambda b, i, e: (b, i, 0)
    return pl.pallas_call(
        _moe_dense_kernel,
        grid=(B, L // tm, E),
        in_specs=[
            pl.BlockSpec((1, tm, D), tok),
            pl.BlockSpec((1, tm, ROUTER_LANES), tok),
            pl.BlockSpec((1, tm, D), tok),
            pl.BlockSpec((1, 6, D), lambda b, i, e: (b, 0, 0)),
            pl.BlockSpec((1, D, DE), lambda b, i, e: (e, 0, 0)),
            pl.BlockSpec((1, D, DE), lambda b, i, e: (e, 0, 0)),
            pl.BlockSpec((1, DE, D), lambda b, i, e: (e, 0, 0)),
        ],
        out_specs=pl.BlockSpec((1, tm, D), tok),
        out_shape=jax.ShapeDtypeStruct((B, L, D), F32),
        scratch_shapes=[pltpu.VMEM((tm, D), F32)],
        compiler_params=_cparams("parallel", "parallel", "arbitrary"),
        name="moe_dense",
    )(f, comb, hmid, mod, wg, wu, wd)


def _gate_weights(w_gate, b_gate):
    D = w_gate.shape[0]
    wt = w_gate.T.reshape(4, M_HEADS, D).transpose(1, 0, 2)
    wt = jnp.concatenate([wt, wt[:, 1:2], wt[:, 3:4], jnp.zeros((M_HEADS, GATE_ROWS - 6, D), wt.dtype)], axis=1)
    bt = b_gate.reshape(4, M_HEADS).T
    bt = jnp.concatenate([bt, bt[:, 1:2], bt[:, 3:4], jnp.zeros((M_HEADS, GATE_ROWS - 6), bt.dtype)], axis=1)
    return wt.reshape(M_HEADS * GATE_ROWS, D), bt.reshape(M_HEADS * GATE_ROWS)


def kernel(x, c, ctx, c_ctx, w_ada, b_ada, norm1_w, w_in, b_mgate, q_norm_w, k_norm_w, mh_norm_w, w_branch_m,
           w_branch_a, w_out, norm2_w, w_rg, b_rg, w_re, b_re, w_e_gate, w_e_up, w_e_down):
    B, L, D = x.shape
    depth = w_ada.shape[0]
    assert depth == 1, "context-stream update between layers is not implemented"
    l = 0
    mw = M_HEADS * HEAD_DIM
    aq = A_HEADS * HEAD_DIM
    akv = A_KV_HEADS * HEAD_DIM
    o_mq, o_mk, o_mv, o_og = 0, mw, 2 * mw, 3 * mw
    o_g = 4 * mw
    o_aq = o_g + 4 * M_HEADS
    o_ak = o_aq + aq
    o_av = o_ak + akv
    o_bg = o_av + akv
    scale = HEAD_DIM ** -0.5

    rows = ((B + 1 + 7) // 8) * 8
    cc = jnp.concatenate([c, c_ctx[None, :], jnp.zeros((rows - B - 1, D), F32)], axis=0)
    mod = _ada_mod(cc, w_ada[l], b_ada[l]).reshape(rows, 6, D)

    xn = _norm_mod(x, mod, norm1_w[l], lambda b: b)
    cn = _norm_mod(ctx, mod, norm1_w[l], lambda b: B)

    w = w_in[l]
    wb = lambda a, b: w[:, a:b].astype(BF16)
    cos, sin = _rope_tables(L)
    wgt, bgt = _gate_weights(w[:, o_g:o_aq], b_mgate[l])
    wgt = wgt.astype(BF16)
    wkt = w[:, o_mk:o_mv].T.astype(BF16)

    mq = _proj(xn, wb(o_mq, o_mk), scale=scale, name="proj_mq")
    mkt = _proj_t(xn, wkt, chunk=MLSTM_CHUNK, name="proj_mkt")
    mv = _proj(xn, wb(o_mv, o_og), name="proj_mv")
    og = _proj(xn, wb(o_og, o_g), act="sigmoid", name="proj_og")
    gt = _proj_gates(xn, wgt, bgt, chunk=MLSTM_CHUNK, name="proj_gates")
    qa = _proj_normrope(xn, wb(o_aq, o_ak), q_norm_w[l], cos, sin, scale=scale, rope=True, name="proj_aq")
    ka = _proj_normrope(xn, wb(o_ak, o_av), k_norm_w[l], cos, sin, scale=1.0, rope=True, name="proj_ak")
    va = _proj(xn, wb(o_av, o_bg), name="proj_av")
    bg = _proj(xn, wb(o_bg, o_bg + 2 * D), act="sigmoid", name="proj_bg")

    lctx = ctx.shape[1]
    cmkt = _proj_t(cn, wkt, chunk=MLSTM_CHUNK, name="proj_cmkt")
    cmv = _proj(cn, wb(o_mv, o_og), name="proj_cmv")
    cgt = _proj_gates(cn, wgt, bgt, chunk=MLSTM_CHUNK, name="proj_cgates")
    cka = _proj_normrope(cn, wb(o_ak, o_av), k_norm_w[l], cos[:lctx], sin[:lctx], scale=1.0, rope=False, name="proj_cak")
    cva = _proj(cn, wb(o_av, o_bg), name="proj_cav")

    ga = _gate_prep(gt, "gate_prep")
    gca = _gate_prep(cgt, "gate_prep_ctx")
    ym = _mlstm(mq, mkt, mv, ga, og, cmkt, cmv, gca, mh_norm_w[l])
    oa = _attention(qa, ka, va, cka, cva)

    wr = jnp.concatenate([w_rg[l], w_re[l], jnp.zeros((D, ROUTER_LANES - N_GROUPS - N_EXPERTS), F32)], axis=1)
    br = jnp.concatenate([b_rg[l], b_re[l], jnp.zeros((ROUTER_LANES - N_GROUPS - N_EXPERTS,), F32)]).reshape(1, -1)
    hmid, f, comb = _merge(ym, oa, bg, x, mod, w_branch_m[l].astype(BF16), w_branch_a[l].astype(BF16),
                           w_out[l].astype(BF16), norm2_w[l], wr, br)

    return _moe_dense(f, comb, hmid, mod, w_e_gate[l].astype(BF16), w_e_up[l].astype(BF16), w_e_down[l].astype(BF16))
```

```python
import functools

import jax
import jax.numpy as jnp
from jax import lax
from jax.experimental import pallas as pl
from jax.experimental.pallas import tpu as pltpu

F32 = jnp.float32
BF16 = jnp.bfloat16

EPS = 1e-6
GRID_W = 64
ROPE_THETA = 10000.0
HEAD_DIM = 128
M_HEADS = 8
A_HEADS = 8
A_KV_HEADS = 2
A_GROUP = A_HEADS // A_KV_HEADS
N_GROUPS = 4
EXPERTS_PER_GROUP = 8
N_EXPERTS = N_GROUPS * EXPERTS_PER_GROUP
M_INIT = -1e30
MLSTM_CHUNK = 128
GATE_ROWS = 8
ROUTER_LANES = 128
EXPERT_ROW0 = 8
ROUTER_ROWS = EXPERT_ROW0 + N_EXPERTS
EXPERT_TILE = 512
ROUTE_ROWS = 8
VMEM_LIMIT_BYTES = 48 * 1024 * 1024

NT_DIMS = (((1,), (1,)), ((), ()))


def _cparams(*sem):
    return pltpu.CompilerParams(dimension_semantics=sem, vmem_limit_bytes=VMEM_LIMIT_BYTES)


def _dot(a, b):
    return jnp.dot(a, b, preferred_element_type=F32)


def _dot_nt(a, b):
    return lax.dot_general(a, b, NT_DIMS, preferred_element_type=F32)


def _split3(x):
    hi = x.astype(BF16)
    r1 = x - hi.astype(F32)
    mid = r1.astype(BF16)
    lo = (r1 - mid.astype(F32)).astype(BF16)
    return hi, mid, lo


def _silu(x):
    return x * jax.nn.sigmoid(x)


def _ada_kernel(c_ref, w_ref, b_ref, o_ref):
    s = _silu(c_ref[...])
    parts = _split3(s)
    w = w_ref[...]
    w_hi = w.astype(BF16)
    w_lo = (w - w_hi.astype(F32)).astype(BF16)
    acc = _dot(parts[0], w_hi) + _dot(parts[1], w_hi) + _dot(parts[0], w_lo)
    o_ref[...] = acc + b_ref[...]


def _ada_mod(cc, w, b):
    rows, d = cc.shape
    n = w.shape[1]
    tn = min(n, 1536)
    return pl.pallas_call(
        _ada_kernel,
        grid=(n // tn,),
        in_specs=[
            pl.BlockSpec((rows, d), lambda j: (0, 0)),
            pl.BlockSpec((d, tn), lambda j: (0, j)),
            pl.BlockSpec((1, tn), lambda j: (0, j)),
        ],
        out_specs=pl.BlockSpec((rows, tn), lambda j: (0, j)),
        out_shape=jax.ShapeDtypeStruct((rows, n), F32),
        compiler_params=_cparams("parallel"),
        name="ada_mod",
    )(cc, w, b.reshape(1, n))


def _norm_mod_kernel(x_ref, mod_ref, nw_ref, o_ref):
    x = x_ref[0]
    var = jnp.mean(x * x, axis=-1, keepdims=True)
    y = x * lax.rsqrt(var + EPS) * nw_ref[...]
    sh = mod_ref[0, 0:1, :]
    sc = mod_ref[0, 1:2, :]
    o_ref[0] = (y * (1.0 + sc) + sh).astype(o_ref.dtype)


def _norm_mod(x, mod, nw, mod_row):
    B, L, D = x.shape
    tm = min(L, 512)
    return pl.pallas_call(
        _norm_mod_kernel,
        grid=(B, L // tm),
        in_specs=[
            pl.BlockSpec((1, tm, D), lambda b, i: (b, i, 0)),
            pl.BlockSpec((1, 6, D), lambda b, i: (mod_row(b), 0, 0)),
            pl.BlockSpec((1, D), lambda b, i: (0, 0)),
        ],
        out_specs=pl.BlockSpec((1, tm, D), lambda b, i: (b, i, 0)),
        out_shape=jax.ShapeDtypeStruct((B, L, D), BF16),
        compiler_params=_cparams("parallel", "parallel"),
        name="norm_mod",
    )(x, mod, nw.reshape(1, D))


def _proj_kernel(x_ref, w_ref, o_ref, *, act, scale):
    acc = _dot(x_ref[0], w_ref[...])
    if act == "sigmoid":
        acc = jax.nn.sigmoid(acc)
    if scale != 1.0:
        acc = acc * scale
    o_ref[0] = acc.astype(o_ref.dtype)


def _proj(xn, w, *, act="none", scale=1.0, name):
    B, L, D = xn.shape
    N = w.shape[1]
    tm = min(L, 512)
    tn = min(N, 1024)
    return pl.pallas_call(
        functools.partial(_proj_kernel, act=act, scale=scale),
        grid=(B, N // tn, L // tm),
        in_specs=[
            pl.BlockSpec((1, tm, D), lambda b, j, i: (b, i, 0)),
            pl.BlockSpec((D, tn), lambda b, j, i: (0, j)),
        ],
        out_specs=pl.BlockSpec((1, tm, tn), lambda b, j, i: (b, i, j)),
        out_shape=jax.ShapeDtypeStruct((B, L, N), BF16),
        compiler_params=_cparams("parallel", "parallel", "parallel"),
        name=name,
    )(xn, w)


def _proj_t_kernel(w_ref, x_ref, o_ref, *, nchunk, chunk):
    acc = _dot_nt(w_ref[...], x_ref[0])
    for c in range(nchunk):
        o_ref[0, c] = acc[:, c * chunk:(c + 1) * chunk].astype(o_ref.dtype)


def _proj_t(xn, wt, *, chunk, name):
    B, L, D = xn.shape
    R = wt.shape[0]
    tm = min(L, 512)
    nchunk = tm // chunk
    return pl.pallas_call(
        functools.partial(_proj_t_kernel, nchunk=nchunk, chunk=chunk),
        grid=(B, L // tm),
        in_specs=[
            pl.BlockSpec((R, D), lambda b, i: (0, 0)),
            pl.BlockSpec((1, tm, D), lambda b, i: (b, i, 0)),
        ],
        out_specs=pl.BlockSpec((1, nchunk, R, chunk), lambda b, i: (b, i, 0, 0)),
        out_shape=jax.ShapeDtypeStruct((B, L // chunk, R, chunk), BF16),
        compiler_params=_cparams("parallel", "parallel"),
        name=name,
    )(wt, xn)


def _proj_gates_kernel(w_ref, x_ref, b_ref, o_ref, *, nchunk, chunk):
    acc = _dot_nt(w_ref[...], x_ref[0]) + b_ref[...]
    for h in range(M_HEADS):
        for c in range(nchunk):
            o_ref[0, h, c] = acc[h * GATE_ROWS:(h + 1) * GATE_ROWS, c * chunk:(c + 1) * chunk]


def _proj_gates(xn, wt, bias, *, chunk, name):
    B, L, D = xn.shape
    R = wt.shape[0]
    tm = min(L, 512)
    nchunk = tm // chunk
    return pl.pallas_call(
        functools.partial(_proj_gates_kernel, nchunk=nchunk, chunk=chunk),
        grid=(B, L // tm),
        in_specs=[
            pl.BlockSpec((R, D), lambda b, i: (0, 0)),
            pl.BlockSpec((1, tm, D), lambda b, i: (b, i, 0)),
            pl.BlockSpec((R, 1), lambda b, i: (0, 0)),
        ],
        out_specs=pl.BlockSpec((1, M_HEADS, nchunk, GATE_ROWS, chunk), lambda b, i: (b, 0, i, 0, 0)),
        out_shape=jax.ShapeDtypeStruct((B, M_HEADS, L // chunk, GATE_ROWS, chunk), F32),
        compiler_params=_cparams("parallel", "parallel"),
        name=name,
    )(wt, xn, bias.reshape(R, 1))


def _proj_normrope_kernel(x_ref, w_ref, nw_ref, cos_ref, sin_ref, o_ref, *, nheads, scale, rope):
    acc = _dot(x_ref[0], w_ref[...])
    nw = nw_ref[...]
    if rope:
        cos = cos_ref[...]
        sin = sin_ref[...]
        lane = lax.broadcasted_iota(jnp.int32, (acc.shape[0], HEAD_DIM), 1)
        first = (lane % (HEAD_DIM // 2)) < (HEAD_DIM // 4)
    for h in range(nheads):
        a = acc[:, h * HEAD_DIM:(h + 1) * HEAD_DIM]
        var = jnp.mean(a * a, axis=-1, keepdims=True)
        y = a * lax.rsqrt(var + EPS) * nw
        if rope:
            partner = jnp.where(first, pltpu.roll(y, HEAD_DIM - HEAD_DIM // 4, 1), pltpu.roll(y, HEAD_DIM // 4, 1))
            y = y * cos + partner * sin
        if scale != 1.0:
            y = y * scale
        o_ref[0, :, h * HEAD_DIM:(h + 1) * HEAD_DIM] = y.astype(o_ref.dtype)


def _proj_normrope(xn, w, nw, cos, sin, *, scale, rope, name):
    B, L, D = xn.shape
    N = w.shape[1]
    nheads = N // HEAD_DIM
    tm = min(L, 512)
    return pl.pallas_call(
        functools.partial(_proj_normrope_kernel, nheads=nheads, scale=scale, rope=rope),
        grid=(B, L // tm),
        in_specs=[
            pl.BlockSpec((1, tm, D), lambda b, i: (b, i, 0)),
            pl.BlockSpec((D, N), lambda b, i: (0, 0)),
            pl.BlockSpec((1, HEAD_DIM), lambda b, i: (0, 0)),
            pl.BlockSpec((tm, HEAD_DIM), lambda b, i: (i, 0)),
            pl.BlockSpec((tm, HEAD_DIM), lambda b, i: (i, 0)),
        ],
        out_specs=pl.BlockSpec((1, tm, N), lambda b, i: (b, i, 0)),
        out_shape=jax.ShapeDtypeStruct((B, L, N), BF16),
        compiler_params=_cparams("parallel", "parallel"),
        name=name,
    )(xn, w, nw.reshape(1, HEAD_DIM), cos, sin)


def _rope_tables(L):
    rows = L // GRID_W
    row = jnp.repeat(jnp.arange(rows), GRID_W).astype(F32)
    col = jnp.tile(jnp.arange(GRID_W), rows).astype(F32)
    half = HEAD_DIM // 2
    inv = ROPE_THETA ** (-jnp.arange(0, half, 2, dtype=F32) / half)
    ang_r = row[:, None] * inv[None, :]
    ang_c = col[:, None] * inv[None, :]
    cr, sr, cc, sc = jnp.cos(ang_r), jnp.sin(ang_r), jnp.cos(ang_c), jnp.sin(ang_c)
    return jnp.concatenate([cr, cr, cc, cc], axis=-1), jnp.concatenate([-sr, sr, -sc, sc], axis=-1)


def _log_sigmoid(x):
    return jnp.minimum(x, 0.0) - jnp.log1p(jnp.exp(-jnp.abs(x)))


def _tri_masks(n):
    row_i = lax.broadcasted_iota(jnp.int32, (n, n), 0)
    col_i = lax.broadcasted_iota(jnp.int32, (n, n), 1)
    return row_i <= col_i, row_i >= col_i


def _gate_prep_kernel(g_ref, a_ref):
    _, H, nc, R8, C = g_ref.shape
    rows = H * nc * R8
    g = g_ref[0].reshape(rows, C)
    le, ge = _tri_masks(C)
    upper = jnp.where(le, 1.0, 0.0).astype(BF16)
    lower = jnp.where(ge, 1.0, 0.0).astype(BF16)
    lf = _log_sigmoid(g)
    parts = _split3(lf)
    pre = _dot(parts[0], upper) + _dot(parts[1], upper) + _dot(parts[2], upper)
    suf = _dot(parts[0], lower) + _dot(parts[1], lower) + _dot(parts[2], lower)
    rtype = lax.broadcasted_iota(jnp.int32, (rows, C), 0) % R8
    a = jnp.where(rtype == 1, pre, jnp.where(rtype == 3, suf, jnp.where(rtype >= 4, lf, g)))
    a_ref[0] = a.reshape(H, nc, R8, C)


def _gate_prep(g, name):
    B, H, nc, R8, C = g.shape
    blk = pl.BlockSpec((1, H, nc, R8, C), lambda b: (b, 0, 0, 0, 0))
    return pl.pallas_call(
        _gate_prep_kernel,
        grid=(B,),
        in_specs=[blk],
        out_specs=blk,
        out_shape=jax.ShapeDtypeStruct(g.shape, F32),
        compiler_params=_cparams("parallel"),
        name=name,
    )(g)


def _mlstm_kernel(q_ref, kt_ref, v_ref, a_ref, og_ref, kct_ref, vc_ref, ac_ref, nw_ref, o_ref, cin_ref,
                  *, nc, nctx):
    C = MLSTM_CHUNK
    D = HEAD_DIM
    le, ge = _tri_masks(C)
    le_bf = jnp.where(le, 1.0, 0.0).astype(BF16)
    ge_bf = jnp.where(ge, 1.0, 0.0).astype(BF16)
    ones_blk = jnp.ones((C, D), BF16)

    def gates(ref, j, fwd):
        ir, fr = (0, 1) if fwd else (2, 3)
        b_row = ref[0, 0, j, fr:fr + 1, :]
        a_row = ref[0, 0, j, ir:ir + 1, :] - b_row
        return a_row, (b_row[:, C - 1:C] if fwd else b_row[:, 0:1])

    def vaug_of(ref, j):
        return jnp.concatenate([ref[0, j * C:(j + 1) * C, :], ones_blk], axis=1)

    def update(cst, m_prev, kt, vaug, a_row, btot):
        mc = jnp.maximum(jnp.max(a_row, axis=1, keepdims=True), m_prev)
        w_in = jnp.exp(a_row - mc)
        ktw = (kt.astype(F32) * w_in).astype(BF16)
        return jnp.exp(m_prev - mc) * cst + _dot(ktw, vaug), btot + mc

    m_in = {}
    for d, fwd in ((0, True), (1, False)):
        cst = jnp.zeros((D, 2 * D), F32)
        m = jnp.full((1, 1), M_INIT, F32)
        for j in (range(nctx) if fwd else reversed(range(nctx))):
            a_row, btot = gates(ac_ref, j, fwd)
            cst, m = update(cst, m, kct_ref[0, j], vaug_of(vc_ref, j), a_row, btot)
        order = list(range(nc) if fwd else reversed(range(nc)))
        for idx, j in enumerate(order):
            cin_ref[d, j] = cst.astype(BF16)
            m_in[d, j] = m
            if idx + 1 < nc:
                a_row, btot = gates(a_ref, j, fwd)
                cst, m = update(cst, m, kt_ref[0, j], vaug_of(v_ref, j), a_row, btot)

    def out_step(d, j, fwd, q, kt, vaug):
        a_row, _ = gates(a_ref, j, fwd)
        m_prev = m_in[d, j]
        nlf = -a_ref[0, 0, j, (4 if fwd else 5):(5 if fwd else 6), :]
        nlf_hi = nlf.astype(BF16)
        nlf_lo = (nlf - nlf_hi.astype(F32)).astype(BF16)
        vis = ge_bf if fwd else le_bf
        nb = _dot_nt(vis, jnp.broadcast_to(nlf_hi, (D, C))) + _dot_nt(vis, jnp.broadcast_to(nlf_lo, (D, C)))
        a_mat = jnp.where(ge if fwd else le, a_row, -jnp.inf)
        m_q = jnp.maximum(jnp.max(a_mat, axis=1, keepdims=True), m_prev)
        bm = jnp.broadcast_to(m_q, (C, D))
        w_intra = jnp.exp(a_mat - bm)
        p = (_dot(q, kt) * w_intra).astype(BF16)
        qc = _dot(q, cin_ref[d, j])
        pv = _dot(p, vaug)
        w_inter = jnp.exp(m_prev - bm)
        num = w_inter * qc[:, :D] + pv[:, :D]
        den = w_inter * qc[:, D:] + pv[:, D:]
        return num / jnp.maximum(jnp.abs(den), jnp.exp(nb - bm))

    nw = nw_ref[...]
    for j in range(nc):
        sl = slice(j * C, (j + 1) * C)
        q = q_ref[0, sl, :]
        kt = kt_ref[0, j]
        vaug = vaug_of(v_ref, j)
        h = out_step(0, j, True, q, kt, vaug) + out_step(1, j, False, q, kt, vaug)
        var = jnp.mean(h * h, axis=-1, keepdims=True)
        y = h * lax.rsqrt(var + EPS) * nw
        o_ref[0, sl, :] = (og_ref[0, sl, :].astype(F32) * y).astype(o_ref.dtype)


def _mlstm(q, kt, v, a, og, kct, vc, ac, mh_norm_w):
    B, L, W = q.shape
    C = MLSTM_CHUNK
    assert C == HEAD_DIM
    nc = L // C
    lctx = vc.shape[1]
    nctx = lctx // C
    H = W // HEAD_DIM
    return pl.pallas_call(
        functools.partial(_mlstm_kernel, nc=nc, nctx=nctx),
        grid=(B, H),
        in_specs=[
            pl.BlockSpec((1, L, HEAD_DIM), lambda b, h: (b, 0, h)),
            pl.BlockSpec((1, nc, HEAD_DIM, C), lambda b, h: (b, 0, h, 0)),
            pl.BlockSpec((1, L, HEAD_DIM), lambda b, h: (b, 0, h)),
            pl.BlockSpec((1, 1, nc, GATE_ROWS, C), lambda b, h: (b, h, 0, 0, 0)),
            pl.BlockSpec((1, L, HEAD_DIM), lambda b, h: (b, 0, h)),
            pl.BlockSpec((1, nctx, HEAD_DIM, C), lambda b, h: (b, 0, h, 0)),
            pl.BlockSpec((1, lctx, HEAD_DIM), lambda b, h: (b, 0, h)),
            pl.BlockSpec((1, 1, nctx, GATE_ROWS, C), lambda b, h: (b, h, 0, 0, 0)),
            pl.BlockSpec((1, HEAD_DIM), lambda b, h: (0, h)),
        ],
        out_specs=pl.BlockSpec((1, L, HEAD_DIM), lambda b, h: (b, 0, h)),
        out_shape=jax.ShapeDtypeStruct((B, L, W), BF16),
        scratch_shapes=[pltpu.VMEM((2, nc, HEAD_DIM, 2 * HEAD_DIM), BF16)],
        compiler_params=_cparams("parallel", "parallel"),
        name="mlstm",
    )(q, kt, v, a, og, kct, vc, ac, mh_norm_w.reshape(1, W))


def _attn_kernel(q_ref, k_ref, v_ref, kc_ref, vc_ref, o_ref):
    k = k_ref[0]
    v = v_ref[0]
    kc = kc_ref[0]
    vc = vc_ref[0]
    for h in range(A_GROUP):
        q = q_ref[0, :, h * HEAD_DIM:(h + 1) * HEAD_DIM]
        s1 = _dot_nt(q, k)
        s2 = _dot_nt(q, kc)
        m = jnp.maximum(jnp.max(s1, axis=1, keepdims=True), jnp.max(s2, axis=1, keepdims=True))
        p1 = jnp.exp(s1 - m)
        p2 = jnp.exp(s2 - m)
        l = jnp.sum(p1, axis=1, keepdims=True) + jnp.sum(p2, axis=1, keepdims=True)
        o = _dot(p1.astype(BF16), v) + _dot(p2.astype(BF16), vc)
        o_ref[0, :, h * HEAD_DIM:(h + 1) * HEAD_DIM] = (o / l).astype(o_ref.dtype)


def _attention(q, k, v, kc, vc):
    B, L, W = q.shape
    lctx = kc.shape[1]
    tq = min(L, 256)
    gw = A_GROUP * HEAD_DIM
    return pl.pallas_call(
        _attn_kernel,
        grid=(B, A_KV_HEADS, L // tq),
        in_specs=[
            pl.BlockSpec((1, tq, gw), lambda b, g, i: (b, i, g)),
            pl.BlockSpec((1, L, HEAD_DIM), lambda b, g, i: (b, 0, g)),
            pl.BlockSpec((1, L, HEAD_DIM), lambda b, g, i: (b, 0, g)),
            pl.BlockSpec((1, lctx, HEAD_DIM), lambda b, g, i: (b, 0, g)),
            pl.BlockSpec((1, lctx, HEAD_DIM), lambda b, g, i: (b, 0, g)),
        ],
        out_specs=pl.BlockSpec((1, tq, gw), lambda b, g, i: (b, i, g)),
        out_shape=jax.ShapeDtypeStruct((B, L, W), BF16),
        compiler_params=_cparams("parallel", "parallel", "parallel"),
        name="attention",
    )(q, k, v, kc, vc)


def _merge_kernel(ym_ref, oa_ref, bg_ref, x_ref, mod_ref, wbm_ref, wba_ref, wo_ref, n2_ref, wr_ref, br_ref,
                  hmid_ref, f_ref, route_ref, w1_ref, w2_ref, cnt_ref, run_ref):
    D = x_ref.shape[-1]
    tm = x_ref.shape[1]

    @pl.when((pl.program_id(0) == 0) & (pl.program_id(1) == 0))
    def _():
        run_ref[...] = jnp.zeros_like(run_ref)

    ym = _dot(ym_ref[0], wbm_ref[...])
    ya = _dot(oa_ref[0], wba_ref[...])
    g0 = bg_ref[0, :, :D].astype(F32)
    g1 = bg_ref[0, :, D:].astype(F32)
    mix = _dot((g0 * ym + g1 * ya).astype(BF16), wo_ref[...])
    hmid = x_ref[0] + mod_ref[0, 2:3, :] * mix
    hmid_ref[0] = hmid
    var = jnp.mean(hmid * hmid, axis=-1, keepdims=True)
    y = hmid * lax.rsqrt(var + EPS) * n2_ref[...]
    f = y * (1.0 + mod_ref[0, 4:5, :]) + mod_ref[0, 3:4, :]
    f_ref[0] = f.astype(f_ref.dtype)

    fp = _split3(f)
    wr = wr_ref[...]
    wr_hi = wr.astype(BF16)
    wr_lo = (wr - wr_hi.astype(F32)).astype(BF16)
    logits = (_dot_nt(wr_hi, fp[0]) + _dot_nt(wr_hi, fp[1]) + _dot_nt(wr_lo, fp[0]) + _dot_nt(wr_hi, fp[2])
              + _dot_nt(wr_lo, fp[1])) + br_ref[...]
    row = lax.broadcasted_iota(jnp.int32, logits.shape, 0)
    big = jnp.int32(ROUTER_ROWS)
    neg = -jnp.inf
    is_grp = row < N_GROUPS
    gl = jnp.where(is_grp, logits, neg)
    gmax = jnp.max(gl, axis=0, keepdims=True)
    gsel = jnp.min(jnp.where(gl == gmax, row, big), axis=0, keepdims=True)
    p_grp = 1.0 / jnp.sum(jnp.where(is_grp, jnp.exp(logits - gmax), 0.0), axis=0, keepdims=True)
    lo = EXPERT_ROW0 + gsel * EXPERTS_PER_GROUP
    in_grp = (row >= lo) & (row < lo + EXPERTS_PER_GROUP)
    el = jnp.where(in_grp, logits, neg)
    v1 = jnp.max(el, axis=0, keepdims=True)
    i1 = jnp.min(jnp.where(el == v1, row, big), axis=0, keepdims=True)
    el2 = jnp.where(row == i1, neg, el)
    v2 = jnp.max(el2, axis=0, keepdims=True)
    i2 = jnp.min(jnp.where(el2 == v2, row, big), axis=0, keepdims=True)
    e21 = jnp.exp(v2 - v1)
    w1 = p_grp / (1.0 + e21)
    w2 = p_grp * e21 / (1.0 + e21)

    sel = jnp.where((row == i1) | (row == i2), 1.0, 0.0)
    tok_u = lax.broadcasted_iota(jnp.int32, (tm, tm), 0)
    tok_n = lax.broadcasted_iota(jnp.int32, (tm, tm), 1)
    earlier = jnp.where(tok_u < tok_n, 1.0, 0.0).astype(BF16)
    rank_all = run_ref[...] + _dot(sel.astype(BF16), earlier)
    rank1 = jnp.sum(jnp.where(row == i1, rank_all, 0.0), axis=0, keepdims=True)
    rank2 = jnp.sum(jnp.where(row == i2, rank_all, 0.0), axis=0, keepdims=True)
    run_ref[...] += jnp.sum(sel, axis=1, keepdims=True)
    cnt_ref[...] = jnp.broadcast_to(run_ref[...], cnt_ref.shape)
    route_ref[...] = jnp.zeros_like(route_ref)
    route_ref[0:1, :] = i1 - EXPERT_ROW0
    route_ref[1:2, :] = i2 - EXPERT_ROW0
    route_ref[2:3, :] = rank1.astype(jnp.int32)
    route_ref[3:4, :] = rank2.astype(jnp.int32)

    eye = jnp.where(tok_u == tok_n, 1.0, 0.0).astype(BF16)
    for w_row, w_ref in ((w1, w1_ref), (w2, w2_ref)):
        parts = _split3(jnp.broadcast_to(w_row, (ROUTER_LANES, tm)))
        w_ref[0] = _dot_nt(eye, parts[0]) + _dot_nt(eye, parts[1]) + _dot_nt(eye, parts[2])


def _merge(ym, oa, bg, x, mod, wbm, wba, wo, n2, wr, br):
    B, L, D = x.shape
    tm = min(L, 512)
    nt = L // tm
    full = lambda b, i: (0, 0)
    tok = lambda b, i: (b, i, 0)
    return pl.pallas_call(
        _merge_kernel,
        grid=(B, L // tm),
        in_specs=[
            pl.BlockSpec((1, tm, D), tok),
            pl.BlockSpec((1, tm, D), tok),
            pl.BlockSpec((1, tm, 2 * D), tok),
            pl.BlockSpec((1, tm, D), tok),
            pl.BlockSpec((1, 6, D), lambda b, i: (b, 0, 0)),
            pl.BlockSpec((D, D), full),
            pl.BlockSpec((D, D), full),
            pl.BlockSpec((D, D), full),
            pl.BlockSpec((1, D), full),
            pl.BlockSpec((ROUTER_ROWS, D), full),
            pl.BlockSpec((ROUTER_ROWS, 1), full),
        ],
        out_specs=[
            pl.BlockSpec((1, tm, D), tok),
            pl.BlockSpec((1, tm, D), tok),
            pl.BlockSpec((ROUTE_ROWS, tm), lambda b, i: (0, b * nt + i)),
            pl.BlockSpec((1, tm, ROUTER_LANES), tok),
            pl.BlockSpec((1, tm, ROUTER_LANES), tok),
            pl.BlockSpec((ROUTER_ROWS, ROUTER_LANES), full),
        ],
        out_shape=[
            jax.ShapeDtypeStruct((B, L, D), F32),
            jax.ShapeDtypeStruct((B, L, D), F32),
            jax.ShapeDtypeStruct((ROUTE_ROWS, B * L), jnp.int32),
            jax.ShapeDtypeStruct((B, L, ROUTER_LANES), F32),
            jax.ShapeDtypeStruct((B, L, ROUTER_LANES), F32),
            jax.ShapeDtypeStruct((ROUTER_ROWS, ROUTER_LANES), F32),
        ],
        scratch_shapes=[pltpu.VMEM((ROUTER_ROWS, 1), F32)],
        compiler_params=_cparams("arbitrary", "arbitrary"),
        name="merge_router",
    )(ym, oa, bg, x, mod, wbm, wba, wo, n2.reshape(1, D), wr, br)


def _load_route(r_ref, r_smem, rsem):
    cp = pltpu.make_async_copy(r_ref, r_smem, rsem)
    cp.start()
    cp.wait()


def _dispatch_kernel(off_ref, nu_ref, r_ref, f_ref, xs_ref, r_smem, zbuf, sem, rsem, *, tm):
    T = EXPERT_TILE

    @pl.when(pl.program_id(0) == 0)
    def _():
        zbuf[...] = jnp.zeros_like(zbuf)

        def zero_tile(t, carry):
            cp = pltpu.make_async_copy(zbuf, xs_ref.at[pl.ds(pl.multiple_of(t * T, T), T), :], sem)
            cp.start()
            cp.wait()
            return carry

        lax.fori_loop(nu_ref[0], xs_ref.shape[0] // T, zero_tile, 0)
        for e in range(N_EXPERTS):
            start = off_ref[e]
            end = off_ref[e + 1]

            @pl.when(end > start)
            def _():
                cp = pltpu.make_async_copy(zbuf, xs_ref.at[pl.ds(pl.multiple_of(end - T, T), T), :], sem)
                cp.start()
                cp.wait()

    _load_route(r_ref, r_smem, rsem)

    def body(t, carry):
        for k in range(2):
            dst = off_ref[r_smem[k, t]] + r_smem[2 + k, t]
            pltpu.make_async_copy(f_ref.at[pl.ds(t, 1), :], xs_ref.at[pl.ds(dst, 1), :], sem).start()
        return carry

    lax.fori_loop(0, tm, body, 0, unroll=8)
    for _ in range(2):
        pltpu.make_async_copy(f_ref, f_ref, sem).wait()


def _dispatch(off, n_used, route, f, n_rows):
    N, D = f.shape
    tm = min(N, 512)
    return pl.pallas_call(
        functools.partial(_dispatch_kernel, tm=tm),
        grid_spec=pltpu.PrefetchScalarGridSpec(
            num_scalar_prefetch=2,
            grid=(N // tm,),
            in_specs=[
                pl.BlockSpec((ROUTE_ROWS, tm), lambda i, off, nu: (0, i)),
                pl.BlockSpec((tm, D), lambda i, off, nu: (i, 0)),
            ],
            out_specs=pl.BlockSpec(memory_space=pl.ANY),
            scratch_shapes=[
                pltpu.SMEM((ROUTE_ROWS, tm), jnp.int32),
                pltpu.VMEM((EXPERT_TILE, D), F32),
                pltpu.SemaphoreType.DMA(()),
                pltpu.SemaphoreType.DMA(()),
            ],
        ),
        out_shape=jax.ShapeDtypeStruct((n_rows, D), F32),
        compiler_params=_cparams("arbitrary"),
        name="moe_dispatch",
    )(off, n_used, route, f)


def _expert_kernel(te_ref, nu_ref, xs_ref, wg_ref, wu_ref, wd_ref, ys_ref):
    used = pl.program_id(0) < nu_ref[0]

    @pl.when(used)
    def _():
        x = xs_ref[...].astype(BF16)
        hidden = (_silu(_dot(x, wg_ref[0])) * _dot(x, wu_ref[0])).astype(BF16)
        ys_ref[...] = _dot(hidden, wd_ref[0])

    @pl.when(jnp.logical_not(used))
    def _():
        ys_ref[...] = jnp.zeros_like(ys_ref)


def _experts(tile_expert, n_used, xs, wg, wu, wd):
    P, D = xs.shape
    E, _, DE = wg.shape
    T = EXPERT_TILE
    row_map = lambda t, te, nu: (t, 0)
    w_map = lambda t, te, nu: (te[t], 0, 0)
    return pl.pallas_call(
        _expert_kernel,
        grid_spec=pltpu.PrefetchScalarGridSpec(
            num_scalar_prefetch=2,
            grid=(P // T,),
            in_specs=[
                pl.BlockSpec((T, D), row_map),
                pl.BlockSpec((1, D, DE), w_map),
                pl.BlockSpec((1, D, DE), w_map),
                pl.BlockSpec((1, DE, D), w_map),
            ],
            out_specs=pl.BlockSpec((T, D), row_map),
        ),
        out_shape=jax.ShapeDtypeStruct((P, D), F32),
        compiler_params=_cparams("arbitrary"),
        name="moe_experts",
    )(tile_expert, n_used, xs, wg, wu, wd)


def _combine_kernel(off_ref, r_ref, hmid_ref, mod_ref, w1_ref, w2_ref, ys_ref, o_ref, r_smem, y1, y2, sem, rsem,
                    *, tm):
    _load_route(r_ref, r_smem, rsem)

    def body(t, carry):
        for k, buf in ((0, y1), (1, y2)):
            src = off_ref[r_smem[k, t]] + r_smem[2 + k, t]
            pltpu.make_async_copy(ys_ref.at[pl.ds(src, 1), :], buf.at[pl.ds(t, 1), :], sem).start()
        return carry

    lax.fori_loop(0, tm, body, 0, unroll=8)
    pltpu.make_async_copy(y1, y1, sem).wait()
    pltpu.make_async_copy(y2, y2, sem).wait()
    w1 = w1_ref[0]
    w2 = w2_ref[0]
    D = o_ref.shape[-1]
    for c in range(D // ROUTER_LANES):
        sl = slice(c * ROUTER_LANES, (c + 1) * ROUTER_LANES)
        moe = w1 * y1[:, sl] + w2 * y2[:, sl]
        o_ref[0, :, sl] = hmid_ref[0, :, sl] + mod_ref[0, 5:6, sl] * moe


def _combine(off, route, hmid, mod, w1, w2, ys):
    B, L, D = hmid.shape
    tm = min(L, 512)
    nt = L // tm
    tok = lambda b, i, off: (b, i, 0)
    return pl.pallas_call(
        functools.partial(_combine_kernel, tm=tm),
        grid_spec=pltpu.PrefetchScalarGridSpec(
            num_scalar_prefetch=1,
            grid=(B, nt),
            in_specs=[
                pl.BlockSpec((ROUTE_ROWS, tm), lambda b, i, off: (0, b * nt + i)),
                pl.BlockSpec((1, tm, D), tok),
                pl.BlockSpec((1, 6, D), lambda b, i, off: (b, 0, 0)),
                pl.BlockSpec((1, tm, ROUTER_LANES), tok),
                pl.BlockSpec((1, tm, ROUTER_LANES), tok),
                pl.BlockSpec(memory_space=pl.ANY),
            ],
            out_specs=pl.BlockSpec((1, tm, D), tok),
            scratch_shapes=[
                pltpu.SMEM((ROUTE_ROWS, tm), jnp.int32),
                pltpu.VMEM((tm, D), F32),
                pltpu.VMEM((tm, D), F32),
                pltpu.SemaphoreType.DMA(()),
                pltpu.SemaphoreType.DMA(()),
            ],
        ),
        out_shape=jax.ShapeDtypeStruct((B, L, D), F32),
        compiler_params=_cparams("arbitrary", "arbitrary"),
        name="moe_combine",
    )(off, route, hmid, mod, w1, w2, ys)


def _expert_layout(cnt, n_pairs):
    T = EXPERT_TILE
    counts = cnt[EXPERT_ROW0:, 0].astype(jnp.int32)
    ends = jnp.cumsum((counts + T - 1) // T).astype(jnp.int32)
    off = jnp.concatenate([jnp.zeros((1,), jnp.int32), ends * T])
    n_tiles = n_pairs // T + N_EXPERTS
    t_idx = jnp.arange(n_tiles, dtype=jnp.int32)
    tile_expert = jnp.minimum(jnp.sum(t_idx[:, None] >= ends[None, :], axis=1), N_EXPERTS - 1).astype(jnp.int32)
    n_used = ends[-1]
    tile_expert = jnp.where(t_idx < n_used, tile_expert, jnp.take(tile_expert, n_used - 1))
    return off, tile_expert, n_used.reshape(1), n_tiles * T


def _gate_weights(w_gate, b_gate):
    D = w_gate.shape[0]
    wt = w_gate.T.reshape(4, M_HEADS, D).transpose(1, 0, 2)
    wt = jnp.concatenate([wt, wt[:, 1:2], wt[:, 3:4], jnp.zeros((M_HEADS, GATE_ROWS - 6, D), wt.dtype)], axis=1)
    bt = b_gate.reshape(4, M_HEADS).T
    bt = jnp.concatenate([bt, bt[:, 1:2], bt[:, 3:4], jnp.zeros((M_HEADS, GATE_ROWS - 6), bt.dtype)], axis=1)
    return wt.reshape(M_HEADS * GATE_ROWS, D), bt.reshape(M_HEADS * GATE_ROWS)


def kernel(x, c, ctx, c_ctx, w_ada, b_ada, norm1_w, w_in, b_mgate, q_norm_w, k_norm_w, mh_norm_w, w_branch_m,
           w_branch_a, w_out, norm2_w, w_rg, b_rg, w_re, b_re, w_e_gate, w_e_up, w_e_down):
    B, L, D = x.shape
    depth = w_ada.shape[0]
    assert depth == 1, "context-stream update between layers is not implemented"
    l = 0
    mw = M_HEADS * HEAD_DIM
    aq = A_HEADS * HEAD_DIM
    akv = A_KV_HEADS * HEAD_DIM
    o_mq, o_mk, o_mv, o_og = 0, mw, 2 * mw, 3 * mw
    o_g = 4 * mw
    o_aq = o_g + 4 * M_HEADS
    o_ak = o_aq + aq
    o_av = o_ak + akv
    o_bg = o_av + akv
    scale = HEAD_DIM ** -0.5

    rows = ((B + 1 + 7) // 8) * 8
    cc = jnp.concatenate([c, c_ctx[None, :], jnp.zeros((rows - B - 1, D), F32)], axis=0)
    mod = _ada_mod(cc, w_ada[l], b_ada[l]).reshape(rows, 6, D)

    xn = _norm_mod(x, mod, norm1_w[l], lambda b: b)
    cn = _norm_mod(ctx, mod, norm1_w[l], lambda b: B)

    w = w_in[l]
    wb = lambda a, b: w[:, a:b].astype(BF16)
    cos, sin = _rope_tables(L)
    wgt, bgt = _gate_weights(w[:, o_g:o_aq], b_mgate[l])
    wgt = wgt.astype(BF16)
    wkt = w[:, o_mk:o_mv].T.astype(BF16)

    mq = _proj(xn, wb(o_mq, o_mk), scale=scale, name="proj_mq")
    mkt = _proj_t(xn, wkt, chunk=MLSTM_CHUNK, name="proj_mkt")
    mv = _proj(xn, wb(o_mv, o_og), name="proj_mv")
    og = _proj(xn, wb(o_og, o_g), act="sigmoid", name="proj_og")
    gt = _proj_gates(xn, wgt, bgt, chunk=MLSTM_CHUNK, name="proj_gates")
    qa = _proj_normrope(xn, wb(o_aq, o_ak), q_norm_w[l], cos, sin, scale=scale, rope=True, name="proj_aq")
    ka = _proj_normrope(xn, wb(o_ak, o_av), k_norm_w[l], cos, sin, scale=1.0, rope=True, name="proj_ak")
    va = _proj(xn, wb(o_av, o_bg), name="proj_av")
    bg = _proj(xn, wb(o_bg, o_bg + 2 * D), act="sigmoid", name="proj_bg")

    lctx = ctx.shape[1]
    cmkt = _proj_t(cn, wkt, chunk=MLSTM_CHUNK, name="proj_cmkt")
    cmv = _proj(cn, wb(o_mv, o_og), name="proj_cmv")
    cgt = _proj_gates(cn, wgt, bgt, chunk=MLSTM_CHUNK, name="proj_cgates")
    cka = _proj_normrope(cn, wb(o_ak, o_av), k_norm_w[l], cos[:lctx], sin[:lctx], scale=1.0, rope=False, name="proj_cak")
    cva = _proj(cn, wb(o_av, o_bg), name="proj_cav")

    ga = _gate_prep(gt, "gate_prep")
    gca = _gate_prep(cgt, "gate_prep_ctx")
    ym = _mlstm(mq, mkt, mv, ga, og, cmkt, cmv, gca, mh_norm_w[l])
    oa = _attention(qa, ka, va, cka, cva)

    pad_w = jnp.zeros((EXPERT_ROW0 - N_GROUPS, D), F32)
    pad_b = jnp.zeros((EXPERT_ROW0 - N_GROUPS,), F32)
    wr = jnp.concatenate([w_rg[l].T, pad_w, w_re[l].T], axis=0)
    br = jnp.concatenate([b_rg[l], pad_b, b_re[l]]).reshape(ROUTER_ROWS, 1)
    hmid, f, route, w1, w2, cnt = _merge(ym, oa, bg, x, mod, w_branch_m[l].astype(BF16), w_branch_a[l].astype(BF16),
                                         w_out[l].astype(BF16), norm2_w[l], wr, br)

    off, tile_expert, n_used, n_rows = _expert_layout(cnt, 2 * B * L)
    xs = _dispatch(off, n_used, route, f.reshape(B * L, D), n_rows)
    ys = _experts(tile_expert, n_used, xs, w_e_gate[l].astype(BF16), w_e_up[l].astype(BF16),
                  w_e_down[l].astype(BF16))
    return _combine(off, route, hmid, mod, w1, w2, ys)
```

```python
import functools

import jax
import jax.numpy as jnp
from jax import lax
from jax.experimental import pallas as pl
from jax.experimental.pallas import tpu as pltpu

F32 = jnp.float32
BF16 = jnp.bfloat16

EPS = 1e-6
GRID_W = 64
ROPE_THETA = 10000.0
HEAD_DIM = 128
M_HEADS = 8
A_HEADS = 8
A_KV_HEADS = 2
A_GROUP = A_HEADS // A_KV_HEADS
N_GROUPS = 4
EXPERTS_PER_GROUP = 8
N_EXPERTS = N_GROUPS * EXPERTS_PER_GROUP
M_INIT = -1e30
MLSTM_CHUNK = 128
GATE_ROWS = 8
ROUTER_LANES = 128
EXPERT_ROW0 = 8
ROUTER_ROWS = EXPERT_ROW0 + N_EXPERTS
EXPERT_TILE = 512
ROUTE_ROWS = 16
SLOT_ROWS = 8
VMEM_LIMIT_BYTES = 48 * 1024 * 1024

NT_DIMS = (((1,), (1,)), ((), ()))


def _cparams(*sem, flags=None):
    return pltpu.CompilerParams(dimension_semantics=sem, vmem_limit_bytes=VMEM_LIMIT_BYTES, flags=flags)


def _dot(a, b):
    return jnp.dot(a, b, preferred_element_type=F32)


def _dot_nt(a, b):
    return lax.dot_general(a, b, NT_DIMS, preferred_element_type=F32)


def _split3(x):
    hi = x.astype(BF16)
    r1 = x - hi.astype(F32)
    mid = r1.astype(BF16)
    lo = (r1 - mid.astype(F32)).astype(BF16)
    return hi, mid, lo


def _silu(x):
    return x * jax.nn.sigmoid(x)


def _ada_kernel(c_ref, w_ref, b_ref, o_ref):
    s = _silu(c_ref[...])
    parts = _split3(s)
    w = w_ref[...]
    w_hi = w.astype(BF16)
    w_lo = (w - w_hi.astype(F32)).astype(BF16)
    acc = _dot(parts[0], w_hi) + _dot(parts[1], w_hi) + _dot(parts[0], w_lo)
    o_ref[...] = acc + b_ref[...]


def _ada_mod(cc, w, b):
    rows, d = cc.shape
    n = w.shape[1]
    tn = min(n, 1536)
    return pl.pallas_call(
        _ada_kernel,
        grid=(n // tn,),
        in_specs=[
            pl.BlockSpec((rows, d), lambda j: (0, 0)),
            pl.BlockSpec((d, tn), lambda j: (0, j)),
            pl.BlockSpec((1, tn), lambda j: (0, j)),
        ],
        out_specs=pl.BlockSpec((rows, tn), lambda j: (0, j)),
        out_shape=jax.ShapeDtypeStruct((rows, n), F32),
        compiler_params=_cparams("parallel"),
        name="ada_mod",
    )(cc, w, b.reshape(1, n))


def _norm_mod_kernel(x_ref, mod_ref, nw_ref, o_ref):
    x = x_ref[0]
    var = jnp.mean(x * x, axis=-1, keepdims=True)
    y = x * lax.rsqrt(var + EPS) * nw_ref[...]
    sh = mod_ref[0, 0:1, :]
    sc = mod_ref[0, 1:2, :]
    o_ref[0] = (y * (1.0 + sc) + sh).astype(o_ref.dtype)


def _norm_mod(x, mod, nw, mod_row):
    B, L, D = x.shape
    tm = min(L, 512)
    return pl.pallas_call(
        _norm_mod_kernel,
        grid=(B, L // tm),
        in_specs=[
            pl.BlockSpec((1, tm, D), lambda b, i: (b, i, 0)),
            pl.BlockSpec((1, 6, D), lambda b, i: (mod_row(b), 0, 0)),
            pl.BlockSpec((1, D), lambda b, i: (0, 0)),
        ],
        out_specs=pl.BlockSpec((1, tm, D), lambda b, i: (b, i, 0)),
        out_shape=jax.ShapeDtypeStruct((B, L, D), BF16),
        compiler_params=_cparams("parallel", "parallel"),
        name="norm_mod",
    )(x, mod, nw.reshape(1, D))


def _proj_kernel(x_ref, w_ref, o_ref, *, act, scale):
    acc = _dot(x_ref[0], w_ref[...])
    if act == "sigmoid":
        acc = jax.nn.sigmoid(acc)
    if scale != 1.0:
        acc = acc * scale
    o_ref[0] = acc.astype(o_ref.dtype)


def _proj(xn, w, *, act="none", scale=1.0, name):
    B, L, D = xn.shape
    N = w.shape[1]
    tm = min(L, 512)
    tn = min(N, 1024)
    return pl.pallas_call(
        functools.partial(_proj_kernel, act=act, scale=scale),
        grid=(B, N // tn, L // tm),
        in_specs=[
            pl.BlockSpec((1, tm, D), lambda b, j, i: (b, i, 0)),
            pl.BlockSpec((D, tn), lambda b, j, i: (0, j)),
        ],
        out_specs=pl.BlockSpec((1, tm, tn), lambda b, j, i: (b, i, j)),
        out_shape=jax.ShapeDtypeStruct((B, L, N), BF16),
        compiler_params=_cparams("parallel", "parallel", "parallel"),
        name=name,
    )(xn, w)


def _proj_t_kernel(w_ref, x_ref, o_ref, *, nchunk, chunk):
    acc = _dot_nt(w_ref[...], x_ref[0])
    for c in range(nchunk):
        o_ref[0, c] = acc[:, c * chunk:(c + 1) * chunk].astype(o_ref.dtype)


def _proj_t(xn, wt, *, chunk, name):
    B, L, D = xn.shape
    R = wt.shape[0]
    tm = min(L, 512)
    nchunk = tm // chunk
    return pl.pallas_call(
        functools.partial(_proj_t_kernel, nchunk=nchunk, chunk=chunk),
        grid=(B, L // tm),
        in_specs=[
            pl.BlockSpec((R, D), lambda b, i: (0, 0)),
            pl.BlockSpec((1, tm, D), lambda b, i: (b, i, 0)),
        ],
        out_specs=pl.BlockSpec((1, nchunk, R, chunk), lambda b, i: (b, i, 0, 0)),
        out_shape=jax.ShapeDtypeStruct((B, L // chunk, R, chunk), BF16),
        compiler_params=_cparams("parallel", "parallel"),
        name=name,
    )(wt, xn)


def _proj_gates_kernel(w_ref, x_ref, b_ref, o_ref, *, nchunk, chunk):
    acc = _dot_nt(w_ref[...], x_ref[0]) + b_ref[...]
    for h in range(M_HEADS):
        for c in range(nchunk):
            o_ref[0, h, c] = acc[h * GATE_ROWS:(h + 1) * GATE_ROWS, c * chunk:(c + 1) * chunk]


def _proj_gates(xn, wt, bias, *, chunk, name):
    B, L, D = xn.shape
    R = wt.shape[0]
    tm = min(L, 512)
    nchunk = tm // chunk
    return pl.pallas_call(
        functools.partial(_proj_gates_kernel, nchunk=nchunk, chunk=chunk),
        grid=(B, L // tm),
        in_specs=[
            pl.BlockSpec((R, D), lambda b, i: (0, 0)),
            pl.BlockSpec((1, tm, D), lambda b, i: (b, i, 0)),
            pl.BlockSpec((R, 1), lambda b, i: (0, 0)),
        ],
        out_specs=pl.BlockSpec((1, M_HEADS, nchunk, GATE_ROWS, chunk), lambda b, i: (b, 0, i, 0, 0)),
        out_shape=jax.ShapeDtypeStruct((B, M_HEADS, L // chunk, GATE_ROWS, chunk), F32),
        compiler_params=_cparams("parallel", "parallel"),
        name=name,
    )(wt, xn, bias.reshape(R, 1))


def _rope_partner(j):
    quarter = HEAD_DIM // 4
    return jnp.where((j % (2 * quarter)) < quarter, j + quarter, j - quarter)


def _split2_lanes(x):
    hi = x.astype(BF16)
    return jnp.concatenate([hi, (x - hi.astype(F32)).astype(BF16)], axis=1)


def _proj_normrope_kernel(x_ref, w_ref, cw_ref, sw_ref, o_ref, *, nheads, rope):
    acc = _dot(x_ref[0], w_ref[...])
    cw = cw_ref[...]
    avg = jnp.full((2 * HEAD_DIM, HEAD_DIM), 1.0 / HEAD_DIM, BF16)
    if rope:
        sw = sw_ref[...]
        src = lax.broadcasted_iota(jnp.int32, (2 * HEAD_DIM, HEAD_DIM), 0) % HEAD_DIM
        dst = lax.broadcasted_iota(jnp.int32, (2 * HEAD_DIM, HEAD_DIM), 1)
        perm = jnp.where(src == _rope_partner(dst), 1.0, 0.0).astype(BF16)
    for h in range(nheads):
        a = acc[:, h * HEAD_DIM:(h + 1) * HEAD_DIM]
        var = _dot(_split2_lanes(a * a), avg)
        y = a * cw
        if rope:
            y = y + _dot(_split2_lanes(a), perm) * sw
        o_ref[0, :, h * HEAD_DIM:(h + 1) * HEAD_DIM] = (y * lax.rsqrt(var + EPS)).astype(o_ref.dtype)


def _proj_normrope(xn, w, cw, sw, *, rope, name):
    B, L, D = xn.shape
    N = w.shape[1]
    nheads = N // HEAD_DIM
    tm = min(L, 512)
    return pl.pallas_call(
        functools.partial(_proj_normrope_kernel, nheads=nheads, rope=rope),
        grid=(B, L // tm),
        in_specs=[
            pl.BlockSpec((1, tm, D), lambda b, i: (b, i, 0)),
            pl.BlockSpec((D, N), lambda b, i: (0, 0)),
            pl.BlockSpec((tm, HEAD_DIM), lambda b, i: (i, 0)),
            pl.BlockSpec((tm, HEAD_DIM), lambda b, i: (i, 0)),
        ],
        out_specs=pl.BlockSpec((1, tm, N), lambda b, i: (b, i, 0)),
        out_shape=jax.ShapeDtypeStruct((B, L, N), BF16),
        compiler_params=_cparams("parallel", "parallel"),
        name=name,
    )(xn, w, cw, sw)


def _rope_tables(L):
    rows = L // GRID_W
    row = jnp.repeat(jnp.arange(rows), GRID_W).astype(F32)
    col = jnp.tile(jnp.arange(GRID_W), rows).astype(F32)
    half = HEAD_DIM // 2
    inv = ROPE_THETA ** (-jnp.arange(0, half, 2, dtype=F32) / half)
    ang_r = row[:, None] * inv[None, :]
    ang_c = col[:, None] * inv[None, :]
    cr, sr, cc, sc = jnp.cos(ang_r), jnp.sin(ang_r), jnp.cos(ang_c), jnp.sin(ang_c)
    return jnp.concatenate([cr, cr, cc, cc], axis=-1), jnp.concatenate([-sr, sr, -sc, sc], axis=-1)


def _log_sigmoid(x):
    return jnp.minimum(x, 0.0) - jnp.log1p(jnp.exp(-jnp.abs(x)))


def _tri_masks(n):
    row_i = lax.broadcasted_iota(jnp.int32, (n, n), 0)
    col_i = lax.broadcasted_iota(jnp.int32, (n, n), 1)
    return row_i <= col_i, row_i >= col_i


def _gate_prep_kernel(g_ref, a_ref):
    _, H, nc, R8, C = g_ref.shape
    rows = H * nc * R8
    g = g_ref[0].reshape(rows, C)
    le, ge = _tri_masks(C)
    upper = jnp.where(le, 1.0, 0.0).astype(BF16)
    lower = jnp.where(ge, 1.0, 0.0).astype(BF16)
    lf = _log_sigmoid(g)
    parts = _split3(lf)
    pre = _dot(parts[0], upper) + _dot(parts[1], upper) + _dot(parts[2], upper)
    suf = _dot(parts[0], lower) + _dot(parts[1], lower) + _dot(parts[2], lower)
    rtype = lax.broadcasted_iota(jnp.int32, (rows, C), 0) % R8
    a = jnp.where(rtype == 1, pre, jnp.where(rtype == 3, suf, jnp.where(rtype >= 4, lf, g)))
    a_ref[0] = a.reshape(H, nc, R8, C)


def _gate_prep(g, name):
    B, H, nc, R8, C = g.shape
    blk = pl.BlockSpec((1, H, nc, R8, C), lambda b: (b, 0, 0, 0, 0))
    return pl.pallas_call(
        _gate_prep_kernel,
        grid=(B,),
        in_specs=[blk],
        out_specs=blk,
        out_shape=jax.ShapeDtypeStruct(g.shape, F32),
        compiler_params=_cparams("parallel"),
        name=name,
    )(g)


def _mlstm_kernel(q_ref, kt_ref, v_ref, a_ref, og_ref, kct_ref, vc_ref, ac_ref, nw_ref, o_ref, cin_ref,
                  *, nc, nctx):
    C = MLSTM_CHUNK
    D = HEAD_DIM
    le, ge = _tri_masks(C)
    le_bf = jnp.where(le, 1.0, 0.0).astype(BF16)
    ge_bf = jnp.where(ge, 1.0, 0.0).astype(BF16)
    ones_blk = jnp.ones((C, D), BF16)

    def gates(ref, j, fwd):
        ir, fr = (0, 1) if fwd else (2, 3)
        b_row = ref[0, 0, j, fr:fr + 1, :]
        a_row = ref[0, 0, j, ir:ir + 1, :] - b_row
        return a_row, (b_row[:, C - 1:C] if fwd else b_row[:, 0:1])

    def vaug_of(ref, j):
        return jnp.concatenate([ref[0, j * C:(j + 1) * C, :], ones_blk], axis=1)

    def update(cst, m_prev, kt, vaug, a_row, btot):
        mc = jnp.maximum(jnp.max(a_row, axis=1, keepdims=True), m_prev)
        w_in = jnp.exp(a_row - mc)
        ktw = (kt.astype(F32) * w_in).astype(BF16)
        return jnp.exp(m_prev - mc) * cst + _dot(ktw, vaug), btot + mc

    m_in = {}
    for d, fwd in ((0, True), (1, False)):
        cst = jnp.zeros((D, 2 * D), F32)
        m = jnp.full((1, 1), M_INIT, F32)
        for j in (range(nctx) if fwd else reversed(range(nctx))):
            a_row, btot = gates(ac_ref, j, fwd)
            cst, m = update(cst, m, kct_ref[0, j], vaug_of(vc_ref, j), a_row, btot)
        order = list(range(nc) if fwd else reversed(range(nc)))
        for idx, j in enumerate(order):
            cin_ref[d, j] = cst.astype(BF16)
            m_in[d, j] = m
            if idx + 1 < nc:
                a_row, btot = gates(a_ref, j, fwd)
                cst, m = update(cst, m, kt_ref[0, j], vaug_of(v_ref, j), a_row, btot)

    def out_step(d, j, fwd, q, kt, vaug):
        a_row, _ = gates(a_ref, j, fwd)
        m_prev = m_in[d, j]
        nlf = -a_ref[0, 0, j, (4 if fwd else 5):(5 if fwd else 6), :]
        nlf_hi = nlf.astype(BF16)
        nlf_lo = (nlf - nlf_hi.astype(F32)).astype(BF16)
        vis = ge_bf if fwd else le_bf
        nb = _dot_nt(vis, jnp.broadcast_to(nlf_hi, (D, C))) + _dot_nt(vis, jnp.broadcast_to(nlf_lo, (D, C)))
        a_mat = jnp.where(ge if fwd else le, a_row, -jnp.inf)
        m_q = jnp.maximum(jnp.max(a_mat, axis=1, keepdims=True), m_prev)
        bm = jnp.broadcast_to(m_q, (C, D))
        w_intra = jnp.exp(a_mat - bm)
        p = (_dot(q, kt) * w_intra).astype(BF16)
        qc = _dot(q, cin_ref[d, j])
        pv = _dot(p, vaug)
        w_inter = jnp.exp(m_prev - bm)
        num = w_inter * qc[:, :D] + pv[:, :D]
        den = w_inter * qc[:, D:] + pv[:, D:]
        return num / jnp.maximum(jnp.abs(den), jnp.exp(nb - bm))

    nw = nw_ref[...]
    for j in range(nc):
        sl = slice(j * C, (j + 1) * C)
        q = q_ref[0, sl, :]
        kt = kt_ref[0, j]
        vaug = vaug_of(v_ref, j)
        h = out_step(0, j, True, q, kt, vaug) + out_step(1, j, False, q, kt, vaug)
        var = jnp.mean(h * h, axis=-1, keepdims=True)
        y = h * lax.rsqrt(var + EPS) * nw
        o_ref[0, sl, :] = (og_ref[0, sl, :].astype(F32) * y).astype(o_ref.dtype)


def _mlstm(q, kt, v, a, og, kct, vc, ac, mh_norm_w):
    B, L, W = q.shape
    C = MLSTM_CHUNK
    assert C == HEAD_DIM
    nc = L // C
    lctx = vc.shape[1]
    nctx = lctx // C
    H = W // HEAD_DIM
    return pl.pallas_call(
        functools.partial(_mlstm_kernel, nc=nc, nctx=nctx),
        grid=(B, H),
        in_specs=[
            pl.BlockSpec((1, L, HEAD_DIM), lambda b, h: (b, 0, h)),
            pl.BlockSpec((1, nc, HEAD_DIM, C), lambda b, h: (b, 0, h, 0)),
            pl.BlockSpec((1, L, HEAD_DIM), lambda b, h: (b, 0, h)),
            pl.BlockSpec((1, 1, nc, GATE_ROWS, C), lambda b, h: (b, h, 0, 0, 0)),
            pl.BlockSpec((1, L, HEAD_DIM), lambda b, h: (b, 0, h)),
            pl.BlockSpec((1, nctx, HEAD_DIM, C), lambda b, h: (b, 0, h, 0)),
            pl.BlockSpec((1, lctx, HEAD_DIM), lambda b, h: (b, 0, h)),
            pl.BlockSpec((1, 1, nctx, GATE_ROWS, C), lambda b, h: (b, h, 0, 0, 0)),
            pl.BlockSpec((1, HEAD_DIM), lambda b, h: (0, h)),
        ],
        out_specs=pl.BlockSpec((1, L, HEAD_DIM), lambda b, h: (b, 0, h)),
        out_shape=jax.ShapeDtypeStruct((B, L, W), BF16),
        scratch_shapes=[pltpu.VMEM((2, nc, HEAD_DIM, 2 * HEAD_DIM), BF16)],
        compiler_params=_cparams("parallel", "parallel"),
        name="mlstm",
    )(q, kt, v, a, og, kct, vc, ac, mh_norm_w.reshape(1, W))


def _attn_kernel(q_ref, k_ref, v_ref, kc_ref, vc_ref, o_ref):
    k = k_ref[0]
    kc = kc_ref[0]
    v = jnp.concatenate([v_ref[0], jnp.ones(v_ref.shape[1:], BF16)], axis=1)
    vc = jnp.concatenate([vc_ref[0], jnp.ones(vc_ref.shape[1:], BF16)], axis=1)
    for h in range(A_GROUP):
        q = q_ref[0, :, h * HEAD_DIM:(h + 1) * HEAD_DIM]
        s1 = _dot_nt(q, k)
        s2 = _dot_nt(q, kc)
        m = jnp.maximum(jnp.max(s1, axis=1, keepdims=True), jnp.max(s2, axis=1, keepdims=True))
        ol = _dot(jnp.exp(s1 - m).astype(BF16), v) + _dot(jnp.exp(s2 - m).astype(BF16), vc)
        o_ref[0, :, h * HEAD_DIM:(h + 1) * HEAD_DIM] = (ol[:, :HEAD_DIM] / ol[:, HEAD_DIM:]).astype(o_ref.dtype)


def _attention(q, k, v, kc, vc):
    B, L, W = q.shape
    lctx = kc.shape[1]
    tq = min(L, 256)
    gw = A_GROUP * HEAD_DIM
    return pl.pallas_call(
        _attn_kernel,
        grid=(B, A_KV_HEADS, L // tq),
        in_specs=[
            pl.BlockSpec((1, tq, gw), lambda b, g, i: (b, i, g)),
            pl.BlockSpec((1, L, HEAD_DIM), lambda b, g, i: (b, 0, g)),
            pl.BlockSpec((1, L, HEAD_DIM), lambda b, g, i: (b, 0, g)),
            pl.BlockSpec((1, lctx, HEAD_DIM), lambda b, g, i: (b, 0, g)),
            pl.BlockSpec((1, lctx, HEAD_DIM), lambda b, g, i: (b, 0, g)),
        ],
        out_specs=pl.BlockSpec((1, tq, gw), lambda b, g, i: (b, i, g)),
        out_shape=jax.ShapeDtypeStruct((B, L, W), BF16),
        compiler_params=_cparams("parallel", "parallel", "parallel"),
        name="attention",
    )(q, k, v, kc, vc)


def _merge_kernel(ym_ref, oa_ref, bg_ref, x_ref, mod_ref, wbm_ref, wba_ref, wo_ref, n2_ref, wr_ref, br_ref,
                  hmid_ref, f_ref, route_ref, w1_ref, w2_ref, cnt_ref, run_ref):
    D = x_ref.shape[-1]
    tm = x_ref.shape[1]

    @pl.when((pl.program_id(0) == 0) & (pl.program_id(1) == 0))
    def _():
        run_ref[...] = jnp.zeros_like(run_ref)

    ym = _dot(ym_ref[0], wbm_ref[...])
    ya = _dot(oa_ref[0], wba_ref[...])
    g0 = bg_ref[0, :, :D].astype(F32)
    g1 = bg_ref[0, :, D:].astype(F32)
    mix = _dot((g0 * ym + g1 * ya).astype(BF16), wo_ref[...])
    hmid = x_ref[0] + mod_ref[0, 2:3, :] * mix
    hmid_ref[0] = hmid
    var = jnp.mean(hmid * hmid, axis=-1, keepdims=True)
    y = hmid * lax.rsqrt(var + EPS) * n2_ref[...]
    f = y * (1.0 + mod_ref[0, 4:5, :]) + mod_ref[0, 3:4, :]
    f_ref[0] = f.astype(f_ref.dtype)

    fp = _split3(f)
    wr = wr_ref[...]
    wr_hi = wr.astype(BF16)
    wr_lo = (wr - wr_hi.astype(F32)).astype(BF16)
    logits = (_dot_nt(wr_hi, fp[0]) + _dot_nt(wr_hi, fp[1]) + _dot_nt(wr_lo, fp[0]) + _dot_nt(wr_hi, fp[2])
              + _dot_nt(wr_lo, fp[1])) + br_ref[...]
    row = lax.broadcasted_iota(jnp.int32, logits.shape, 0)
    big = jnp.int32(ROUTER_ROWS)
    neg = -jnp.inf
    is_grp = row < N_GROUPS
    gl = jnp.where(is_grp, logits, neg)
    gmax = jnp.max(gl, axis=0, keepdims=True)
    gsel = jnp.min(jnp.where(gl == gmax, row, big), axis=0, keepdims=True)
    p_grp = 1.0 / jnp.sum(jnp.where(is_grp, jnp.exp(logits - gmax), 0.0), axis=0, keepdims=True)
    lo = EXPERT_ROW0 + gsel * EXPERTS_PER_GROUP
    in_grp = (row >= lo) & (row < lo + EXPERTS_PER_GROUP)
    el = jnp.where(in_grp, logits, neg)
    v1 = jnp.max(el, axis=0, keepdims=True)
    i1 = jnp.min(jnp.where(el == v1, row, big), axis=0, keepdims=True)
    el2 = jnp.where(row == i1, neg, el)
    v2 = jnp.max(el2, axis=0, keepdims=True)
    i2 = jnp.min(jnp.where(el2 == v2, row, big), axis=0, keepdims=True)
    e21 = jnp.exp(v2 - v1)
    w1 = p_grp / (1.0 + e21)
    w2 = p_grp * e21 / (1.0 + e21)

    sel = jnp.where((row == i1) | (row == i2), 1.0, 0.0)
    tok_u = lax.broadcasted_iota(jnp.int32, (tm, tm), 0)
    tok_n = lax.broadcasted_iota(jnp.int32, (tm, tm), 1)
    earlier = jnp.where(tok_u < tok_n, 1.0, 0.0).astype(BF16)
    rank_all = run_ref[...] + _dot(sel.astype(BF16), earlier)
    rank1 = jnp.sum(jnp.where(row == i1, rank_all, 0.0), axis=0, keepdims=True)
    rank2 = jnp.sum(jnp.where(row == i2, rank_all, 0.0), axis=0, keepdims=True)
    run_ref[...] += jnp.sum(sel, axis=1, keepdims=True)
    cnt_ref[...] = jnp.broadcast_to(run_ref[...], cnt_ref.shape)
    route_ref[...] = jnp.zeros_like(route_ref)
    route_ref[0:1, :] = i1 - EXPERT_ROW0
    route_ref[1:2, :] = i2 - EXPERT_ROW0
    route_ref[8:9, :] = rank1.astype(jnp.int32)
    route_ref[9:10, :] = rank2.astype(jnp.int32)

    eye = jnp.where(tok_u == tok_n, 1.0, 0.0).astype(BF16)
    for w_row, w_ref in ((w1, w1_ref), (w2, w2_ref)):
        parts = _split3(jnp.broadcast_to(w_row, (ROUTER_LANES, tm)))
        w_ref[0] = _dot_nt(eye, parts[0]) + _dot_nt(eye, parts[1]) + _dot_nt(eye, parts[2])


def _merge(ym, oa, bg, x, mod, wbm, wba, wo, n2, wr, br):
    B, L, D = x.shape
    tm = min(L, 512)
    nt = L // tm
    full = lambda b, i: (0, 0)
    tok = lambda b, i: (b, i, 0)
    return pl.pallas_call(
        _merge_kernel,
        grid=(B, L // tm),
        in_specs=[
            pl.BlockSpec((1, tm, D), tok),
            pl.BlockSpec((1, tm, D), tok),
            pl.BlockSpec((1, tm, 2 * D), tok),
            pl.BlockSpec((1, tm, D), tok),
            pl.BlockSpec((1, 6, D), lambda b, i: (b, 0, 0)),
            pl.BlockSpec((D, D), full),
            pl.BlockSpec((D, D), full),
            pl.BlockSpec((D, D), full),
            pl.BlockSpec((1, D), full),
            pl.BlockSpec((ROUTER_ROWS, D), full),
            pl.BlockSpec((ROUTER_ROWS, 1), full),
        ],
        out_specs=[
            pl.BlockSpec((1, tm, D), tok),
            pl.BlockSpec((1, tm, D), tok),
            pl.BlockSpec((ROUTE_ROWS, tm), lambda b, i: (0, b * nt + i)),
            pl.BlockSpec((1, tm, ROUTER_LANES), tok),
            pl.BlockSpec((1, tm, ROUTER_LANES), tok),
            pl.BlockSpec((ROUTER_ROWS, ROUTER_LANES), full),
        ],
        out_shape=[
            jax.ShapeDtypeStruct((B, L, D), F32),
            jax.ShapeDtypeStruct((B, L, D), F32),
            jax.ShapeDtypeStruct((ROUTE_ROWS, B * L), jnp.int32),
            jax.ShapeDtypeStruct((B, L, ROUTER_LANES), F32),
            jax.ShapeDtypeStruct((B, L, ROUTER_LANES), F32),
            jax.ShapeDtypeStruct((ROUTER_ROWS, ROUTER_LANES), F32),
        ],
        scratch_shapes=[pltpu.VMEM((ROUTER_ROWS, 1), F32)],
        compiler_params=_cparams("arbitrary", "arbitrary"),
        name="merge_router",
    )(ym, oa, bg, x, mod, wbm, wba, wo, n2.reshape(1, D), wr, br)


def _load_slots(off_ref, r_ref, s_vmem, s_smem, rsem):
    expert = r_ref[0:SLOT_ROWS, :]
    slot = r_ref[SLOT_ROWS:2 * SLOT_ROWS, :]
    for e in range(1, N_EXPERTS):
        slot = slot + jnp.where(expert == e, off_ref[e], 0)
    s_vmem[...] = slot
    cp = pltpu.make_async_copy(s_vmem, s_smem, rsem)
    cp.start()
    cp.wait()


def _dispatch_kernel(off_ref, nu_ref, r_ref, f_ref, xs_ref, s_vmem, s_smem, zbuf, sem, rsem, *, tm):
    T = EXPERT_TILE

    @pl.when(pl.program_id(0) == 0)
    def _():
        zbuf[...] = jnp.zeros_like(zbuf)

        def zero_tile(t, carry):
            cp = pltpu.make_async_copy(zbuf, xs_ref.at[pl.ds(pl.multiple_of(t * T, T), T), :], sem)
            cp.start()
            cp.wait()
            return carry

        lax.fori_loop(nu_ref[0], xs_ref.shape[0] // T, zero_tile, 0)
        for e in range(N_EXPERTS):
            start = off_ref[e]
            end = off_ref[e + 1]

            @pl.when(end > start)
            def _():
                cp = pltpu.make_async_copy(zbuf, xs_ref.at[pl.ds(pl.multiple_of(end - T, T), T), :], sem)
                cp.start()
                cp.wait()

    _load_slots(off_ref, r_ref, s_vmem, s_smem, rsem)
    for t in range(tm):
        for k in range(2):
            pltpu.make_async_copy(f_ref.at[pl.ds(t, 1), :], xs_ref.at[pl.ds(s_smem[k, t], 1), :], sem).start()
    for _ in range(2):
        pltpu.make_async_copy(f_ref, f_ref, sem).wait()


def _dispatch(off, n_used, route, f, n_rows):
    N, D = f.shape
    tm = min(N, 512)
    return pl.pallas_call(
        functools.partial(_dispatch_kernel, tm=tm),
        grid_spec=pltpu.PrefetchScalarGridSpec(
            num_scalar_prefetch=2,
            grid=(N // tm,),
            in_specs=[
                pl.BlockSpec((ROUTE_ROWS, tm), lambda i, off, nu: (0, i)),
                pl.BlockSpec((tm, D), lambda i, off, nu: (i, 0)),
            ],
            out_specs=pl.BlockSpec(memory_space=pl.ANY),
            scratch_shapes=[
                pltpu.VMEM((SLOT_ROWS, tm), jnp.int32),
                pltpu.SMEM((SLOT_ROWS, tm), jnp.int32),
                pltpu.VMEM((EXPERT_TILE, D), F32),
                pltpu.SemaphoreType.DMA(()),
                pltpu.SemaphoreType.DMA(()),
            ],
        ),
        out_shape=jax.ShapeDtypeStruct((n_rows, D), F32),
        compiler_params=_cparams("arbitrary"),
        name="moe_dispatch",
    )(off, n_used, route, f)


def _expert_kernel(te_ref, nu_ref, xs_ref, wg_ref, wu_ref, wd_ref, ys_ref):
    used = pl.program_id(0) < nu_ref[0]

    @pl.when(used)
    def _():
        x = xs_ref[...].astype(BF16)
        hidden = (_silu(_dot(x, wg_ref[0])) * _dot(x, wu_ref[0])).astype(BF16)
        ys_ref[...] = _dot(hidden, wd_ref[0])

    @pl.when(jnp.logical_not(used))
    def _():
        ys_ref[...] = jnp.zeros_like(ys_ref)


def _experts(tile_expert, n_used, xs, wg, wu, wd):
    P, D = xs.shape
    E, _, DE = wg.shape
    T = EXPERT_TILE
    row_map = lambda t, te, nu: (t, 0)
    w_map = lambda t, te, nu: (te[t], 0, 0)
    return pl.pallas_call(
        _expert_kernel,
        grid_spec=pltpu.PrefetchScalarGridSpec(
            num_scalar_prefetch=2,
            grid=(P // T,),
            in_specs=[
                pl.BlockSpec((T, D), row_map),
                pl.BlockSpec((1, D, DE), w_map),
                pl.BlockSpec((1, D, DE), w_map),
                pl.BlockSpec((1, DE, D), w_map),
            ],
            out_specs=pl.BlockSpec((T, D), row_map),
        ),
        out_shape=jax.ShapeDtypeStruct((P, D), F32),
        compiler_params=_cparams("arbitrary"),
        name="moe_experts",
    )(tile_expert, n_used, xs, wg, wu, wd)


def _combine_kernel(off_ref, r_ref, hmid_ref, mod_ref, w1_ref, w2_ref, ys_ref, o_ref, s_vmem, s_smem, y1, y2, sem,
                    rsem, *, tm):
    _load_slots(off_ref, r_ref, s_vmem, s_smem, rsem)
    for t in range(tm):
        for k, buf in ((0, y1), (1, y2)):
            pltpu.make_async_copy(ys_ref.at[pl.ds(s_smem[k, t], 1), :], buf.at[pl.ds(t, 1), :], sem).start()
    pltpu.make_async_copy(y1, y1, sem).wait()
    pltpu.make_async_copy(y2, y2, sem).wait()
    w1 = w1_ref[0]
    w2 = w2_ref[0]
    D = o_ref.shape[-1]
    for c in range(D // ROUTER_LANES):
        sl = slice(c * ROUTER_LANES, (c + 1) * ROUTER_LANES)
        moe = w1 * y1[:, sl] + w2 * y2[:, sl]
        o_ref[0, :, sl] = hmid_ref[0, :, sl] + mod_ref[0, 5:6, sl] * moe


def _combine(off, route, hmid, mod, w1, w2, ys):
    B, L, D = hmid.shape
    tm = min(L, 512)
    nt = L // tm
    tok = lambda b, i, off: (b, i, 0)
    return pl.pallas_call(
        functools.partial(_combine_kernel, tm=tm),
        grid_spec=pltpu.PrefetchScalarGridSpec(
            num_scalar_prefetch=1,
            grid=(B, nt),
            in_specs=[
                pl.BlockSpec((ROUTE_ROWS, tm), lambda b, i, off: (0, b * nt + i)),
                pl.BlockSpec((1, tm, D), tok),
                pl.BlockSpec((1, 6, D), lambda b, i, off: (b, 0, 0)),
                pl.BlockSpec((1, tm, ROUTER_LANES), tok),
                pl.BlockSpec((1, tm, ROUTER_LANES), tok),
                pl.BlockSpec(memory_space=pl.ANY),
            ],
            out_specs=pl.BlockSpec((1, tm, D), tok),
            scratch_shapes=[
                pltpu.VMEM((SLOT_ROWS, tm), jnp.int32),
                pltpu.SMEM((SLOT_ROWS, tm), jnp.int32),
                pltpu.VMEM((tm, D), F32),
                pltpu.VMEM((tm, D), F32),
                pltpu.SemaphoreType.DMA(()),
                pltpu.SemaphoreType.DMA(()),
            ],
        ),
        out_shape=jax.ShapeDtypeStruct((B, L, D), F32),
        compiler_params=_cparams("arbitrary", "arbitrary"),
        name="moe_combine",
    )(off, route, hmid, mod, w1, w2, ys)


def _expert_layout(cnt, n_pairs):
    T = EXPERT_TILE
    counts = cnt[EXPERT_ROW0:, 0].astype(jnp.int32)
    ends = jnp.cumsum((counts + T - 1) // T).astype(jnp.int32)
    off = jnp.concatenate([jnp.zeros((1,), jnp.int32), ends * T])
    n_tiles = n_pairs // T + N_EXPERTS
    t_idx = jnp.arange(n_tiles, dtype=jnp.int32)
    tile_expert = jnp.minimum(jnp.sum(t_idx[:, None] >= ends[None, :], axis=1), N_EXPERTS - 1).astype(jnp.int32)
    n_used = ends[-1]
    tile_expert = jnp.where(t_idx < n_used, tile_expert, jnp.take(tile_expert, n_used - 1))
    return off, tile_expert, n_used.reshape(1), n_tiles * T


def _gate_weights(w_gate, b_gate):
    D = w_gate.shape[0]
    wt = w_gate.T.reshape(4, M_HEADS, D).transpose(1, 0, 2)
    wt = jnp.concatenate([wt, wt[:, 1:2], wt[:, 3:4], jnp.zeros((M_HEADS, GATE_ROWS - 6, D), wt.dtype)], axis=1)
    bt = b_gate.reshape(4, M_HEADS).T
    bt = jnp.concatenate([bt, bt[:, 1:2], bt[:, 3:4], jnp.zeros((M_HEADS, GATE_ROWS - 6), bt.dtype)], axis=1)
    return wt.reshape(M_HEADS * GATE_ROWS, D), bt.reshape(M_HEADS * GATE_ROWS)


def kernel(x, c, ctx, c_ctx, w_ada, b_ada, norm1_w, w_in, b_mgate, q_norm_w, k_norm_w, mh_norm_w, w_branch_m,
           w_branch_a, w_out, norm2_w, w_rg, b_rg, w_re, b_re, w_e_gate, w_e_up, w_e_down):
    B, L, D = x.shape
    depth = w_ada.shape[0]
    assert depth == 1, "context-stream update between layers is not implemented"
    l = 0
    mw = M_HEADS * HEAD_DIM
    aq = A_HEADS * HEAD_DIM
    akv = A_KV_HEADS * HEAD_DIM
    o_mq, o_mk, o_mv, o_og = 0, mw, 2 * mw, 3 * mw
    o_g = 4 * mw
    o_aq = o_g + 4 * M_HEADS
    o_ak = o_aq + aq
    o_av = o_ak + akv
    o_bg = o_av + akv
    scale = HEAD_DIM ** -0.5

    rows = ((B + 1 + 7) // 8) * 8
    cc = jnp.concatenate([c, c_ctx[None, :], jnp.zeros((rows - B - 1, D), F32)], axis=0)
    mod = _ada_mod(cc, w_ada[l], b_ada[l]).reshape(rows, 6, D)

    xn = _norm_mod(x, mod, norm1_w[l], lambda b: b)
    cn = _norm_mod(ctx, mod, norm1_w[l], lambda b: B)

    w = w_in[l]
    wb = lambda a, b: w[:, a:b].astype(BF16)
    cos, sin = _rope_tables(L)
    wgt, bgt = _gate_weights(w[:, o_g:o_aq], b_mgate[l])
    wgt = wgt.astype(BF16)
    wkt = w[:, o_mk:o_mv].T.astype(BF16)

    mq = _proj(xn, wb(o_mq, o_mk), scale=scale, name="proj_mq")
    mkt = _proj_t(xn, wkt, chunk=MLSTM_CHUNK, name="proj_mkt")
    mv = _proj(xn, wb(o_mv, o_og), name="proj_mv")
    og = _proj(xn, wb(o_og, o_g), act="sigmoid", name="proj_og")
    gt = _proj_gates(xn, wgt, bgt, chunk=MLSTM_CHUNK, name="proj_gates")
    partner = _rope_partner(jnp.arange(HEAD_DIM))
    qw, kw = q_norm_w[l] * scale, k_norm_w[l]
    qa = _proj_normrope(xn, wb(o_aq, o_ak), cos * qw, sin * qw[partner], rope=True, name="proj_aq")
    ka = _proj_normrope(xn, wb(o_ak, o_av), cos * kw, sin * kw[partner], rope=True, name="proj_ak")
    va = _proj(xn, wb(o_av, o_bg), name="proj_av")
    bg = _proj(xn, wb(o_bg, o_bg + 2 * D), act="sigmoid", name="proj_bg")

    lctx = ctx.shape[1]
    cmkt = _proj_t(cn, wkt, chunk=MLSTM_CHUNK, name="proj_cmkt")
    cmv = _proj(cn, wb(o_mv, o_og), name="proj_cmv")
    cgt = _proj_gates(cn, wgt, bgt, chunk=MLSTM_CHUNK, name="proj_cgates")
    ckw = jnp.broadcast_to(kw, (lctx, HEAD_DIM))
    cka = _proj_normrope(cn, wb(o_ak, o_av), ckw, ckw, rope=False, name="proj_cak")
    cva = _proj(cn, wb(o_av, o_bg), name="proj_cav")

    ga = _gate_prep(gt, "gate_prep")
    gca = _gate_prep(cgt, "gate_prep_ctx")
    ym = _mlstm(mq, mkt, mv, ga, og, cmkt, cmv, gca, mh_norm_w[l])
    oa = _attention(qa, ka, va, cka, cva)

    pad_w = jnp.zeros((EXPERT_ROW0 - N_GROUPS, D), F32)
    pad_b = jnp.zeros((EXPERT_ROW0 - N_GROUPS,), F32)
    wr = jnp.concatenate([w_rg[l].T, pad_w, w_re[l].T], axis=0)
    br = jnp.concatenate([b_rg[l], pad_b, b_re[l]]).reshape(ROUTER_ROWS, 1)
    hmid, f, route, w1, w2, cnt = _merge(ym, oa, bg, x, mod, w_branch_m[l].astype(BF16), w_branch_a[l].astype(BF16),
                                         w_out[l].astype(BF16), norm2_w[l], wr, br)

    off, tile_expert, n_used, n_rows = _expert_layout(cnt, 2 * B * L)
    xs = _dispatch(off, n_used, route, f.reshape(B * L, D), n_rows)
    ys = _experts(tile_expert, n_used, xs, w_e_gate[l].astype(BF16), w_e_up[l].astype(BF16),
                  w_e_down[l].astype(BF16))
    return _combine(off, route, hmid, mod, w1, w2, ys)
```

```python
import functools

import jax
import jax.numpy as jnp
from jax import lax
from jax.experimental import pallas as pl
from jax.experimental.pallas import tpu as pltpu

F32 = jnp.float32
BF16 = jnp.bfloat16

EPS = 1e-6
GRID_W = 64
ROPE_THETA = 10000.0
HEAD_DIM = 128
M_HEADS = 8
A_HEADS = 8
A_KV_HEADS = 2
A_GROUP = A_HEADS // A_KV_HEADS
N_GROUPS = 4
EXPERTS_PER_GROUP = 8
N_EXPERTS = N_GROUPS * EXPERTS_PER_GROUP
M_INIT = -1e30
MLSTM_CHUNK = 128
GATE_ROWS = 8
ROUTER_LANES = 128
EXPERT_ROW0 = 8
ROUTER_ROWS = EXPERT_ROW0 + N_EXPERTS
EXPERT_TILE = 512
MOE_TOKEN_TILE = 1024
ROUTE_ROWS = 16
SLOT_ROWS = 8
VMEM_LIMIT_BYTES = 48 * 1024 * 1024

NT_DIMS = (((1,), (1,)), ((), ()))


def _cparams(*sem, flags=None):
    return pltpu.CompilerParams(dimension_semantics=sem, vmem_limit_bytes=VMEM_LIMIT_BYTES, flags=flags)


def _dot(a, b):
    return jnp.dot(a, b, preferred_element_type=F32)


def _dot_nt(a, b):
    return lax.dot_general(a, b, NT_DIMS, preferred_element_type=F32)


def _split3(x):
    hi = x.astype(BF16)
    r1 = x - hi.astype(F32)
    mid = r1.astype(BF16)
    lo = (r1 - mid.astype(F32)).astype(BF16)
    return hi, mid, lo


def _silu(x):
    return x * jax.nn.sigmoid(x)


def _ada_kernel(c_ref, w_ref, b_ref, o_ref):
    s = _silu(c_ref[...])
    parts = _split3(s)
    w = w_ref[...]
    w_hi = w.astype(BF16)
    w_lo = (w - w_hi.astype(F32)).astype(BF16)
    acc = _dot(parts[0], w_hi) + _dot(parts[1], w_hi) + _dot(parts[0], w_lo)
    o_ref[...] = acc + b_ref[...]


def _ada_mod(cc, w, b):
    rows, d = cc.shape
    n = w.shape[1]
    tn = min(n, 1536)
    return pl.pallas_call(
        _ada_kernel,
        grid=(n // tn,),
        in_specs=[
            pl.BlockSpec((rows, d), lambda j: (0, 0)),
            pl.BlockSpec((d, tn), lambda j: (0, j)),
            pl.BlockSpec((1, tn), lambda j: (0, j)),
        ],
        out_specs=pl.BlockSpec((rows, tn), lambda j: (0, j)),
        out_shape=jax.ShapeDtypeStruct((rows, n), F32),
        compiler_params=_cparams("parallel"),
        name="ada_mod",
    )(cc, w, b.reshape(1, n))


def _norm_mod_kernel(x_ref, mod_ref, nw_ref, o_ref):
    x = x_ref[0]
    var = jnp.mean(x * x, axis=-1, keepdims=True)
    y = x * lax.rsqrt(var + EPS) * nw_ref[...]
    sh = mod_ref[0, 0:1, :]
    sc = mod_ref[0, 1:2, :]
    o_ref[0] = (y * (1.0 + sc) + sh).astype(o_ref.dtype)


def _norm_mod(x, mod, nw, mod_row):
    B, L, D = x.shape
    tm = min(L, 512)
    return pl.pallas_call(
        _norm_mod_kernel,
        grid=(B, L // tm),
        in_specs=[
            pl.BlockSpec((1, tm, D), lambda b, i: (b, i, 0)),
            pl.BlockSpec((1, 6, D), lambda b, i: (mod_row(b), 0, 0)),
            pl.BlockSpec((1, D), lambda b, i: (0, 0)),
        ],
        out_specs=pl.BlockSpec((1, tm, D), lambda b, i: (b, i, 0)),
        out_shape=jax.ShapeDtypeStruct((B, L, D), BF16),
        compiler_params=_cparams("parallel", "parallel"),
        name="norm_mod",
    )(x, mod, nw.reshape(1, D))


def _proj_kernel(x_ref, w_ref, o_ref, *, act, scale):
    acc = _dot(x_ref[0], w_ref[...])
    if act == "sigmoid":
        acc = jax.nn.sigmoid(acc)
    if scale != 1.0:
        acc = acc * scale
    o_ref[0] = acc.astype(o_ref.dtype)


def _proj(xn, w, *, act="none", scale=1.0, name):
    B, L, D = xn.shape
    N = w.shape[1]
    tm = min(L, 512)
    tn = min(N, 1024)
    return pl.pallas_call(
        functools.partial(_proj_kernel, act=act, scale=scale),
        grid=(B, N // tn, L // tm),
        in_specs=[
            pl.BlockSpec((1, tm, D), lambda b, j, i: (b, i, 0)),
            pl.BlockSpec((D, tn), lambda b, j, i: (0, j)),
        ],
        out_specs=pl.BlockSpec((1, tm, tn), lambda b, j, i: (b, i, j)),
        out_shape=jax.ShapeDtypeStruct((B, L, N), BF16),
        compiler_params=_cparams("parallel", "parallel", "parallel"),
        name=name,
    )(xn, w)


def _proj_t_kernel(w_ref, x_ref, o_ref, *, nchunk, chunk):
    acc = _dot_nt(w_ref[...], x_ref[0])
    for c in range(nchunk):
        o_ref[0, c] = acc[:, c * chunk:(c + 1) * chunk].astype(o_ref.dtype)


def _proj_t(xn, wt, *, chunk, name):
    B, L, D = xn.shape
    R = wt.shape[0]
    tm = min(L, 512)
    nchunk = tm // chunk
    return pl.pallas_call(
        functools.partial(_proj_t_kernel, nchunk=nchunk, chunk=chunk),
        grid=(B, L // tm),
        in_specs=[
            pl.BlockSpec((R, D), lambda b, i: (0, 0)),
            pl.BlockSpec((1, tm, D), lambda b, i: (b, i, 0)),
        ],
        out_specs=pl.BlockSpec((1, nchunk, R, chunk), lambda b, i: (b, i, 0, 0)),
        out_shape=jax.ShapeDtypeStruct((B, L // chunk, R, chunk), BF16),
        compiler_params=_cparams("parallel", "parallel"),
        name=name,
    )(wt, xn)


def _proj_gates_kernel(w_ref, x_ref, b_ref, o_ref, *, nchunk, chunk):
    acc = _dot_nt(w_ref[...], x_ref[0]) + b_ref[...]
    for h in range(M_HEADS):
        for c in range(nchunk):
            o_ref[0, h, c] = acc[h * GATE_ROWS:(h + 1) * GATE_ROWS, c * chunk:(c + 1) * chunk]


def _proj_gates(xn, wt, bias, *, chunk, name):
    B, L, D = xn.shape
    R = wt.shape[0]
    tm = min(L, 512)
    nchunk = tm // chunk
    return pl.pallas_call(
        functools.partial(_proj_gates_kernel, nchunk=nchunk, chunk=chunk),
        grid=(B, L // tm),
        in_specs=[
            pl.BlockSpec((R, D), lambda b, i: (0, 0)),
            pl.BlockSpec((1, tm, D), lambda b, i: (b, i, 0)),
            pl.BlockSpec((R, 1), lambda b, i: (0, 0)),
        ],
        out_specs=pl.BlockSpec((1, M_HEADS, nchunk, GATE_ROWS, chunk), lambda b, i: (b, 0, i, 0, 0)),
        out_shape=jax.ShapeDtypeStruct((B, M_HEADS, L // chunk, GATE_ROWS, chunk), F32),
        compiler_params=_cparams("parallel", "parallel"),
        name=name,
    )(wt, xn, bias.reshape(R, 1))


def _rope_partner(j):
    quarter = HEAD_DIM // 4
    return jnp.where((j % (2 * quarter)) < quarter, j + quarter, j - quarter)


def _split2_lanes(x):
    hi = x.astype(BF16)
    return jnp.concatenate([hi, (x - hi.astype(F32)).astype(BF16)], axis=1)


def _proj_normrope_kernel(x_ref, w_ref, cw_ref, sw_ref, o_ref, *, nheads, rope):
    acc = _dot(x_ref[0], w_ref[...])
    cw = cw_ref[...]
    avg = jnp.full((2 * HEAD_DIM, HEAD_DIM), 1.0 / HEAD_DIM, BF16)
    if rope:
        sw = sw_ref[...]
        src = lax.broadcasted_iota(jnp.int32, (2 * HEAD_DIM, HEAD_DIM), 0) % HEAD_DIM
        dst = lax.broadcasted_iota(jnp.int32, (2 * HEAD_DIM, HEAD_DIM), 1)
        perm = jnp.where(src == _rope_partner(dst), 1.0, 0.0).astype(BF16)
    for h in range(nheads):
        a = acc[:, h * HEAD_DIM:(h + 1) * HEAD_DIM]
        var = _dot(_split2_lanes(a * a), avg)
        y = a * cw
        if rope:
            y = y + _dot(_split2_lanes(a), perm) * sw
        o_ref[0, :, h * HEAD_DIM:(h + 1) * HEAD_DIM] = (y * lax.rsqrt(var + EPS)).astype(o_ref.dtype)


def _proj_normrope(xn, w, cw, sw, *, rope, name):
    B, L, D = xn.shape
    N = w.shape[1]
    nheads = N // HEAD_DIM
    tm = min(L, 512)
    return pl.pallas_call(
        functools.partial(_proj_normrope_kernel, nheads=nheads, rope=rope),
        grid=(B, L // tm),
        in_specs=[
            pl.BlockSpec((1, tm, D), lambda b, i: (b, i, 0)),
            pl.BlockSpec((D, N), lambda b, i: (0, 0)),
            pl.BlockSpec((tm, HEAD_DIM), lambda b, i: (i, 0)),
            pl.BlockSpec((tm, HEAD_DIM), lambda b, i: (i, 0)),
        ],
        out_specs=pl.BlockSpec((1, tm, N), lambda b, i: (b, i, 0)),
        out_shape=jax.ShapeDtypeStruct((B, L, N), BF16),
        compiler_params=_cparams("parallel", "parallel"),
        name=name,
    )(xn, w, cw, sw)


def _fused_proj_kernel(*refs, plan):
    x_ref, mod_ref, nw_ref = refs[:3]
    n_inputs = sum(n for _, n, _ in plan)
    in_refs = refs[3:3 + n_inputs]
    out_refs = refs[3 + n_inputs:]
    x = x_ref[0]
    var = jnp.mean(x * x, axis=-1, keepdims=True)
    y = x * lax.rsqrt(var + EPS) * nw_ref[...]
    xn = (y * (1.0 + mod_ref[0, 1:2, :]) + mod_ref[0, 0:1, :]).astype(BF16)
    tm = xn.shape[0]
    C = MLSTM_CHUNK
    pos = 0
    for (kind, n, prm), o_ref in zip(plan, out_refs):
        ins = in_refs[pos:pos + n]
        pos += n
        if kind == "plain":
            acc = _dot(xn, ins[0][...])
            if prm["act"] == "sigmoid":
                acc = jax.nn.sigmoid(acc)
            if prm["scale"] != 1.0:
                acc = acc * prm["scale"]
            o_ref[0] = acc.astype(o_ref.dtype)
        elif kind == "kt":
            acc = _dot_nt(ins[0][...], xn)
            for c in range(tm // C):
                o_ref[0, c] = acc[:, c * C:(c + 1) * C].astype(o_ref.dtype)
        elif kind == "gates":
            acc = _dot_nt(ins[0][...], xn) + ins[1][...]
            for h in range(M_HEADS):
                for c in range(tm // C):
                    o_ref[0, h, c] = acc[h * GATE_ROWS:(h + 1) * GATE_ROWS, c * C:(c + 1) * C]
        else:
            acc = _dot(xn, ins[0][...])
            cw = ins[1][...]
            avg = jnp.full((2 * HEAD_DIM, HEAD_DIM), 1.0 / HEAD_DIM, BF16)
            if prm["rope"]:
                sw = ins[2][...]
                src = lax.broadcasted_iota(jnp.int32, (2 * HEAD_DIM, HEAD_DIM), 0) % HEAD_DIM
                dst = lax.broadcasted_iota(jnp.int32, (2 * HEAD_DIM, HEAD_DIM), 1)
                perm = jnp.where(src == _rope_partner(dst), 1.0, 0.0).astype(BF16)
            for h in range(acc.shape[1] // HEAD_DIM):
                a = acc[:, h * HEAD_DIM:(h + 1) * HEAD_DIM]
                hvar = _dot(_split2_lanes(a * a), avg)
                yh = a * cw
                if prm["rope"]:
                    yh = yh + _dot(_split2_lanes(a), perm) * sw
                o_ref[0, :, h * HEAD_DIM:(h + 1) * HEAD_DIM] = (yh * lax.rsqrt(hvar + EPS)).astype(o_ref.dtype)


def _fused_proj(x, mod, nw, mod_row, items, name):
    B, L, D = x.shape
    tm = min(L, 512)
    C = MLSTM_CHUNK
    full2 = lambda b, i: (0, 0)
    in_specs = [
        pl.BlockSpec((1, tm, D), lambda b, i: (b, i, 0)),
        pl.BlockSpec((1, 6, D), lambda b, i: (mod_row(b), 0, 0)),
        pl.BlockSpec((1, D), full2),
    ]
    args = [x, mod, nw.reshape(1, D)]
    out_specs, out_shape, plan = [], [], []
    for kind, arrays, prm in items:
        plan.append((kind, len(arrays), prm))
        args.extend(arrays)
        w = arrays[0]
        if kind in ("plain", "normrope"):
            N = w.shape[1]
            in_specs.append(pl.BlockSpec((D, N), full2))
            for t in arrays[1:]:
                in_specs.append(pl.BlockSpec((tm, HEAD_DIM), lambda b, i: (i, 0)))
            out_specs.append(pl.BlockSpec((1, tm, N), lambda b, i: (b, i, 0)))
            out_shape.append(jax.ShapeDtypeStruct((B, L, N), BF16))
        elif kind == "kt":
            R = w.shape[0]
            in_specs.append(pl.BlockSpec((R, D), full2))
            out_specs.append(pl.BlockSpec((1, tm // C, R, C), lambda b, i: (b, i, 0, 0)))
            out_shape.append(jax.ShapeDtypeStruct((B, L // C, R, C), BF16))
        else:
            R = w.shape[0]
            in_specs.append(pl.BlockSpec((R, D), full2))
            in_specs.append(pl.BlockSpec((R, 1), full2))
            out_specs.append(pl.BlockSpec((1, M_HEADS, tm // C, GATE_ROWS, C), lambda b, i: (b, 0, i, 0, 0)))
            out_shape.append(jax.ShapeDtypeStruct((B, M_HEADS, L // C, GATE_ROWS, C), F32))
    return pl.pallas_call(
        functools.partial(_fused_proj_kernel, plan=tuple(plan)),
        grid=(B, L // tm),
        in_specs=in_specs,
        out_specs=out_specs,
        out_shape=out_shape,
        compiler_params=_cparams("parallel", "parallel"),
        name=name,
    )(*args)


def _rope_tables(L):
    rows = L // GRID_W
    row = jnp.repeat(jnp.arange(rows), GRID_W).astype(F32)
    col = jnp.tile(jnp.arange(GRID_W), rows).astype(F32)
    half = HEAD_DIM // 2
    inv = ROPE_THETA ** (-jnp.arange(0, half, 2, dtype=F32) / half)
    ang_r = row[:, None] * inv[None, :]
    ang_c = col[:, None] * inv[None, :]
    cr, sr, cc, sc = jnp.cos(ang_r), jnp.sin(ang_r), jnp.cos(ang_c), jnp.sin(ang_c)
    return jnp.concatenate([cr, cr, cc, cc], axis=-1), jnp.concatenate([-sr, sr, -sc, sc], axis=-1)


def _log_sigmoid(x):
    return jnp.minimum(x, 0.0) - jnp.log1p(jnp.exp(-jnp.abs(x)))


def _tri_masks(n):
    row_i = lax.broadcasted_iota(jnp.int32, (n, n), 0)
    col_i = lax.broadcasted_iota(jnp.int32, (n, n), 1)
    return row_i <= col_i, row_i >= col_i


def _gate_prep_kernel(g_ref, a_ref):
    _, H, nc, R8, C = g_ref.shape
    rows = H * nc * R8
    g = g_ref[0].reshape(rows, C)
    le, ge = _tri_masks(C)
    upper = jnp.where(le, 1.0, 0.0).astype(BF16)
    lower = jnp.where(ge, 1.0, 0.0).astype(BF16)
    lf = _log_sigmoid(g)
    parts = _split3(lf)
    pre = _dot(parts[0], upper) + _dot(parts[1], upper) + _dot(parts[2], upper)
    suf = _dot(parts[0], lower) + _dot(parts[1], lower) + _dot(parts[2], lower)
    rtype = lax.broadcasted_iota(jnp.int32, (rows, C), 0) % R8
    a = jnp.where(rtype == 1, pre, jnp.where(rtype == 3, suf, jnp.where(rtype >= 4, lf, g)))
    a_ref[0] = a.reshape(H, nc, R8, C)


def _gate_prep(g, name):
    B, H, nc, R8, C = g.shape
    blk = pl.BlockSpec((1, H, nc, R8, C), lambda b: (b, 0, 0, 0, 0))
    return pl.pallas_call(
        _gate_prep_kernel,
        grid=(B,),
        in_specs=[blk],
        out_specs=blk,
        out_shape=jax.ShapeDtypeStruct(g.shape, F32),
        compiler_params=_cparams("parallel"),
        name=name,
    )(g)


def _mlstm_kernel(q_ref, kt_ref, v_ref, a_ref, og_ref, kct_ref, vc_ref, ac_ref, nw_ref, o_ref, cin_ref,
                  *, nc, nctx):
    C = MLSTM_CHUNK
    D = HEAD_DIM
    le, ge = _tri_masks(C)
    le_bf = jnp.where(le, 1.0, 0.0).astype(BF16)
    ge_bf = jnp.where(ge, 1.0, 0.0).astype(BF16)
    ones_blk = jnp.ones((C, D), BF16)

    def gates(ref, j, fwd):
        ir, fr = (0, 1) if fwd else (2, 3)
        b_row = ref[0, 0, j, fr:fr + 1, :]
        a_row = ref[0, 0, j, ir:ir + 1, :] - b_row
        return a_row, (b_row[:, C - 1:C] if fwd else b_row[:, 0:1])

    def vaug_of(ref, j):
        return jnp.concatenate([ref[0, j * C:(j + 1) * C, :], ones_blk], axis=1)

    def update(cst, m_prev, kt, vaug, a_row, btot):
        mc = jnp.maximum(jnp.max(a_row, axis=1, keepdims=True), m_prev)
        w_in = jnp.exp(a_row - mc)
        ktw = (kt.astype(F32) * w_in).astype(BF16)
        return jnp.exp(m_prev - mc) * cst + _dot(ktw, vaug), btot + mc

    m_in = {}
    for d, fwd in ((0, True), (1, False)):
        cst = jnp.zeros((D, 2 * D), F32)
        m = jnp.full((1, 1), M_INIT, F32)
        for j in (range(nctx) if fwd else reversed(range(nctx))):
            a_row, btot = gates(ac_ref, j, fwd)
            cst, m = update(cst, m, kct_ref[0, j], vaug_of(vc_ref, j), a_row, btot)
        order = list(range(nc) if fwd else reversed(range(nc)))
        for idx, j in enumerate(order):
            cin_ref[d, j] = cst.astype(BF16)
            m_in[d, j] = m
            if idx + 1 < nc:
                a_row, btot = gates(a_ref, j, fwd)
                cst, m = update(cst, m, kt_ref[0, j], vaug_of(v_ref, j), a_row, btot)

    def out_step(d, j, fwd, q, kt, vaug):
        a_row, _ = gates(a_ref, j, fwd)
        m_prev = m_in[d, j]
        nlf = -a_ref[0, 0, j, (4 if fwd else 5):(5 if fwd else 6), :]
        nlf_hi = nlf.astype(BF16)
        nlf_lo = (nlf - nlf_hi.astype(F32)).astype(BF16)
        vis = ge_bf if fwd else le_bf
        nb = _dot_nt(vis, jnp.broadcast_to(nlf_hi, (D, C))) + _dot_nt(vis, jnp.broadcast_to(nlf_lo, (D, C)))
        a_mat = jnp.where(ge if fwd else le, a_row, -jnp.inf)
        m_q = jnp.maximum(jnp.max(a_mat, axis=1, keepdims=True), m_prev)
        bm = jnp.broadcast_to(m_q, (C, D))
        w_intra = jnp.exp(a_mat - bm)
        p = (_dot(q, kt) * w_intra).astype(BF16)
        qc = _dot(q, cin_ref[d, j])
        pv = _dot(p, vaug)
        w_inter = jnp.exp(m_prev - bm)
        num = w_inter * qc[:, :D] + pv[:, :D]
        den = w_inter * qc[:, D:] + pv[:, D:]
        return num / jnp.maximum(jnp.abs(den), jnp.exp(nb - bm))

    nw = nw_ref[...]
    for j in range(nc):
        sl = slice(j * C, (j + 1) * C)
        q = q_ref[0, sl, :]
        kt = kt_ref[0, j]
        vaug = vaug_of(v_ref, j)
        h = out_step(0, j, True, q, kt, vaug) + out_step(1, j, False, q, kt, vaug)
        var = jnp.mean(h * h, axis=-1, keepdims=True)
        y = h * lax.rsqrt(var + EPS) * nw
        o_ref[0, sl, :] = (og_ref[0, sl, :].astype(F32) * y).astype(o_ref.dtype)


def _mlstm(q, kt, v, a, og, kct, vc, ac, mh_norm_w):
    B, L, W = q.shape
    C = MLSTM_CHUNK
    assert C == HEAD_DIM
    nc = L // C
    lctx = vc.shape[1]
    nctx = lctx // C
    H = W // HEAD_DIM
    return pl.pallas_call(
        functools.partial(_mlstm_kernel, nc=nc, nctx=nctx),
        grid=(B, H),
        in_specs=[
            pl.BlockSpec((1, L, HEAD_DIM), lambda b, h: (b, 0, h)),
            pl.BlockSpec((1, nc, HEAD_DIM, C), lambda b, h: (b, 0, h, 0)),
            pl.BlockSpec((1, L, HEAD_DIM), lambda b, h: (b, 0, h)),
            pl.BlockSpec((1, 1, nc, GATE_ROWS, C), lambda b, h: (b, h, 0, 0, 0)),
            pl.BlockSpec((1, L, HEAD_DIM), lambda b, h: (b, 0, h)),
            pl.BlockSpec((1, nctx, HEAD_DIM, C), lambda b, h: (b, 0, h, 0)),
            pl.BlockSpec((1, lctx, HEAD_DIM), lambda b, h: (b, 0, h)),
            pl.BlockSpec((1, 1, nctx, GATE_ROWS, C), lambda b, h: (b, h, 0, 0, 0)),
            pl.BlockSpec((1, HEAD_DIM), lambda b, h: (0, h)),
        ],
        out_specs=pl.BlockSpec((1, L, HEAD_DIM), lambda b, h: (b, 0, h)),
        out_shape=jax.ShapeDtypeStruct((B, L, W), BF16),
        scratch_shapes=[pltpu.VMEM((2, nc, HEAD_DIM, 2 * HEAD_DIM), BF16)],
        compiler_params=_cparams("parallel", "parallel"),
        name="mlstm",
    )(q, kt, v, a, og, kct, vc, ac, mh_norm_w.reshape(1, W))


def _attn_kernel(q_ref, k_ref, v_ref, kc_ref, vc_ref, o_ref):
    k = k_ref[0]
    kc = kc_ref[0]
    v = jnp.concatenate([v_ref[0], jnp.ones(v_ref.shape[1:], BF16)], axis=1)
    vc = jnp.concatenate([vc_ref[0], jnp.ones(vc_ref.shape[1:], BF16)], axis=1)
    for h in range(A_GROUP):
        q = q_ref[0, :, h * HEAD_DIM:(h + 1) * HEAD_DIM]
        s1 = _dot_nt(q, k)
        s2 = _dot_nt(q, kc)
        m = jnp.maximum(jnp.max(s1, axis=1, keepdims=True), jnp.max(s2, axis=1, keepdims=True))
        ol = _dot(jnp.exp(s1 - m).astype(BF16), v) + _dot(jnp.exp(s2 - m).astype(BF16), vc)
        o_ref[0, :, h * HEAD_DIM:(h + 1) * HEAD_DIM] = (ol[:, :HEAD_DIM] / ol[:, HEAD_DIM:]).astype(o_ref.dtype)


def _attention(q, k, v, kc, vc):
    B, L, W = q.shape
    lctx = kc.shape[1]
    tq = min(L, 256)
    gw = A_GROUP * HEAD_DIM
    return pl.pallas_call(
        _attn_kernel,
        grid=(B, A_KV_HEADS, L // tq),
        in_specs=[
            pl.BlockSpec((1, tq, gw), lambda b, g, i: (b, i, g)),
            pl.BlockSpec((1, L, HEAD_DIM), lambda b, g, i: (b, 0, g)),
            pl.BlockSpec((1, L, HEAD_DIM), lambda b, g, i: (b, 0, g)),
            pl.BlockSpec((1, lctx, HEAD_DIM), lambda b, g, i: (b, 0, g)),
            pl.BlockSpec((1, lctx, HEAD_DIM), lambda b, g, i: (b, 0, g)),
        ],
        out_specs=pl.BlockSpec((1, tq, gw), lambda b, g, i: (b, i, g)),
        out_shape=jax.ShapeDtypeStruct((B, L, W), BF16),
        compiler_params=_cparams("parallel", "parallel", "parallel"),
        name="attention",
    )(q, k, v, kc, vc)


def _merge_kernel(ym_ref, oa_ref, bg_ref, x_ref, mod_ref, wbm_ref, wba_ref, wo_ref, n2_ref, wr_ref, br_ref,
                  hmid_ref, f_ref, route_ref, w1_ref, w2_ref, cnt_ref, run_ref):
    D = x_ref.shape[-1]
    tm = x_ref.shape[1]

    @pl.when((pl.program_id(0) == 0) & (pl.program_id(1) == 0))
    def _():
        run_ref[...] = jnp.zeros_like(run_ref)

    ym = _dot(ym_ref[0], wbm_ref[...])
    ya = _dot(oa_ref[0], wba_ref[...])
    g0 = bg_ref[0, :, :D].astype(F32)
    g1 = bg_ref[0, :, D:].astype(F32)
    mix = _dot((g0 * ym + g1 * ya).astype(BF16), wo_ref[...])
    hmid = x_ref[0] + mod_ref[0, 2:3, :] * mix
    hmid_ref[0] = hmid
    var = jnp.mean(hmid * hmid, axis=-1, keepdims=True)
    y = hmid * lax.rsqrt(var + EPS) * n2_ref[...]
    f = y * (1.0 + mod_ref[0, 4:5, :]) + mod_ref[0, 3:4, :]
    f_ref[0] = f.astype(f_ref.dtype)

    fp = _split3(f)
    wr = wr_ref[...]
    wr_hi = wr.astype(BF16)
    wr_lo = (wr - wr_hi.astype(F32)).astype(BF16)
    logits = (_dot_nt(wr_hi, fp[0]) + _dot_nt(wr_hi, fp[1]) + _dot_nt(wr_lo, fp[0]) + _dot_nt(wr_hi, fp[2])
              + _dot_nt(wr_lo, fp[1])) + br_ref[...]
    row = lax.broadcasted_iota(jnp.int32, logits.shape, 0)
    big = jnp.int32(ROUTER_ROWS)
    neg = -jnp.inf
    is_grp = row < N_GROUPS
    gl = jnp.where(is_grp, logits, neg)
    gmax = jnp.max(gl, axis=0, keepdims=True)
    gsel = jnp.min(jnp.where(gl == gmax, row, big), axis=0, keepdims=True)
    p_grp = 1.0 / jnp.sum(jnp.where(is_grp, jnp.exp(logits - gmax), 0.0), axis=0, keepdims=True)
    lo = EXPERT_ROW0 + gsel * EXPERTS_PER_GROUP
    in_grp = (row >= lo) & (row < lo + EXPERTS_PER_GROUP)
    el = jnp.where(in_grp, logits, neg)
    v1 = jnp.max(el, axis=0, keepdims=True)
    i1 = jnp.min(jnp.where(el == v1, row, big), axis=0, keepdims=True)
    el2 = jnp.where(row == i1, neg, el)
    v2 = jnp.max(el2, axis=0, keepdims=True)
    i2 = jnp.min(jnp.where(el2 == v2, row, big), axis=0, keepdims=True)
    e21 = jnp.exp(v2 - v1)
    w1 = p_grp / (1.0 + e21)
    w2 = p_grp * e21 / (1.0 + e21)

    sel = jnp.where((row == i1) | (row == i2), 1.0, 0.0)
    tok_u = lax.broadcasted_iota(jnp.int32, (tm, tm), 0)
    tok_n = lax.broadcasted_iota(jnp.int32, (tm, tm), 1)
    earlier = jnp.where(tok_u < tok_n, 1.0, 0.0).astype(BF16)
    rank_all = run_ref[...] + _dot(sel.astype(BF16), earlier)
    rank1 = jnp.sum(jnp.where(row == i1, rank_all, 0.0), axis=0, keepdims=True)
    rank2 = jnp.sum(jnp.where(row == i2, rank_all, 0.0), axis=0, keepdims=True)
    run_ref[...] += jnp.sum(sel, axis=1, keepdims=True)
    cnt_ref[...] = jnp.broadcast_to(run_ref[...], cnt_ref.shape)
    route_ref[...] = jnp.zeros_like(route_ref)
    route_ref[0:1, :] = i1 - EXPERT_ROW0
    route_ref[1:2, :] = i2 - EXPERT_ROW0
    route_ref[8:9, :] = rank1.astype(jnp.int32)
    route_ref[9:10, :] = rank2.astype(jnp.int32)

    eye = jnp.where(tok_u == tok_n, 1.0, 0.0).astype(BF16)
    for w_row, w_ref in ((w1, w1_ref), (w2, w2_ref)):
        parts = _split3(jnp.broadcast_to(w_row, (ROUTER_LANES, tm)))
        w_ref[0] = _dot_nt(eye, parts[0]) + _dot_nt(eye, parts[1]) + _dot_nt(eye, parts[2])


def _merge(ym, oa, bg, x, mod, wbm, wba, wo, n2, wr, br):
    B, L, D = x.shape
    tm = min(L, 512)
    nt = L // tm
    full = lambda b, i: (0, 0)
    tok = lambda b, i: (b, i, 0)
    return pl.pallas_call(
        _merge_kernel,
        grid=(B, L // tm),
        in_specs=[
            pl.BlockSpec((1, tm, D), tok),
            pl.BlockSpec((1, tm, D), tok),
            pl.BlockSpec((1, tm, 2 * D), tok),
            pl.BlockSpec((1, tm, D), tok),
            pl.BlockSpec((1, 6, D), lambda b, i: (b, 0, 0)),
            pl.BlockSpec((D, D), full),
            pl.BlockSpec((D, D), full),
            pl.BlockSpec((D, D), full),
            pl.BlockSpec((1, D), full),
            pl.BlockSpec((ROUTER_ROWS, D), full),
            pl.BlockSpec((ROUTER_ROWS, 1), full),
        ],
        out_specs=[
            pl.BlockSpec((1, tm, D), tok),
            pl.BlockSpec((1, tm, D), tok),
            pl.BlockSpec((ROUTE_ROWS, tm), lambda b, i: (0, b * nt + i)),
            pl.BlockSpec((1, tm, ROUTER_LANES), tok),
            pl.BlockSpec((1, tm, ROUTER_LANES), tok),
            pl.BlockSpec((ROUTER_ROWS, ROUTER_LANES), full),
        ],
        out_shape=[
            jax.ShapeDtypeStruct((B, L, D), F32),
            jax.ShapeDtypeStruct((B, L, D), F32),
            jax.ShapeDtypeStruct((ROUTE_ROWS, B * L), jnp.int32),
            jax.ShapeDtypeStruct((B, L, ROUTER_LANES), F32),
            jax.ShapeDtypeStruct((B, L, ROUTER_LANES), F32),
            jax.ShapeDtypeStruct((ROUTER_ROWS, ROUTER_LANES), F32),
        ],
        scratch_shapes=[pltpu.VMEM((ROUTER_ROWS, 1), F32)],
        compiler_params=_cparams("arbitrary", "arbitrary"),
        name="merge_router",
    )(ym, oa, bg, x, mod, wbm, wba, wo, n2.reshape(1, D), wr, br)


def _load_slots(off_ref, r_ref, s_vmem, s_smem, rsem):
    expert = r_ref[0:SLOT_ROWS, :]
    slot = r_ref[SLOT_ROWS:2 * SLOT_ROWS, :]
    for e in range(1, N_EXPERTS):
        slot = slot + jnp.where(expert == e, off_ref[e], 0)
    s_vmem[...] = slot
    cp = pltpu.make_async_copy(s_vmem, s_smem, rsem)
    cp.start()
    cp.wait()


def _dispatch_kernel(off_ref, nu_ref, r_ref, f_ref, xs_ref, s_vmem, s_smem, zbuf, sem, rsem, *, tm):
    T = EXPERT_TILE

    @pl.when(pl.program_id(0) == 0)
    def _():
        zbuf[...] = jnp.zeros_like(zbuf)

        def zero_tile(t, carry):
            cp = pltpu.make_async_copy(zbuf, xs_ref.at[pl.ds(pl.multiple_of(t * T, T), T), :], sem)
            cp.start()
            cp.wait()
            return carry

        lax.fori_loop(nu_ref[0], xs_ref.shape[0] // T, zero_tile, 0)
        for e in range(N_EXPERTS):
            start = off_ref[e]
            end = off_ref[e + 1]

            @pl.when(end > start)
            def _():
                cp = pltpu.make_async_copy(zbuf, xs_ref.at[pl.ds(pl.multiple_of(end - T, T), T), :], sem)
                cp.start()
                cp.wait()

    _load_slots(off_ref, r_ref, s_vmem, s_smem, rsem)
    for t in range(tm):
        for k in range(2):
            pltpu.make_async_copy(f_ref.at[pl.ds(t, 1), :], xs_ref.at[pl.ds(s_smem[k, t], 1), :], sem).start(priority=k)
    for _ in range(2):
        pltpu.make_async_copy(f_ref, f_ref, sem).wait()


def _dispatch(off, n_used, route, f, n_rows):
    N, D = f.shape
    tm = min(N, MOE_TOKEN_TILE)
    return pl.pallas_call(
        functools.partial(_dispatch_kernel, tm=tm),
        grid_spec=pltpu.PrefetchScalarGridSpec(
            num_scalar_prefetch=2,
            grid=(N // tm,),
            in_specs=[
                pl.BlockSpec((ROUTE_ROWS, tm), lambda i, off, nu: (0, i)),
                pl.BlockSpec((tm, D), lambda i, off, nu: (i, 0)),
            ],
            out_specs=pl.BlockSpec(memory_space=pl.ANY),
            scratch_shapes=[
                pltpu.VMEM((SLOT_ROWS, tm), jnp.int32),
                pltpu.SMEM((SLOT_ROWS, tm), jnp.int32),
                pltpu.VMEM((EXPERT_TILE, D), F32),
                pltpu.SemaphoreType.DMA(()),
                pltpu.SemaphoreType.DMA(()),
            ],
        ),
        out_shape=jax.ShapeDtypeStruct((n_rows, D), F32),
        compiler_params=_cparams("arbitrary"),
        name="moe_dispatch",
    )(off, n_used, route, f)


def _expert_kernel(te_ref, nu_ref, xs_ref, wg_ref, wu_ref, wd_ref, ys_ref):
    used = pl.program_id(0) < nu_ref[0]

    @pl.when(used)
    def _():
        x = xs_ref[...].astype(BF16)
        hidden = (_silu(_dot(x, wg_ref[0])) * _dot(x, wu_ref[0])).astype(BF16)
        ys_ref[...] = _dot(hidden, wd_ref[0])

    @pl.when(jnp.logical_not(used))
    def _():
        ys_ref[...] = jnp.zeros_like(ys_ref)


def _experts(tile_expert, n_used, xs, wg, wu, wd):
    P, D = xs.shape
    E, _, DE = wg.shape
    T = EXPERT_TILE
    row_map = lambda t, te, nu: (t, 0)
    w_map = lambda t, te, nu: (te[t], 0, 0)
    return pl.pallas_call(
        _expert_kernel,
        grid_spec=pltpu.PrefetchScalarGridSpec(
            num_scalar_prefetch=2,
            grid=(P // T,),
            in_specs=[
                pl.BlockSpec((T, D), row_map),
                pl.BlockSpec((1, D, DE), w_map),
                pl.BlockSpec((1, D, DE), w_map),
                pl.BlockSpec((1, DE, D), w_map),
            ],
            out_specs=pl.BlockSpec((T, D), row_map),
        ),
        out_shape=jax.ShapeDtypeStruct((P, D), F32),
        compiler_params=_cparams("arbitrary"),
        name="moe_experts",
    )(tile_expert, n_used, xs, wg, wu, wd)


def _combine_kernel(off_ref, r_ref, hmid_ref, mod_ref, w1_ref, w2_ref, ys_ref, o_ref, s_vmem, s_smem, y1, y2, sem,
                    rsem, *, tm):
    _load_slots(off_ref, r_ref, s_vmem, s_smem, rsem)
    for t in range(tm):
        for k, buf in ((0, y1), (1, y2)):
            pltpu.make_async_copy(ys_ref.at[pl.ds(s_smem[k, t], 1), :], buf.at[pl.ds(t, 1), :], sem).start(priority=k)
    pltpu.make_async_copy(y1, y1, sem).wait()
    pltpu.make_async_copy(y2, y2, sem).wait()
    w1 = w1_ref[0]
    w2 = w2_ref[0]
    D = o_ref.shape[-1]
    for c in range(D // ROUTER_LANES):
        sl = slice(c * ROUTER_LANES, (c + 1) * ROUTER_LANES)
        moe = w1 * y1[:, sl] + w2 * y2[:, sl]
        o_ref[0, :, sl] = hmid_ref[0, :, sl] + mod_ref[0, 5:6, sl] * moe


def _combine(off, route, hmid, mod, w1, w2, ys):
    B, L, D = hmid.shape
    tm = min(L, MOE_TOKEN_TILE)
    nt = L // tm
    tok = lambda b, i, off: (b, i, 0)
    return pl.pallas_call(
        functools.partial(_combine_kernel, tm=tm),
        grid_spec=pltpu.PrefetchScalarGridSpec(
            num_scalar_prefetch=1,
            grid=(B, nt),
            in_specs=[
                pl.BlockSpec((ROUTE_ROWS, tm), lambda b, i, off: (0, b * nt + i)),
                pl.BlockSpec((1, tm, D), tok),
                pl.BlockSpec((1, 6, D), lambda b, i, off: (b, 0, 0)),
                pl.BlockSpec((1, tm, ROUTER_LANES), tok),
                pl.BlockSpec((1, tm, ROUTER_LANES), tok),
                pl.BlockSpec(memory_space=pl.ANY),
            ],
            out_specs=pl.BlockSpec((1, tm, D), tok),
            scratch_shapes=[
                pltpu.VMEM((SLOT_ROWS, tm), jnp.int32),
                pltpu.SMEM((SLOT_ROWS, tm), jnp.int32),
                pltpu.VMEM((tm, D), F32),
                pltpu.VMEM((tm, D), F32),
                pltpu.SemaphoreType.DMA(()),
                pltpu.SemaphoreType.DMA(()),
            ],
        ),
        out_shape=jax.ShapeDtypeStruct((B, L, D), F32),
        compiler_params=_cparams("arbitrary", "arbitrary"),
        name="moe_combine",
    )(off, route, hmid, mod, w1, w2, ys)


def _expert_layout(cnt, n_pairs):
    T = EXPERT_TILE
    counts = cnt[EXPERT_ROW0:, 0].astype(jnp.int32)
    ends = jnp.cumsum((counts + T - 1) // T).astype(jnp.int32)
    off = jnp.concatenate([jnp.zeros((1,), jnp.int32), ends * T])
    n_tiles = n_pairs // T + N_EXPERTS
    t_idx = jnp.arange(n_tiles, dtype=jnp.int32)
    tile_expert = jnp.minimum(jnp.sum(t_idx[:, None] >= ends[None, :], axis=1), N_EXPERTS - 1).astype(jnp.int32)
    n_used = ends[-1]
    tile_expert = jnp.where(t_idx < n_used, tile_expert, jnp.take(tile_expert, n_used - 1))
    return off, tile_expert, n_used.reshape(1), n_tiles * T


def _gate_weights(w_gate, b_gate):
    D = w_gate.shape[0]
    wt = w_gate.T.reshape(4, M_HEADS, D).transpose(1, 0, 2)
    wt = jnp.concatenate([wt, wt[:, 1:2], wt[:, 3:4], jnp.zeros((M_HEADS, GATE_ROWS - 6, D), wt.dtype)], axis=1)
    bt = b_gate.reshape(4, M_HEADS).T
    bt = jnp.concatenate([bt, bt[:, 1:2], bt[:, 3:4], jnp.zeros((M_HEADS, GATE_ROWS - 6), bt.dtype)], axis=1)
    return wt.reshape(M_HEADS * GATE_ROWS, D), bt.reshape(M_HEADS * GATE_ROWS)


def kernel(x, c, ctx, c_ctx, w_ada, b_ada, norm1_w, w_in, b_mgate, q_norm_w, k_norm_w, mh_norm_w, w_branch_m,
           w_branch_a, w_out, norm2_w, w_rg, b_rg, w_re, b_re, w_e_gate, w_e_up, w_e_down):
    B, L, D = x.shape
    depth = w_ada.shape[0]
    assert depth == 1, "context-stream update between layers is not implemented"
    l = 0
    mw = M_HEADS * HEAD_DIM
    aq = A_HEADS * HEAD_DIM
    akv = A_KV_HEADS * HEAD_DIM
    o_mq, o_mk, o_mv, o_og = 0, mw, 2 * mw, 3 * mw
    o_g = 4 * mw
    o_aq = o_g + 4 * M_HEADS
    o_ak = o_aq + aq
    o_av = o_ak + akv
    o_bg = o_av + akv
    scale = HEAD_DIM ** -0.5

    rows = ((B + 1 + 7) // 8) * 8
    cc = jnp.concatenate([c, c_ctx[None, :], jnp.zeros((rows - B - 1, D), F32)], axis=0)
    mod = _ada_mod(cc, w_ada[l], b_ada[l]).reshape(rows, 6, D)

    w = w_in[l]
    wb = lambda a, b: w[:, a:b].astype(BF16)
    cos, sin = _rope_tables(L)
    wgt, bgt = _gate_weights(w[:, o_g:o_aq], b_mgate[l])
    wgt, bgt = wgt.astype(BF16), bgt.reshape(-1, 1)
    wkt = w[:, o_mk:o_mv].T.astype(BF16)
    partner = _rope_partner(jnp.arange(HEAD_DIM))
    qw, kw = q_norm_w[l] * scale, k_norm_w[l]
    plain = lambda a, b, act="none", sc=1.0: ("plain", [wb(a, b)], dict(act=act, scale=sc))
    item_mkt = ("kt", [wkt], {})
    item_gates = ("gates", [wgt, bgt], {})

    mq, mkt, mv, og, gt = _fused_proj(
        x, mod, norm1_w[l], lambda b: b,
        [plain(o_mq, o_mk, sc=scale), item_mkt, plain(o_mv, o_og), plain(o_og, o_g, "sigmoid"), item_gates],
        "proj_mlstm")
    qa, ka, va, bg = _fused_proj(
        x, mod, norm1_w[l], lambda b: b,
        [("normrope", [wb(o_aq, o_ak), cos * qw, sin * qw[partner]], dict(rope=True)),
         ("normrope", [wb(o_ak, o_av), cos * kw, sin * kw[partner]], dict(rope=True)),
         plain(o_av, o_bg), plain(o_bg, o_bg + 2 * D, "sigmoid")],
        "proj_attn")

    lctx = ctx.shape[1]
    cmkt, cmv, cgt, cka, cva = _fused_proj(
        ctx, mod, norm1_w[l], lambda b: B,
        [item_mkt, plain(o_mv, o_og), item_gates,
         ("normrope", [wb(o_ak, o_av), jnp.broadcast_to(kw, (lctx, HEAD_DIM))], dict(rope=False)),
         plain(o_av, o_bg)],
        "proj_ctx")

    ga = _gate_prep(gt, "gate_prep")
    gca = _gate_prep(cgt, "gate_prep_ctx")
    ym = _mlstm(mq, mkt, mv, ga, og, cmkt, cmv, gca, mh_norm_w[l])
    oa = _attention(qa, ka, va, cka, cva)

    pad_w = jnp.zeros((EXPERT_ROW0 - N_GROUPS, D), F32)
    pad_b = jnp.zeros((EXPERT_ROW0 - N_GROUPS,), F32)
    wr = jnp.concatenate([w_rg[l].T, pad_w, w_re[l].T], axis=0)
    br = jnp.concatenate([b_rg[l], pad_b, b_re[l]]).reshape(ROUTER_ROWS, 1)
    hmid, f, route, w1, w2, cnt = _merge(ym, oa, bg, x, mod, w_branch_m[l].astype(BF16), w_branch_a[l].astype(BF16),
                                         w_out[l].astype(BF16), norm2_w[l], wr, br)

    off, tile_expert, n_used, n_rows = _expert_layout(cnt, 2 * B * L)
    xs = _dispatch(off, n_used, route, f.reshape(B * L, D), n_rows)
    ys = _experts(tile_expert, n_used, xs, w_e_gate[l].astype(BF16), w_e_up[l].astype(BF16),
                  w_e_down[l].astype(BF16))
    return _combine(off, route, hmid, mod, w1, w2, ys)
```

```python
import functools

import jax
import jax.numpy as jnp
from jax import lax
from jax.experimental import pallas as pl
from jax.experimental.pallas import tpu as pltpu

F32 = jnp.float32
BF16 = jnp.bfloat16

EPS = 1e-6
GRID_W = 64
ROPE_THETA = 10000.0
HEAD_DIM = 128
M_HEADS = 8
A_HEADS = 8
A_KV_HEADS = 2
A_GROUP = A_HEADS // A_KV_HEADS
N_GROUPS = 4
EXPERTS_PER_GROUP = 8
N_EXPERTS = N_GROUPS * EXPERTS_PER_GROUP
M_INIT = -1e30
MLSTM_CHUNK = 128
MLSTM_OUT_UNROLL = 16
GATE_ROWS = 8
ROUTER_LANES = 128
EXPERT_ROW0 = 8
ROUTER_ROWS = EXPERT_ROW0 + N_EXPERTS
EXPERT_TILE = 512
MOE_TOKEN_TILE = 1024
ROUTE_ROWS = 16
SLOT_ROWS = 8
VMEM_LIMIT_BYTES = 48 * 1024 * 1024

NT_DIMS = (((1,), (1,)), ((), ()))


def _cparams(*sem, flags=None):
    return pltpu.CompilerParams(dimension_semantics=sem, vmem_limit_bytes=VMEM_LIMIT_BYTES, flags=flags)


def _dot(a, b):
    return jnp.dot(a, b, preferred_element_type=F32)


def _dot_nt(a, b):
    return lax.dot_general(a, b, NT_DIMS, preferred_element_type=F32)


def _split3(x):
    hi = x.astype(BF16)
    r1 = x - hi.astype(F32)
    mid = r1.astype(BF16)
    lo = (r1 - mid.astype(F32)).astype(BF16)
    return hi, mid, lo


def _silu(x):
    return x * jax.nn.sigmoid(x)


def _ada_kernel(c_ref, w_ref, b_ref, o_ref):
    s = _silu(c_ref[...])
    parts = _split3(s)
    w = w_ref[...]
    w_hi = w.astype(BF16)
    w_lo = (w - w_hi.astype(F32)).astype(BF16)
    acc = _dot(parts[0], w_hi) + _dot(parts[1], w_hi) + _dot(parts[0], w_lo)
    o_ref[...] = acc + b_ref[...]


def _ada_mod(cc, w, b):
    rows, d = cc.shape
    n = w.shape[1]
    tn = min(n, 1536)
    return pl.pallas_call(
        _ada_kernel,
        grid=(n // tn,),
        in_specs=[
            pl.BlockSpec((rows, d), lambda j: (0, 0)),
            pl.BlockSpec((d, tn), lambda j: (0, j)),
            pl.BlockSpec((1, tn), lambda j: (0, j)),
        ],
        out_specs=pl.BlockSpec((rows, tn), lambda j: (0, j)),
        out_shape=jax.ShapeDtypeStruct((rows, n), F32),
        compiler_params=_cparams("parallel"),
        name="ada_mod",
    )(cc, w, b.reshape(1, n))


def _norm_mod_kernel(x_ref, mod_ref, nw_ref, o_ref):
    x = x_ref[0]
    var = jnp.mean(x * x, axis=-1, keepdims=True)
    y = x * lax.rsqrt(var + EPS) * nw_ref[...]
    sh = mod_ref[0, 0:1, :]
    sc = mod_ref[0, 1:2, :]
    o_ref[0] = (y * (1.0 + sc) + sh).astype(o_ref.dtype)


def _norm_mod(x, mod, nw, mod_row):
    B, L, D = x.shape
    tm = min(L, 512)
    return pl.pallas_call(
        _norm_mod_kernel,
        grid=(B, L // tm),
        in_specs=[
            pl.BlockSpec((1, tm, D), lambda b, i: (b, i, 0)),
            pl.BlockSpec((1, 6, D), lambda b, i: (mod_row(b), 0, 0)),
            pl.BlockSpec((1, D), lambda b, i: (0, 0)),
        ],
        out_specs=pl.BlockSpec((1, tm, D), lambda b, i: (b, i, 0)),
        out_shape=jax.ShapeDtypeStruct((B, L, D), BF16),
        compiler_params=_cparams("parallel", "parallel"),
        name="norm_mod",
    )(x, mod, nw.reshape(1, D))


def _proj_kernel(x_ref, w_ref, o_ref, *, act, scale):
    acc = _dot(x_ref[0], w_ref[...])
    if act == "sigmoid":
        acc = jax.nn.sigmoid(acc)
    if scale != 1.0:
        acc = acc * scale
    o_ref[0] = acc.astype(o_ref.dtype)


def _proj(xn, w, *, act="none", scale=1.0, name):
    B, L, D = xn.shape
    N = w.shape[1]
    tm = min(L, 512)
    tn = min(N, 1024)
    return pl.pallas_call(
        functools.partial(_proj_kernel, act=act, scale=scale),
        grid=(B, N // tn, L // tm),
        in_specs=[
            pl.BlockSpec((1, tm, D), lambda b, j, i: (b, i, 0)),
            pl.BlockSpec((D, tn), lambda b, j, i: (0, j)),
        ],
        out_specs=pl.BlockSpec((1, tm, tn), lambda b, j, i: (b, i, j)),
        out_shape=jax.ShapeDtypeStruct((B, L, N), BF16),
        compiler_params=_cparams("parallel", "parallel", "parallel"),
        name=name,
    )(xn, w)


def _proj_t_kernel(w_ref, x_ref, o_ref, *, nchunk, chunk):
    acc = _dot_nt(w_ref[...], x_ref[0])
    for c in range(nchunk):
        o_ref[0, c] = acc[:, c * chunk:(c + 1) * chunk].astype(o_ref.dtype)


def _proj_t(xn, wt, *, chunk, name):
    B, L, D = xn.shape
    R = wt.shape[0]
    tm = min(L, 512)
    nchunk = tm // chunk
    return pl.pallas_call(
        functools.partial(_proj_t_kernel, nchunk=nchunk, chunk=chunk),
        grid=(B, L // tm),
        in_specs=[
            pl.BlockSpec((R, D), lambda b, i: (0, 0)),
            pl.BlockSpec((1, tm, D), lambda b, i: (b, i, 0)),
        ],
        out_specs=pl.BlockSpec((1, nchunk, R, chunk), lambda b, i: (b, i, 0, 0)),
        out_shape=jax.ShapeDtypeStruct((B, L // chunk, R, chunk), BF16),
        compiler_params=_cparams("parallel", "parallel"),
        name=name,
    )(wt, xn)


def _proj_gates_kernel(w_ref, x_ref, b_ref, o_ref, *, nchunk, chunk):
    acc = _dot_nt(w_ref[...], x_ref[0]) + b_ref[...]
    for h in range(M_HEADS):
        for c in range(nchunk):
            o_ref[0, h, c] = acc[h * GATE_ROWS:(h + 1) * GATE_ROWS, c * chunk:(c + 1) * chunk]


def _proj_gates(xn, wt, bias, *, chunk, name):
    B, L, D = xn.shape
    R = wt.shape[0]
    tm = min(L, 512)
    nchunk = tm // chunk
    return pl.pallas_call(
        functools.partial(_proj_gates_kernel, nchunk=nchunk, chunk=chunk),
        grid=(B, L // tm),
        in_specs=[
            pl.BlockSpec((R, D), lambda b, i: (0, 0)),
            pl.BlockSpec((1, tm, D), lambda b, i: (b, i, 0)),
            pl.BlockSpec((R, 1), lambda b, i: (0, 0)),
        ],
        out_specs=pl.BlockSpec((1, M_HEADS, nchunk, GATE_ROWS, chunk), lambda b, i: (b, 0, i, 0, 0)),
        out_shape=jax.ShapeDtypeStruct((B, M_HEADS, L // chunk, GATE_ROWS, chunk), F32),
        compiler_params=_cparams("parallel", "parallel"),
        name=name,
    )(wt, xn, bias.reshape(R, 1))


def _rope_partner(j):
    quarter = HEAD_DIM // 4
    return jnp.where((j % (2 * quarter)) < quarter, j + quarter, j - quarter)


def _split2_lanes(x):
    hi = x.astype(BF16)
    return jnp.concatenate([hi, (x - hi.astype(F32)).astype(BF16)], axis=1)


def _proj_normrope_kernel(x_ref, w_ref, cw_ref, sw_ref, o_ref, *, nheads, rope):
    acc = _dot(x_ref[0], w_ref[...])
    cw = cw_ref[...]
    avg = jnp.full((2 * HEAD_DIM, HEAD_DIM), 1.0 / HEAD_DIM, BF16)
    if rope:
        sw = sw_ref[...]
        src = lax.broadcasted_iota(jnp.int32, (2 * HEAD_DIM, HEAD_DIM), 0) % HEAD_DIM
        dst = lax.broadcasted_iota(jnp.int32, (2 * HEAD_DIM, HEAD_DIM), 1)
        perm = jnp.where(src == _rope_partner(dst), 1.0, 0.0).astype(BF16)
    for h in range(nheads):
        a = acc[:, h * HEAD_DIM:(h + 1) * HEAD_DIM]
        var = _dot(_split2_lanes(a * a), avg)
        y = a * cw
        if rope:
            y = y + _dot(_split2_lanes(a), perm) * sw
        o_ref[0, :, h * HEAD_DIM:(h + 1) * HEAD_DIM] = (y * lax.rsqrt(var + EPS)).astype(o_ref.dtype)


def _proj_normrope(xn, w, cw, sw, *, rope, name):
    B, L, D = xn.shape
    N = w.shape[1]
    nheads = N // HEAD_DIM
    tm = min(L, 512)
    return pl.pallas_call(
        functools.partial(_proj_normrope_kernel, nheads=nheads, rope=rope),
        grid=(B, L // tm),
        in_specs=[
            pl.BlockSpec((1, tm, D), lambda b, i: (b, i, 0)),
            pl.BlockSpec((D, N), lambda b, i: (0, 0)),
            pl.BlockSpec((tm, HEAD_DIM), lambda b, i: (i, 0)),
            pl.BlockSpec((tm, HEAD_DIM), lambda b, i: (i, 0)),
        ],
        out_specs=pl.BlockSpec((1, tm, N), lambda b, i: (b, i, 0)),
        out_shape=jax.ShapeDtypeStruct((B, L, N), BF16),
        compiler_params=_cparams("parallel", "parallel"),
        name=name,
    )(xn, w, cw, sw)


def _fused_proj_kernel(*refs, plan):
    x_ref, mod_ref, nw_ref = refs[:3]
    n_inputs = sum(n for _, n, _ in plan)
    in_refs = refs[3:3 + n_inputs]
    out_refs = refs[3 + n_inputs:]
    x = x_ref[0]
    var = jnp.mean(x * x, axis=-1, keepdims=True)
    y = x * lax.rsqrt(var + EPS) * nw_ref[...]
    xn = (y * (1.0 + mod_ref[0, 1:2, :]) + mod_ref[0, 0:1, :]).astype(BF16)
    tm = xn.shape[0]
    C = MLSTM_CHUNK
    pos = 0
    for (kind, n, prm), o_ref in zip(plan, out_refs):
        ins = in_refs[pos:pos + n]
        pos += n
        if kind == "plain":
            acc = _dot(xn, ins[0][...])
            if prm["act"] == "sigmoid":
                acc = jax.nn.sigmoid(acc)
            if prm["scale"] != 1.0:
                acc = acc * prm["scale"]
            o_ref[0] = acc.astype(o_ref.dtype)
        elif kind == "kt":
            acc = _dot_nt(ins[0][...], xn)
            for c in range(tm // C):
                o_ref[0, c] = acc[:, c * C:(c + 1) * C].astype(o_ref.dtype)
        elif kind == "gates":
            acc = _dot_nt(ins[0][...], xn) + ins[1][...]
            for h in range(M_HEADS):
                for c in range(tm // C):
                    o_ref[0, h, c] = acc[h * GATE_ROWS:(h + 1) * GATE_ROWS, c * C:(c + 1) * C]
        else:
            acc = _dot(xn, ins[0][...])
            cw = ins[1][...]
            avg = jnp.full((2 * HEAD_DIM, HEAD_DIM), 1.0 / HEAD_DIM, BF16)
            if prm["rope"]:
                sw = ins[2][...]
                src = lax.broadcasted_iota(jnp.int32, (2 * HEAD_DIM, HEAD_DIM), 0) % HEAD_DIM
                dst = lax.broadcasted_iota(jnp.int32, (2 * HEAD_DIM, HEAD_DIM), 1)
                perm = jnp.where(src == _rope_partner(dst), 1.0, 0.0).astype(BF16)
            for h in range(acc.shape[1] // HEAD_DIM):
                a = acc[:, h * HEAD_DIM:(h + 1) * HEAD_DIM]
                hvar = _dot(_split2_lanes(a * a), avg)
                yh = a * cw
                if prm["rope"]:
                    yh = yh + _dot(_split2_lanes(a), perm) * sw
                o_ref[0, :, h * HEAD_DIM:(h + 1) * HEAD_DIM] = (yh * lax.rsqrt(hvar + EPS)).astype(o_ref.dtype)


def _fused_proj(x, mod, nw, mod_row, items, name):
    B, L, D = x.shape
    tm = min(L, 512)
    C = MLSTM_CHUNK
    full2 = lambda b, i: (0, 0)
    in_specs = [
        pl.BlockSpec((1, tm, D), lambda b, i: (b, i, 0)),
        pl.BlockSpec((1, 6, D), lambda b, i: (mod_row(b), 0, 0)),
        pl.BlockSpec((1, D), full2),
    ]
    args = [x, mod, nw.reshape(1, D)]
    out_specs, out_shape, plan = [], [], []
    for kind, arrays, prm in items:
        plan.append((kind, len(arrays), prm))
        args.extend(arrays)
        w = arrays[0]
        if kind in ("plain", "normrope"):
            N = w.shape[1]
            in_specs.append(pl.BlockSpec((D, N), full2))
            for t in arrays[1:]:
                in_specs.append(pl.BlockSpec((tm, HEAD_DIM), lambda b, i: (i, 0)))
            out_specs.append(pl.BlockSpec((1, tm, N), lambda b, i: (b, i, 0)))
            out_shape.append(jax.ShapeDtypeStruct((B, L, N), BF16))
        elif kind == "kt":
            R = w.shape[0]
            in_specs.append(pl.BlockSpec((R, D), full2))
            out_specs.append(pl.BlockSpec((1, tm // C, R, C), lambda b, i: (b, i, 0, 0)))
            out_shape.append(jax.ShapeDtypeStruct((B, L // C, R, C), BF16))
        else:
            R = w.shape[0]
            in_specs.append(pl.BlockSpec((R, D), full2))
            in_specs.append(pl.BlockSpec((R, 1), full2))
            out_specs.append(pl.BlockSpec((1, M_HEADS, tm // C, GATE_ROWS, C), lambda b, i: (b, 0, i, 0, 0)))
            out_shape.append(jax.ShapeDtypeStruct((B, M_HEADS, L // C, GATE_ROWS, C), F32))
    return pl.pallas_call(
        functools.partial(_fused_proj_kernel, plan=tuple(plan)),
        grid=(B, L // tm),
        in_specs=in_specs,
        out_specs=out_specs,
        out_shape=out_shape,
        compiler_params=_cparams("parallel", "parallel"),
        name=name,
    )(*args)


def _rope_tables(L):
    rows = L // GRID_W
    row = jnp.repeat(jnp.arange(rows), GRID_W).astype(F32)
    col = jnp.tile(jnp.arange(GRID_W), rows).astype(F32)
    half = HEAD_DIM // 2
    inv = ROPE_THETA ** (-jnp.arange(0, half, 2, dtype=F32) / half)
    ang_r = row[:, None] * inv[None, :]
    ang_c = col[:, None] * inv[None, :]
    cr, sr, cc, sc = jnp.cos(ang_r), jnp.sin(ang_r), jnp.cos(ang_c), jnp.sin(ang_c)
    return jnp.concatenate([cr, cr, cc, cc], axis=-1), jnp.concatenate([-sr, sr, -sc, sc], axis=-1)


def _log_sigmoid(x):
    return jnp.minimum(x, 0.0) - jnp.log1p(jnp.exp(-jnp.abs(x)))


def _tri_masks(n):
    row_i = lax.broadcasted_iota(jnp.int32, (n, n), 0)
    col_i = lax.broadcasted_iota(jnp.int32, (n, n), 1)
    return row_i <= col_i, row_i >= col_i


def _gate_prep_kernel(g_ref, a_ref):
    _, H, nc, R8, C = g_ref.shape
    rows = H * nc * R8
    g = g_ref[0].reshape(rows, C)
    le, ge = _tri_masks(C)
    upper = jnp.where(le, 1.0, 0.0).astype(BF16)
    lower = jnp.where(ge, 1.0, 0.0).astype(BF16)
    lf = _log_sigmoid(g)
    parts = _split3(lf)
    pre = _dot(parts[0], upper) + _dot(parts[1], upper) + _dot(parts[2], upper)
    suf = _dot(parts[0], lower) + _dot(parts[1], lower) + _dot(parts[2], lower)
    rtype = lax.broadcasted_iota(jnp.int32, (rows, C), 0) % R8
    a = jnp.where(rtype == 1, pre, jnp.where(rtype == 3, suf, jnp.where(rtype >= 4, lf, g)))
    a_ref[0] = a.reshape(H, nc, R8, C)


def _gate_prep(g, name):
    B, H, nc, R8, C = g.shape
    blk = pl.BlockSpec((1, H, nc, R8, C), lambda b: (b, 0, 0, 0, 0))
    return pl.pallas_call(
        _gate_prep_kernel,
        grid=(B,),
        in_specs=[blk],
        out_specs=blk,
        out_shape=jax.ShapeDtypeStruct(g.shape, F32),
        compiler_params=_cparams("parallel"),
        name=name,
    )(g)


def _mlstm_kernel(q_ref, kt_ref, v_ref, a_ref, og_ref, kct_ref, vc_ref, ac_ref, nw_ref, o_ref, cin_ref, min_ref,
                  *, nc, nctx):
    C = MLSTM_CHUNK
    D = HEAD_DIM
    le, ge = _tri_masks(C)
    le_bf = jnp.where(le, 1.0, 0.0).astype(BF16)
    ge_bf = jnp.where(ge, 1.0, 0.0).astype(BF16)
    ones_blk = jnp.ones((C, D), BF16)

    def gates(ref, j, fwd):
        ir, fr = (0, 1) if fwd else (2, 3)
        b_row = ref[0, 0, j, fr:fr + 1, :]
        a_row = ref[0, 0, j, ir:ir + 1, :] - b_row
        return a_row, (b_row[:, C - 1:C] if fwd else b_row[:, 0:1])

    def vaug_of(ref, j):
        return jnp.concatenate([ref[0, j * C:(j + 1) * C, :], ones_blk], axis=1)

    def update(cst, m_prev, kt, vaug, a_row, btot):
        mc = jnp.maximum(jnp.max(a_row, axis=1, keepdims=True), m_prev)
        w_in = jnp.exp(a_row - mc)
        ktw = (kt.astype(F32) * w_in).astype(BF16)
        return jnp.exp(m_prev - mc) * cst + _dot(ktw, vaug), btot + mc

    for d, fwd in ((0, True), (1, False)):
        cst = jnp.zeros((D, 2 * D), F32)
        m = jnp.full((1, 1), M_INIT, F32)
        for j in (range(nctx) if fwd else reversed(range(nctx))):
            a_row, btot = gates(ac_ref, j, fwd)
            cst, m = update(cst, m, kct_ref[0, j], vaug_of(vc_ref, j), a_row, btot)
        order = list(range(nc) if fwd else reversed(range(nc)))
        for idx, j in enumerate(order):
            cin_ref[d, j] = cst.astype(BF16)
            min_ref[d, j] = jnp.broadcast_to(m, min_ref.shape[2:])
            if idx + 1 < nc:
                a_row, btot = gates(a_ref, j, fwd)
                cst, m = update(cst, m, kt_ref[0, j], vaug_of(v_ref, j), a_row, btot)

    def out_step(d, j, fwd, q, kt, vaug):
        a_row, _ = gates(a_ref, j, fwd)
        m_prev = min_ref[d, j][0:1, 0:1]
        nlf = -a_ref[0, 0, j, (4 if fwd else 5):(5 if fwd else 6), :]
        nlf_hi = nlf.astype(BF16)
        nlf_lo = (nlf - nlf_hi.astype(F32)).astype(BF16)
        vis = ge_bf if fwd else le_bf
        nb = _dot_nt(vis, jnp.broadcast_to(nlf_hi, (D, C))) + _dot_nt(vis, jnp.broadcast_to(nlf_lo, (D, C)))
        a_mat = jnp.where(ge if fwd else le, a_row, -jnp.inf)
        m_q = jnp.maximum(jnp.max(a_mat, axis=1, keepdims=True), m_prev)
        bm = jnp.broadcast_to(m_q, (C, D))
        w_intra = jnp.exp(a_mat - bm)
        p = (_dot(q, kt) * w_intra).astype(BF16)
        qc = _dot(q, cin_ref[d, j])
        pv = _dot(p, vaug)
        w_inter = jnp.exp(m_prev - bm)
        num = w_inter * qc[:, :D] + pv[:, :D]
        den = w_inter * qc[:, D:] + pv[:, D:]
        return num / jnp.maximum(jnp.abs(den), jnp.exp(nb - bm))

    nw = nw_ref[...]

    def chunk_out(j, carry):
        sl = pl.ds(pl.multiple_of(j * C, C), C)
        q = q_ref[0, sl, :]
        kt = kt_ref[0, j]
        vaug = jnp.concatenate([v_ref[0, sl, :], ones_blk], axis=1)
        h = out_step(0, j, True, q, kt, vaug) + out_step(1, j, False, q, kt, vaug)
        var = jnp.mean(h * h, axis=-1, keepdims=True)
        y = h * lax.rsqrt(var + EPS) * nw
        o_ref[0, sl, :] = (og_ref[0, sl, :].astype(F32) * y).astype(o_ref.dtype)
        return carry

    lax.fori_loop(0, nc, chunk_out, 0, unroll=MLSTM_OUT_UNROLL)


def _mlstm(q, kt, v, a, og, kct, vc, ac, mh_norm_w):
    B, L, W = q.shape
    C = MLSTM_CHUNK
    assert C == HEAD_DIM
    nc = L // C
    lctx = vc.shape[1]
    nctx = lctx // C
    H = W // HEAD_DIM
    return pl.pallas_call(
        functools.partial(_mlstm_kernel, nc=nc, nctx=nctx),
        grid=(B, H),
        in_specs=[
            pl.BlockSpec((1, L, HEAD_DIM), lambda b, h: (b, 0, h)),
            pl.BlockSpec((1, nc, HEAD_DIM, C), lambda b, h: (b, 0, h, 0)),
            pl.BlockSpec((1, L, HEAD_DIM), lambda b, h: (b, 0, h)),
            pl.BlockSpec((1, 1, nc, GATE_ROWS, C), lambda b, h: (b, h, 0, 0, 0)),
            pl.BlockSpec((1, L, HEAD_DIM), lambda b, h: (b, 0, h)),
            pl.BlockSpec((1, nctx, HEAD_DIM, C), lambda b, h: (b, 0, h, 0)),
            pl.BlockSpec((1, lctx, HEAD_DIM), lambda b, h: (b, 0, h)),
            pl.BlockSpec((1, 1, nctx, GATE_ROWS, C), lambda b, h: (b, h, 0, 0, 0)),
            pl.BlockSpec((1, HEAD_DIM), lambda b, h: (0, h)),
        ],
        out_specs=pl.BlockSpec((1, L, HEAD_DIM), lambda b, h: (b, 0, h)),
        out_shape=jax.ShapeDtypeStruct((B, L, W), BF16),
        scratch_shapes=[
            pltpu.VMEM((2, nc, HEAD_DIM, 2 * HEAD_DIM), BF16),
            pltpu.VMEM((2, nc, 8, HEAD_DIM), F32),
        ],
        compiler_params=_cparams("parallel", "parallel"),
        name="mlstm",
    )(q, kt, v, a, og, kct, vc, ac, mh_norm_w.reshape(1, W))


def _attn_kernel(q_ref, k_ref, v_ref, kc_ref, vc_ref, o_ref):
    ones_l = jnp.ones((v_ref.shape[1], HEAD_DIM), BF16)
    ones_c = jnp.ones((vc_ref.shape[1], HEAD_DIM), BF16)
    for g in range(A_KV_HEADS):
        gs = slice(g * HEAD_DIM, (g + 1) * HEAD_DIM)
        k = k_ref[0, :, gs]
        kc = kc_ref[0, :, gs]
        v = jnp.concatenate([v_ref[0, :, gs], ones_l], axis=1)
        vc = jnp.concatenate([vc_ref[0, :, gs], ones_c], axis=1)
        for h in range(g * A_GROUP, (g + 1) * A_GROUP):
            hs = slice(h * HEAD_DIM, (h + 1) * HEAD_DIM)
            q = q_ref[0, :, hs]
            s1 = _dot_nt(q, k)
            s2 = _dot_nt(q, kc)
            m = jnp.maximum(jnp.max(s1, axis=1, keepdims=True), jnp.max(s2, axis=1, keepdims=True))
            ol = _dot(jnp.exp(s1 - m).astype(BF16), v) + _dot(jnp.exp(s2 - m).astype(BF16), vc)
            o_ref[0, :, hs] = (ol[:, :HEAD_DIM] / ol[:, HEAD_DIM:]).astype(o_ref.dtype)


def _attention(q, k, v, kc, vc):
    B, L, W = q.shape
    lctx = kc.shape[1]
    tq = min(L, 512)
    kvw = A_KV_HEADS * HEAD_DIM
    return pl.pallas_call(
        _attn_kernel,
        grid=(B, L // tq),
        in_specs=[
            pl.BlockSpec((1, tq, W), lambda b, i: (b, i, 0)),
            pl.BlockSpec((1, L, kvw), lambda b, i: (b, 0, 0)),
            pl.BlockSpec((1, L, kvw), lambda b, i: (b, 0, 0)),
            pl.BlockSpec((1, lctx, kvw), lambda b, i: (b, 0, 0)),
            pl.BlockSpec((1, lctx, kvw), lambda b, i: (b, 0, 0)),
        ],
        out_specs=pl.BlockSpec((1, tq, W), lambda b, i: (b, i, 0)),
        out_shape=jax.ShapeDtypeStruct((B, L, W), BF16),
        compiler_params=_cparams("parallel", "parallel"),
        name="attention",
    )(q, k, v, kc, vc)


def _merge_kernel(ym_ref, oa_ref, bg_ref, x_ref, mod_ref, wbm_ref, wba_ref, wo_ref, n2_ref, wr_ref, br_ref,
                  hmid_ref, f_ref, route_ref, w1_ref, w2_ref, cnt_ref, run_ref):
    D = x_ref.shape[-1]
    tm = x_ref.shape[1]

    @pl.when((pl.program_id(0) == 0) & (pl.program_id(1) == 0))
    def _():
        run_ref[...] = jnp.zeros_like(run_ref)

    ym = _dot(ym_ref[0], wbm_ref[...])
    ya = _dot(oa_ref[0], wba_ref[...])
    g0 = bg_ref[0, :, :D].astype(F32)
    g1 = bg_ref[0, :, D:].astype(F32)
    mix = _dot((g0 * ym + g1 * ya).astype(BF16), wo_ref[...])
    hmid = x_ref[0] + mod_ref[0, 2:3, :] * mix
    hmid_ref[0] = hmid
    var = jnp.mean(hmid * hmid, axis=-1, keepdims=True)
    y = hmid * lax.rsqrt(var + EPS) * n2_ref[...]
    f = y * (1.0 + mod_ref[0, 4:5, :]) + mod_ref[0, 3:4, :]
    f_ref[0] = f.astype(f_ref.dtype)

    fp = _split3(f)
    wr = wr_ref[...]
    wr_hi = wr.astype(BF16)
    wr_lo = (wr - wr_hi.astype(F32)).astype(BF16)
    logits = (_dot_nt(wr_hi, fp[0]) + _dot_nt(wr_hi, fp[1]) + _dot_nt(wr_lo, fp[0]) + _dot_nt(wr_hi, fp[2])
              + _dot_nt(wr_lo, fp[1])) + br_ref[...]
    row = lax.broadcasted_iota(jnp.int32, logits.shape, 0)
    big = jnp.int32(ROUTER_ROWS)
    neg = -jnp.inf
    is_grp = row < N_GROUPS
    gl = jnp.where(is_grp, logits, neg)
    gmax = jnp.max(gl, axis=0, keepdims=True)
    gsel = jnp.min(jnp.where(gl == gmax, row, big), axis=0, keepdims=True)
    p_grp = 1.0 / jnp.sum(jnp.where(is_grp, jnp.exp(logits - gmax), 0.0), axis=0, keepdims=True)
    lo = EXPERT_ROW0 + gsel * EXPERTS_PER_GROUP
    in_grp = (row >= lo) & (row < lo + EXPERTS_PER_GROUP)
    el = jnp.where(in_grp, logits, neg)
    v1 = jnp.max(el, axis=0, keepdims=True)
    i1 = jnp.min(jnp.where(el == v1, row, big), axis=0, keepdims=True)
    el2 = jnp.where(row == i1, neg, el)
    v2 = jnp.max(el2, axis=0, keepdims=True)
    i2 = jnp.min(jnp.where(el2 == v2, row, big), axis=0, keepdims=True)
    e21 = jnp.exp(v2 - v1)
    w1 = p_grp / (1.0 + e21)
    w2 = p_grp * e21 / (1.0 + e21)

    sel = jnp.where((row == i1) | (row == i2), 1.0, 0.0)
    tok_u = lax.broadcasted_iota(jnp.int32, (tm, tm), 0)
    tok_n = lax.broadcasted_iota(jnp.int32, (tm, tm), 1)
    earlier = jnp.where(tok_u < tok_n, 1.0, 0.0).astype(BF16)
    rank_all = run_ref[...] + _dot(sel.astype(BF16), earlier)
    rank1 = jnp.sum(jnp.where(row == i1, rank_all, 0.0), axis=0, keepdims=True)
    rank2 = jnp.sum(jnp.where(row == i2, rank_all, 0.0), axis=0, keepdims=True)
    run_ref[...] += jnp.sum(sel, axis=1, keepdims=True)
    cnt_ref[...] = jnp.broadcast_to(run_ref[...], cnt_ref.shape)
    route_ref[...] = jnp.zeros_like(route_ref)
    route_ref[0:1, :] = i1 - EXPERT_ROW0
    route_ref[1:2, :] = i2 - EXPERT_ROW0
    route_ref[8:9, :] = rank1.astype(jnp.int32)
    route_ref[9:10, :] = rank2.astype(jnp.int32)

    eye = jnp.where(tok_u == tok_n, 1.0, 0.0).astype(BF16)
    for w_row, w_ref in ((w1, w1_ref), (w2, w2_ref)):
        parts = _split3(jnp.broadcast_to(w_row, (ROUTER_LANES, tm)))
        w_ref[0] = _dot_nt(eye, parts[0]) + _dot_nt(eye, parts[1]) + _dot_nt(eye, parts[2])


def _merge(ym, oa, bg, x, mod, wbm, wba, wo, n2, wr, br):
    B, L, D = x.shape
    tm = min(L, 512)
    nt = L // tm
    full = lambda b, i: (0, 0)
    tok = lambda b, i: (b, i, 0)
    return pl.pallas_call(
        _merge_kernel,
        grid=(B, L // tm),
        in_specs=[
            pl.BlockSpec((1, tm, D), tok),
            pl.BlockSpec((1, tm, D), tok),
            pl.BlockSpec((1, tm, 2 * D), tok),
            pl.BlockSpec((1, tm, D), tok),
            pl.BlockSpec((1, 6, D), lambda b, i: (b, 0, 0)),
            pl.BlockSpec((D, D), full),
            pl.BlockSpec((D, D), full),
            pl.BlockSpec((D, D), full),
            pl.BlockSpec((1, D), full),
            pl.BlockSpec((ROUTER_ROWS, D), full),
            pl.BlockSpec((ROUTER_ROWS, 1), full),
        ],
        out_specs=[
            pl.BlockSpec((1, tm, D), tok),
            pl.BlockSpec((1, tm, D), tok),
            pl.BlockSpec((ROUTE_ROWS, tm), lambda b, i: (0, b * nt + i)),
            pl.BlockSpec((1, tm, ROUTER_LANES), tok),
            pl.BlockSpec((1, tm, ROUTER_LANES), tok),
            pl.BlockSpec((ROUTER_ROWS, ROUTER_LANES), full),
        ],
        out_shape=[
            jax.ShapeDtypeStruct((B, L, D), F32),
            jax.ShapeDtypeStruct((B, L, D), F32),
            jax.ShapeDtypeStruct((ROUTE_ROWS, B * L), jnp.int32),
            jax.ShapeDtypeStruct((B, L, ROUTER_LANES), F32),
            jax.ShapeDtypeStruct((B, L, ROUTER_LANES), F32),
            jax.ShapeDtypeStruct((ROUTER_ROWS, ROUTER_LANES), F32),
        ],
        scratch_shapes=[pltpu.VMEM((ROUTER_ROWS, 1), F32)],
        compiler_params=_cparams("arbitrary", "arbitrary"),
        name="merge_router",
    )(ym, oa, bg, x, mod, wbm, wba, wo, n2.reshape(1, D), wr, br)


def _load_slots(off_ref, r_ref, s_vmem, s_smem, rsem):
    expert = r_ref[0:SLOT_ROWS, :]
    slot = r_ref[SLOT_ROWS:2 * SLOT_ROWS, :]
    for e in range(1, N_EXPERTS):
        slot = slot + jnp.where(expert == e, off_ref[e], 0)
    s_vmem[...] = slot
    cp = pltpu.make_async_copy(s_vmem, s_smem, rsem)
    cp.start()
    cp.wait()


def _dispatch_kernel(off_ref, nu_ref, r_ref, f_ref, xs_ref, s_vmem, s_smem, zbuf, sem, rsem, *, tm):
    T = EXPERT_TILE

    @pl.when(pl.program_id(0) == 0)
    def _():
        zbuf[...] = jnp.zeros_like(zbuf)

        def zero_tile(t, carry):
            cp = pltpu.make_async_copy(zbuf, xs_ref.at[pl.ds(pl.multiple_of(t * T, T), T), :], sem)
            cp.start()
            cp.wait()
            return carry

        lax.fori_loop(nu_ref[0], xs_ref.shape[0] // T, zero_tile, 0)
        for e in range(N_EXPERTS):
            start = off_ref[e]
            end = off_ref[e + 1]

            @pl.when(end > start)
            def _():
                cp = pltpu.make_async_copy(zbuf, xs_ref.at[pl.ds(pl.multiple_of(end - T, T), T), :], sem)
                cp.start()
                cp.wait()

    _load_slots(off_ref, r_ref, s_vmem, s_smem, rsem)
    for t in range(tm):
        for k in range(2):
            pltpu.make_async_copy(f_ref.at[pl.ds(t, 1), :], xs_ref.at[pl.ds(s_smem[k, t], 1), :], sem).start(priority=k)
    for _ in range(2):
        pltpu.make_async_copy(f_ref, f_ref, sem).wait()


def _dispatch(off, n_used, route, f, n_rows):
    N, D = f.shape
    tm = min(N, MOE_TOKEN_TILE)
    return pl.pallas_call(
        functools.partial(_dispatch_kernel, tm=tm),
        grid_spec=pltpu.PrefetchScalarGridSpec(
            num_scalar_prefetch=2,
            grid=(N // tm,),
            in_specs=[
                pl.BlockSpec((ROUTE_ROWS, tm), lambda i, off, nu: (0, i)),
                pl.BlockSpec((tm, D), lambda i, off, nu: (i, 0)),
            ],
            out_specs=pl.BlockSpec(memory_space=pl.ANY),
            scratch_shapes=[
                pltpu.VMEM((SLOT_ROWS, tm), jnp.int32),
                pltpu.SMEM((SLOT_ROWS, tm), jnp.int32),
                pltpu.VMEM((EXPERT_TILE, D), F32),
                pltpu.SemaphoreType.DMA(()),
                pltpu.SemaphoreType.DMA(()),
            ],
        ),
        out_shape=jax.ShapeDtypeStruct((n_rows, D), F32),
        compiler_params=_cparams("arbitrary"),
        name="moe_dispatch",
    )(off, n_used, route, f)


def _expert_kernel(te_ref, nu_ref, xs_ref, wg_ref, wu_ref, wd_ref, ys_ref):
    used = pl.program_id(0) < nu_ref[0]

    @pl.when(used)
    def _():
        x = xs_ref[...].astype(BF16)
        hidden = (_silu(_dot(x, wg_ref[0])) * _dot(x, wu_ref[0])).astype(BF16)
        ys_ref[...] = _dot(hidden, wd_ref[0])

    @pl.when(jnp.logical_not(used))
    def _():
        ys_ref[...] = jnp.zeros_like(ys_ref)


def _experts(tile_expert, n_used, xs, wg, wu, wd):
    P, D = xs.shape
    E, _, DE = wg.shape
    T = EXPERT_TILE
    row_map = lambda t, te, nu: (t, 0)
    w_map = lambda t, te, nu: (te[t], 0, 0)
    return pl.pallas_call(
        _expert_kernel,
        grid_spec=pltpu.PrefetchScalarGridSpec(
            num_scalar_prefetch=2,
            grid=(P // T,),
            in_specs=[
                pl.BlockSpec((T, D), row_map),
                pl.BlockSpec((1, D, DE), w_map),
                pl.BlockSpec((1, D, DE), w_map),
                pl.BlockSpec((1, DE, D), w_map),
            ],
            out_specs=pl.BlockSpec((T, D), row_map),
        ),
        out_shape=jax.ShapeDtypeStruct((P, D), F32),
        compiler_params=_cparams("arbitrary"),
        name="moe_experts",
    )(tile_expert, n_used, xs, wg, wu, wd)


def _combine_kernel(off_ref, r_ref, hmid_ref, mod_ref, w1_ref, w2_ref, ys_ref, o_ref, s_vmem, s_smem, y1, y2, sem,
                    rsem, *, tm):
    _load_slots(off_ref, r_ref, s_vmem, s_smem, rsem)
    for t in range(tm):
        for k, buf in ((0, y1), (1, y2)):
            pltpu.make_async_copy(ys_ref.at[pl.ds(s_smem[k, t], 1), :], buf.at[pl.ds(t, 1), :], sem).start(priority=k)
    pltpu.make_async_copy(y1, y1, sem).wait()
    pltpu.make_async_copy(y2, y2, sem).wait()
    w1 = w1_ref[0]
    w2 = w2_ref[0]
    D = o_ref.shape[-1]
    for c in range(D // ROUTER_LANES):
        sl = slice(c * ROUTER_LANES, (c + 1) * ROUTER_LANES)
        moe = w1 * y1[:, sl] + w2 * y2[:, sl]
        o_ref[0, :, sl] = hmid_ref[0, :, sl] + mod_ref[0, 5:6, sl] * moe


def _combine(off, route, hmid, mod, w1, w2, ys):
    B, L, D = hmid.shape
    tm = min(L, MOE_TOKEN_TILE)
    nt = L // tm
    tok = lambda b, i, off: (b, i, 0)
    return pl.pallas_call(
        functools.partial(_combine_kernel, tm=tm),
        grid_spec=pltpu.PrefetchScalarGridSpec(
            num_scalar_prefetch=1,
            grid=(B, nt),
            in_specs=[
                pl.BlockSpec((ROUTE_ROWS, tm), lambda b, i, off: (0, b * nt + i)),
                pl.BlockSpec((1, tm, D), tok),
                pl.BlockSpec((1, 6, D), lambda b, i, off: (b, 0, 0)),
                pl.BlockSpec((1, tm, ROUTER_LANES), tok),
                pl.BlockSpec((1, tm, ROUTER_LANES), tok),
                pl.BlockSpec(memory_space=pl.ANY),
            ],
            out_specs=pl.BlockSpec((1, tm, D), tok),
            scratch_shapes=[
                pltpu.VMEM((SLOT_ROWS, tm), jnp.int32),
                pltpu.SMEM((SLOT_ROWS, tm), jnp.int32),
                pltpu.VMEM((tm, D), F32),
                pltpu.VMEM((tm, D), F32),
                pltpu.SemaphoreType.DMA(()),
                pltpu.SemaphoreType.DMA(()),
            ],
        ),
        out_shape=jax.ShapeDtypeStruct((B, L, D), F32),
        compiler_params=_cparams("arbitrary", "arbitrary"),
        name="moe_combine",
    )(off, route, hmid, mod, w1, w2, ys)


def _expert_layout(cnt, n_pairs):
    T = EXPERT_TILE
    counts = cnt[EXPERT_ROW0:, 0].astype(jnp.int32)
    ends = jnp.cumsum((counts + T - 1) // T).astype(jnp.int32)
    off = jnp.concatenate([jnp.zeros((1,), jnp.int32), ends * T])
    n_tiles = n_pairs // T + N_EXPERTS
    t_idx = jnp.arange(n_tiles, dtype=jnp.int32)
    tile_expert = jnp.minimum(jnp.sum(t_idx[:, None] >= ends[None, :], axis=1), N_EXPERTS - 1).astype(jnp.int32)
    n_used = ends[-1]
    tile_expert = jnp.where(t_idx < n_used, tile_expert, jnp.take(tile_expert, n_used - 1))
    return off, tile_expert, n_used.reshape(1), n_tiles * T


def _gate_weights(w_gate, b_gate):
    D = w_gate.shape[0]
    wt = w_gate.T.reshape(4, M_HEADS, D).transpose(1, 0, 2)
    wt = jnp.concatenate([wt, wt[:, 1:2], wt[:, 3:4], jnp.zeros((M_HEADS, GATE_ROWS - 6, D), wt.dtype)], axis=1)
    bt = b_gate.reshape(4, M_HEADS).T
    bt = jnp.concatenate([bt, bt[:, 1:2], bt[:, 3:4], jnp.zeros((M_HEADS, GATE_ROWS - 6), bt.dtype)], axis=1)
    return wt.reshape(M_HEADS * GATE_ROWS, D), bt.reshape(M_HEADS * GATE_ROWS)


def kernel(x, c, ctx, c_ctx, w_ada, b_ada, norm1_w, w_in, b_mgate, q_norm_w, k_norm_w, mh_norm_w, w_branch_m,
           w_branch_a, w_out, norm2_w, w_rg, b_rg, w_re, b_re, w_e_gate, w_e_up, w_e_down):
    B, L, D = x.shape
    depth = w_ada.shape[0]
    assert depth == 1, "context-stream update between layers is not implemented"
    l = 0
    mw = M_HEADS * HEAD_DIM
    aq = A_HEADS * HEAD_DIM
    akv = A_KV_HEADS * HEAD_DIM
    o_mq, o_mk, o_mv, o_og = 0, mw, 2 * mw, 3 * mw
    o_g = 4 * mw
    o_aq = o_g + 4 * M_HEADS
    o_ak = o_aq + aq
    o_av = o_ak + akv
    o_bg = o_av + akv
    scale = HEAD_DIM ** -0.5

    rows = ((B + 1 + 7) // 8) * 8
    cc = jnp.concatenate([c, c_ctx[None, :], jnp.zeros((rows - B - 1, D), F32)], axis=0)
    mod = _ada_mod(cc, w_ada[l], b_ada[l]).reshape(rows, 6, D)

    w = w_in[l]
    wb = lambda a, b: w[:, a:b].astype(BF16)
    cos, sin = _rope_tables(L)
    wgt, bgt = _gate_weights(w[:, o_g:o_aq], b_mgate[l])
    wgt, bgt = wgt.astype(BF16), bgt.reshape(-1, 1)
    wkt = w[:, o_mk:o_mv].T.astype(BF16)
    partner = _rope_partner(jnp.arange(HEAD_DIM))
    qw, kw = q_norm_w[l] * scale, k_norm_w[l]
    plain = lambda a, b, act="none", sc=1.0: ("plain", [wb(a, b)], dict(act=act, scale=sc))
    item_mkt = ("kt", [wkt], {})
    item_gates = ("gates", [wgt, bgt], {})

    mq, mkt, mv, og, gt = _fused_proj(
        x, mod, norm1_w[l], lambda b: b,
        [plain(o_mq, o_mk, sc=scale), item_mkt, plain(o_mv, o_og), plain(o_og, o_g, "sigmoid"), item_gates],
        "proj_mlstm")
    qa, ka, va, bg = _fused_proj(
        x, mod, norm1_w[l], lambda b: b,
        [("normrope", [wb(o_aq, o_ak), cos * qw, sin * qw[partner]], dict(rope=True)),
         ("normrope", [wb(o_ak, o_av), cos * kw, sin * kw[partner]], dict(rope=True)),
         plain(o_av, o_bg), plain(o_bg, o_bg + 2 * D, "sigmoid")],
        "proj_attn")

    lctx = ctx.shape[1]
    cmkt, cmv, cgt, cka, cva = _fused_proj(
        ctx, mod, norm1_w[l], lambda b: B,
        [item_mkt, plain(o_mv, o_og), item_gates,
         ("normrope", [wb(o_ak, o_av), jnp.broadcast_to(kw, (lctx, HEAD_DIM))], dict(rope=False)),
         plain(o_av, o_bg)],
        "proj_ctx")

    ga = _gate_prep(gt, "gate_prep")
    gca = _gate_prep(cgt, "gate_prep_ctx")
    ym = _mlstm(mq, mkt, mv, ga, og, cmkt, cmv, gca, mh_norm_w[l])
    oa = _attention(qa, ka, va, cka, cva)

    pad_w = jnp.zeros((EXPERT_ROW0 - N_GROUPS, D), F32)
    pad_b = jnp.zeros((EXPERT_ROW0 - N_GROUPS,), F32)
    wr = jnp.concatenate([w_rg[l].T, pad_w, w_re[l].T], axis=0)
    br = jnp.concatenate([b_rg[l], pad_b, b_re[l]]).reshape(ROUTER_ROWS, 1)
    hmid, f, route, w1, w2, cnt = _merge(ym, oa, bg, x, mod, w_branch_m[l].astype(BF16), w_branch_a[l].astype(BF16),
                                         w_out[l].astype(BF16), norm2_w[l], wr, br)

    off, tile_expert, n_used, n_rows = _expert_layout(cnt, 2 * B * L)
    xs = _dispatch(off, n_used, route, f.reshape(B * L, D), n_rows)
    ys = _experts(tile_expert, n_used, xs, w_e_gate[l].astype(BF16), w_e_up[l].astype(BF16),
                  w_e_down[l].astype(BF16))
    return _combine(off, route, hmid, mod, w1, w2, ys)
```

```python
import functools

import jax
import jax.numpy as jnp
from jax import lax
from jax.experimental import pallas as pl
from jax.experimental.pallas import tpu as pltpu

F32 = jnp.float32
BF16 = jnp.bfloat16

EPS = 1e-6
GRID_W = 64
ROPE_THETA = 10000.0
HEAD_DIM = 128
M_HEADS = 8
A_HEADS = 8
A_KV_HEADS = 2
A_GROUP = A_HEADS // A_KV_HEADS
N_GROUPS = 4
EXPERTS_PER_GROUP = 8
N_EXPERTS = N_GROUPS * EXPERTS_PER_GROUP
M_INIT = -1e30
MLSTM_CHUNK = 128
MLSTM_OUT_UNROLL = 16
GATE_ROWS = 8
ROUTER_LANES = 128
EXPERT_ROW0 = 8
ROUTER_ROWS = EXPERT_ROW0 + N_EXPERTS
EXPERT_TILE = 512
MOE_TOKEN_TILE = 1024
DISPATCH_TILE = 2048
ROUTE_ROWS = 16
SLOT_ROWS = 8
VMEM_LIMIT_BYTES = 48 * 1024 * 1024

NT_DIMS = (((1,), (1,)), ((), ()))


def _cparams(*sem, flags=None):
    return pltpu.CompilerParams(dimension_semantics=sem, vmem_limit_bytes=VMEM_LIMIT_BYTES, flags=flags)


def _dot(a, b):
    return jnp.dot(a, b, preferred_element_type=F32)


def _dot_nt(a, b):
    return lax.dot_general(a, b, NT_DIMS, preferred_element_type=F32)


def _split3(x):
    hi = x.astype(BF16)
    r1 = x - hi.astype(F32)
    mid = r1.astype(BF16)
    lo = (r1 - mid.astype(F32)).astype(BF16)
    return hi, mid, lo


def _silu(x):
    return x * jax.nn.sigmoid(x)


def _ada_kernel(c_ref, w_ref, b_ref, o_ref):
    s = _silu(c_ref[...])
    parts = _split3(s)
    w = w_ref[...]
    w_hi = w.astype(BF16)
    w_lo = (w - w_hi.astype(F32)).astype(BF16)
    acc = _dot(parts[0], w_hi) + _dot(parts[1], w_hi) + _dot(parts[0], w_lo)
    o_ref[...] = acc + b_ref[...]


def _ada_mod(cc, w, b):
    rows, d = cc.shape
    n = w.shape[1]
    tn = min(n, 1536)
    return pl.pallas_call(
        _ada_kernel,
        grid=(n // tn,),
        in_specs=[
            pl.BlockSpec((rows, d), lambda j: (0, 0)),
            pl.BlockSpec((d, tn), lambda j: (0, j)),
            pl.BlockSpec((1, tn), lambda j: (0, j)),
        ],
        out_specs=pl.BlockSpec((rows, tn), lambda j: (0, j)),
        out_shape=jax.ShapeDtypeStruct((rows, n), F32),
        compiler_params=_cparams("parallel"),
        name="ada_mod",
    )(cc, w, b.reshape(1, n))


def _norm_mod_kernel(x_ref, mod_ref, nw_ref, o_ref):
    x = x_ref[0]
    var = jnp.mean(x * x, axis=-1, keepdims=True)
    y = x * lax.rsqrt(var + EPS) * nw_ref[...]
    sh = mod_ref[0, 0:1, :]
    sc = mod_ref[0, 1:2, :]
    o_ref[0] = (y * (1.0 + sc) + sh).astype(o_ref.dtype)


def _norm_mod(x, mod, nw, mod_row):
    B, L, D = x.shape
    tm = min(L, 512)
    return pl.pallas_call(
        _norm_mod_kernel,
        grid=(B, L // tm),
        in_specs=[
            pl.BlockSpec((1, tm, D), lambda b, i: (b, i, 0)),
            pl.BlockSpec((1, 6, D), lambda b, i: (mod_row(b), 0, 0)),
            pl.BlockSpec((1, D), lambda b, i: (0, 0)),
        ],
        out_specs=pl.BlockSpec((1, tm, D), lambda b, i: (b, i, 0)),
        out_shape=jax.ShapeDtypeStruct((B, L, D), BF16),
        compiler_params=_cparams("parallel", "parallel"),
        name="norm_mod",
    )(x, mod, nw.reshape(1, D))


def _proj_kernel(x_ref, w_ref, o_ref, *, act, scale):
    acc = _dot(x_ref[0], w_ref[...])
    if act == "sigmoid":
        acc = jax.nn.sigmoid(acc)
    if scale != 1.0:
        acc = acc * scale
    o_ref[0] = acc.astype(o_ref.dtype)


def _proj(xn, w, *, act="none", scale=1.0, name):
    B, L, D = xn.shape
    N = w.shape[1]
    tm = min(L, 512)
    tn = min(N, 1024)
    return pl.pallas_call(
        functools.partial(_proj_kernel, act=act, scale=scale),
        grid=(B, N // tn, L // tm),
        in_specs=[
            pl.BlockSpec((1, tm, D), lambda b, j, i: (b, i, 0)),
            pl.BlockSpec((D, tn), lambda b, j, i: (0, j)),
        ],
        out_specs=pl.BlockSpec((1, tm, tn), lambda b, j, i: (b, i, j)),
        out_shape=jax.ShapeDtypeStruct((B, L, N), BF16),
        compiler_params=_cparams("parallel", "parallel", "parallel"),
        name=name,
    )(xn, w)


def _proj_t_kernel(w_ref, x_ref, o_ref, *, nchunk, chunk):
    acc = _dot_nt(w_ref[...], x_ref[0])
    for c in range(nchunk):
        o_ref[0, c] = acc[:, c * chunk:(c + 1) * chunk].astype(o_ref.dtype)


def _proj_t(xn, wt, *, chunk, name):
    B, L, D = xn.shape
    R = wt.shape[0]
    tm = min(L, 512)
    nchunk = tm // chunk
    return pl.pallas_call(
        functools.partial(_proj_t_kernel, nchunk=nchunk, chunk=chunk),
        grid=(B, L // tm),
        in_specs=[
            pl.BlockSpec((R, D), lambda b, i: (0, 0)),
            pl.BlockSpec((1, tm, D), lambda b, i: (b, i, 0)),
        ],
        out_specs=pl.BlockSpec((1, nchunk, R, chunk), lambda b, i: (b, i, 0, 0)),
        out_shape=jax.ShapeDtypeStruct((B, L // chunk, R, chunk), BF16),
        compiler_params=_cparams("parallel", "parallel"),
        name=name,
    )(wt, xn)


def _proj_gates_kernel(w_ref, x_ref, b_ref, o_ref, *, nchunk, chunk):
    acc = _dot_nt(w_ref[...], x_ref[0]) + b_ref[...]
    for h in range(M_HEADS):
        for c in range(nchunk):
            o_ref[0, h, c] = acc[h * GATE_ROWS:(h + 1) * GATE_ROWS, c * chunk:(c + 1) * chunk]


def _proj_gates(xn, wt, bias, *, chunk, name):
    B, L, D = xn.shape
    R = wt.shape[0]
    tm = min(L, 512)
    nchunk = tm // chunk
    return pl.pallas_call(
        functools.partial(_proj_gates_kernel, nchunk=nchunk, chunk=chunk),
        grid=(B, L // tm),
        in_specs=[
            pl.BlockSpec((R, D), lambda b, i: (0, 0)),
            pl.BlockSpec((1, tm, D), lambda b, i: (b, i, 0)),
            pl.BlockSpec((R, 1), lambda b, i: (0, 0)),
        ],
        out_specs=pl.BlockSpec((1, M_HEADS, nchunk, GATE_ROWS, chunk), lambda b, i: (b, 0, i, 0, 0)),
        out_shape=jax.ShapeDtypeStruct((B, M_HEADS, L // chunk, GATE_ROWS, chunk), F32),
        compiler_params=_cparams("parallel", "parallel"),
        name=name,
    )(wt, xn, bias.reshape(R, 1))


def _rope_partner(j):
    quarter = HEAD_DIM // 4
    return jnp.where((j % (2 * quarter)) < quarter, j + quarter, j - quarter)


def _split2_lanes(x):
    hi = x.astype(BF16)
    return jnp.concatenate([hi, (x - hi.astype(F32)).astype(BF16)], axis=1)


def _proj_normrope_kernel(x_ref, w_ref, cw_ref, sw_ref, o_ref, *, nheads, rope):
    acc = _dot(x_ref[0], w_ref[...])
    cw = cw_ref[...]
    avg = jnp.full((2 * HEAD_DIM, HEAD_DIM), 1.0 / HEAD_DIM, BF16)
    if rope:
        sw = sw_ref[...]
        src = lax.broadcasted_iota(jnp.int32, (2 * HEAD_DIM, HEAD_DIM), 0) % HEAD_DIM
        dst = lax.broadcasted_iota(jnp.int32, (2 * HEAD_DIM, HEAD_DIM), 1)
        perm = jnp.where(src == _rope_partner(dst), 1.0, 0.0).astype(BF16)
    for h in range(nheads):
        a = acc[:, h * HEAD_DIM:(h + 1) * HEAD_DIM]
        var = _dot(_split2_lanes(a * a), avg)
        y = a * cw
        if rope:
            y = y + _dot(_split2_lanes(a), perm) * sw
        o_ref[0, :, h * HEAD_DIM:(h + 1) * HEAD_DIM] = (y * lax.rsqrt(var + EPS)).astype(o_ref.dtype)


def _proj_normrope(xn, w, cw, sw, *, rope, name):
    B, L, D = xn.shape
    N = w.shape[1]
    nheads = N // HEAD_DIM
    tm = min(L, 512)
    return pl.pallas_call(
        functools.partial(_proj_normrope_kernel, nheads=nheads, rope=rope),
        grid=(B, L // tm),
        in_specs=[
            pl.BlockSpec((1, tm, D), lambda b, i: (b, i, 0)),
            pl.BlockSpec((D, N), lambda b, i: (0, 0)),
            pl.BlockSpec((tm, HEAD_DIM), lambda b, i: (i, 0)),
            pl.BlockSpec((tm, HEAD_DIM), lambda b, i: (i, 0)),
        ],
        out_specs=pl.BlockSpec((1, tm, N), lambda b, i: (b, i, 0)),
        out_shape=jax.ShapeDtypeStruct((B, L, N), BF16),
        compiler_params=_cparams("parallel", "parallel"),
        name=name,
    )(xn, w, cw, sw)


def _fused_proj_kernel(*refs, plan):
    x_ref, mod_ref, nw_ref = refs[:3]
    n_inputs = sum(n for _, n, _ in plan)
    in_refs = refs[3:3 + n_inputs]
    out_refs = refs[3 + n_inputs:]
    x = x_ref[0]
    var = jnp.mean(x * x, axis=-1, keepdims=True)
    y = x * lax.rsqrt(var + EPS) * nw_ref[...]
    xn = (y * (1.0 + mod_ref[0, 1:2, :]) + mod_ref[0, 0:1, :]).astype(BF16)
    tm = xn.shape[0]
    C = MLSTM_CHUNK
    pos = 0
    for (kind, n, prm), o_ref in zip(plan, out_refs):
        ins = in_refs[pos:pos + n]
        pos += n
        if kind == "plain":
            acc = _dot(xn, ins[0][...])
            if prm["act"] == "sigmoid":
                acc = jax.nn.sigmoid(acc)
            if prm["scale"] != 1.0:
                acc = acc * prm["scale"]
            o_ref[0] = acc.astype(o_ref.dtype)
        elif kind == "kt":
            acc = _dot_nt(ins[0][...], xn)
            for c in range(tm // C):
                o_ref[0, c] = acc[:, c * C:(c + 1) * C].astype(o_ref.dtype)
        elif kind == "gates":
            acc = _dot_nt(ins[0][...], xn) + ins[1][...]
            for h in range(M_HEADS):
                for c in range(tm // C):
                    o_ref[0, h, c] = acc[h * GATE_ROWS:(h + 1) * GATE_ROWS, c * C:(c + 1) * C]
        else:
            acc = _dot(xn, ins[0][...])
            cw = ins[1][...]
            avg = jnp.full((2 * HEAD_DIM, HEAD_DIM), 1.0 / HEAD_DIM, BF16)
            if prm["rope"]:
                sw = ins[2][...]
                src = lax.broadcasted_iota(jnp.int32, (2 * HEAD_DIM, HEAD_DIM), 0) % HEAD_DIM
                dst = lax.broadcasted_iota(jnp.int32, (2 * HEAD_DIM, HEAD_DIM), 1)
                perm = jnp.where(src == _rope_partner(dst), 1.0, 0.0).astype(BF16)
            for h in range(acc.shape[1] // HEAD_DIM):
                a = acc[:, h * HEAD_DIM:(h + 1) * HEAD_DIM]
                hvar = _dot(_split2_lanes(a * a), avg)
                yh = a * cw
                if prm["rope"]:
                    yh = yh + _dot(_split2_lanes(a), perm) * sw
                o_ref[0, :, h * HEAD_DIM:(h + 1) * HEAD_DIM] = (yh * lax.rsqrt(hvar + EPS)).astype(o_ref.dtype)


def _fused_proj(x, mod, nw, mod_row, items, name):
    B, L, D = x.shape
    tm = min(L, 512)
    C = MLSTM_CHUNK
    full2 = lambda b, i: (0, 0)
    in_specs = [
        pl.BlockSpec((1, tm, D), lambda b, i: (b, i, 0)),
        pl.BlockSpec((1, 6, D), lambda b, i: (mod_row(b), 0, 0)),
        pl.BlockSpec((1, D), full2),
    ]
    args = [x, mod, nw.reshape(1, D)]
    out_specs, out_shape, plan = [], [], []
    for kind, arrays, prm in items:
        plan.append((kind, len(arrays), prm))
        args.extend(arrays)
        w = arrays[0]
        if kind in ("plain", "normrope"):
            N = w.shape[1]
            in_specs.append(pl.BlockSpec((D, N), full2))
            for t in arrays[1:]:
                in_specs.append(pl.BlockSpec((tm, HEAD_DIM), lambda b, i: (i, 0)))
            out_specs.append(pl.BlockSpec((1, tm, N), lambda b, i: (b, i, 0)))
            out_shape.append(jax.ShapeDtypeStruct((B, L, N), BF16))
        elif kind == "kt":
            R = w.shape[0]
            in_specs.append(pl.BlockSpec((R, D), full2))
            out_specs.append(pl.BlockSpec((1, tm // C, R, C), lambda b, i: (b, i, 0, 0)))
            out_shape.append(jax.ShapeDtypeStruct((B, L // C, R, C), BF16))
        else:
            R = w.shape[0]
            in_specs.append(pl.BlockSpec((R, D), full2))
            in_specs.append(pl.BlockSpec((R, 1), full2))
            out_specs.append(pl.BlockSpec((1, M_HEADS, tm // C, GATE_ROWS, C), lambda b, i: (b, 0, i, 0, 0)))
            out_shape.append(jax.ShapeDtypeStruct((B, M_HEADS, L // C, GATE_ROWS, C), F32))
    return pl.pallas_call(
        functools.partial(_fused_proj_kernel, plan=tuple(plan)),
        grid=(B, L // tm),
        in_specs=in_specs,
        out_specs=out_specs,
        out_shape=out_shape,
        compiler_params=_cparams("parallel", "parallel"),
        name=name,
    )(*args)


def _rope_tables(L):
    rows = L // GRID_W
    row = jnp.repeat(jnp.arange(rows), GRID_W).astype(F32)
    col = jnp.tile(jnp.arange(GRID_W), rows).astype(F32)
    half = HEAD_DIM // 2
    inv = ROPE_THETA ** (-jnp.arange(0, half, 2, dtype=F32) / half)
    ang_r = row[:, None] * inv[None, :]
    ang_c = col[:, None] * inv[None, :]
    cr, sr, cc, sc = jnp.cos(ang_r), jnp.sin(ang_r), jnp.cos(ang_c), jnp.sin(ang_c)
    return jnp.concatenate([cr, cr, cc, cc], axis=-1), jnp.concatenate([-sr, sr, -sc, sc], axis=-1)


def _log_sigmoid(x):
    return jnp.minimum(x, 0.0) - jnp.log1p(jnp.exp(-jnp.abs(x)))


def _tri_masks(n):
    row_i = lax.broadcasted_iota(jnp.int32, (n, n), 0)
    col_i = lax.broadcasted_iota(jnp.int32, (n, n), 1)
    return row_i <= col_i, row_i >= col_i


def _gate_prep_kernel(g_ref, a_ref):
    _, H, nc, R8, C = g_ref.shape
    rows = H * nc * R8
    g = g_ref[0].reshape(rows, C)
    le, ge = _tri_masks(C)
    upper = jnp.where(le, 1.0, 0.0).astype(BF16)
    lower = jnp.where(ge, 1.0, 0.0).astype(BF16)
    lf = _log_sigmoid(g)
    parts = _split3(lf)
    pre = _dot(parts[0], upper) + _dot(parts[1], upper) + _dot(parts[2], upper)
    suf = _dot(parts[0], lower) + _dot(parts[1], lower) + _dot(parts[2], lower)
    rtype = lax.broadcasted_iota(jnp.int32, (rows, C), 0) % R8
    a = jnp.where(rtype == 1, pre, jnp.where(rtype == 3, suf, jnp.where(rtype >= 4, lf, g)))
    a_ref[0] = a.reshape(H, nc, R8, C)


def _gate_prep(g, name):
    B, H, nc, R8, C = g.shape
    blk = pl.BlockSpec((1, H, nc, R8, C), lambda b: (b, 0, 0, 0, 0))
    return pl.pallas_call(
        _gate_prep_kernel,
        grid=(B,),
        in_specs=[blk],
        out_specs=blk,
        out_shape=jax.ShapeDtypeStruct(g.shape, F32),
        compiler_params=_cparams("parallel"),
        name=name,
    )(g)


def _mlstm_kernel(q_ref, kt_ref, v_ref, a_ref, og_ref, kct_ref, vc_ref, ac_ref, nw_ref, o_ref, cin_ref, min_ref,
                  *, nc, nctx):
    C = MLSTM_CHUNK
    D = HEAD_DIM
    le, ge = _tri_masks(C)
    le_bf = jnp.where(le, 1.0, 0.0).astype(BF16)
    ge_bf = jnp.where(ge, 1.0, 0.0).astype(BF16)
    ones_blk = jnp.ones((C, D), BF16)

    def gates(ref, j, fwd):
        ir, fr = (0, 1) if fwd else (2, 3)
        b_row = ref[0, 0, j, fr:fr + 1, :]
        a_row = ref[0, 0, j, ir:ir + 1, :] - b_row
        return a_row, (b_row[:, C - 1:C] if fwd else b_row[:, 0:1])

    def vaug_of(ref, j):
        return jnp.concatenate([ref[0, j * C:(j + 1) * C, :], ones_blk], axis=1)

    def update(cst, m_prev, kt, vaug, a_row, btot):
        mc = jnp.maximum(jnp.max(a_row, axis=1, keepdims=True), m_prev)
        w_in = jnp.exp(a_row - mc)
        ktw = (kt.astype(F32) * w_in).astype(BF16)
        return jnp.exp(m_prev - mc) * cst + _dot(ktw, vaug), btot + mc

    for d, fwd in ((0, True), (1, False)):
        cst = jnp.zeros((D, 2 * D), F32)
        m = jnp.full((1, 1), M_INIT, F32)
        for j in (range(nctx) if fwd else reversed(range(nctx))):
            a_row, btot = gates(ac_ref, j, fwd)
            cst, m = update(cst, m, kct_ref[0, j], vaug_of(vc_ref, j), a_row, btot)
        order = list(range(nc) if fwd else reversed(range(nc)))
        for idx, j in enumerate(order):
            cin_ref[d, j] = cst.astype(BF16)
            min_ref[d, j] = jnp.broadcast_to(m, min_ref.shape[2:])
            if idx + 1 < nc:
                a_row, btot = gates(a_ref, j, fwd)
                cst, m = update(cst, m, kt_ref[0, j], vaug_of(v_ref, j), a_row, btot)

    def out_step(d, j, fwd, q, kt, vaug):
        a_row, _ = gates(a_ref, j, fwd)
        m_prev = min_ref[d, j][0:1, 0:1]
        nlf = -a_ref[0, 0, j, (4 if fwd else 5):(5 if fwd else 6), :]
        nlf_hi = nlf.astype(BF16)
        nlf_lo = (nlf - nlf_hi.astype(F32)).astype(BF16)
        vis = ge_bf if fwd else le_bf
        nb = _dot_nt(vis, jnp.broadcast_to(nlf_hi, (D, C))) + _dot_nt(vis, jnp.broadcast_to(nlf_lo, (D, C)))
        a_mat = jnp.where(ge if fwd else le, a_row, -jnp.inf)
        m_q = jnp.maximum(jnp.max(a_mat, axis=1, keepdims=True), m_prev)
        bm = jnp.broadcast_to(m_q, (C, D))
        w_intra = jnp.exp(a_mat - bm)
        p = (_dot(q, kt) * w_intra).astype(BF16)
        qc = _dot(q, cin_ref[d, j])
        pv = _dot(p, vaug)
        w_inter = jnp.exp(m_prev - bm)
        num = w_inter * qc[:, :D] + pv[:, :D]
        den = w_inter * qc[:, D:] + pv[:, D:]
        return num / jnp.maximum(jnp.abs(den), jnp.exp(nb - bm))

    nw = nw_ref[...]

    def chunk_out(j, carry):
        sl = pl.ds(pl.multiple_of(j * C, C), C)
        q = q_ref[0, sl, :]
        kt = kt_ref[0, j]
        vaug = jnp.concatenate([v_ref[0, sl, :], ones_blk], axis=1)
        h = out_step(0, j, True, q, kt, vaug) + out_step(1, j, False, q, kt, vaug)
        var = jnp.mean(h * h, axis=-1, keepdims=True)
        y = h * lax.rsqrt(var + EPS) * nw
        o_ref[0, sl, :] = (og_ref[0, sl, :].astype(F32) * y).astype(o_ref.dtype)
        return carry

    lax.fori_loop(0, nc, chunk_out, 0, unroll=MLSTM_OUT_UNROLL)


def _mlstm(q, kt, v, a, og, kct, vc, ac, mh_norm_w):
    B, L, W = q.shape
    C = MLSTM_CHUNK
    assert C == HEAD_DIM
    nc = L // C
    lctx = vc.shape[1]
    nctx = lctx // C
    H = W // HEAD_DIM
    return pl.pallas_call(
        functools.partial(_mlstm_kernel, nc=nc, nctx=nctx),
        grid=(B, H),
        in_specs=[
            pl.BlockSpec((1, L, HEAD_DIM), lambda b, h: (b, 0, h)),
            pl.BlockSpec((1, nc, HEAD_DIM, C), lambda b, h: (b, 0, h, 0)),
            pl.BlockSpec((1, L, HEAD_DIM), lambda b, h: (b, 0, h)),
            pl.BlockSpec((1, 1, nc, GATE_ROWS, C), lambda b, h: (b, h, 0, 0, 0)),
            pl.BlockSpec((1, L, HEAD_DIM), lambda b, h: (b, 0, h)),
            pl.BlockSpec((1, nctx, HEAD_DIM, C), lambda b, h: (b, 0, h, 0)),
            pl.BlockSpec((1, lctx, HEAD_DIM), lambda b, h: (b, 0, h)),
            pl.BlockSpec((1, 1, nctx, GATE_ROWS, C), lambda b, h: (b, h, 0, 0, 0)),
            pl.BlockSpec((1, HEAD_DIM), lambda b, h: (0, h)),
        ],
        out_specs=pl.BlockSpec((1, L, HEAD_DIM), lambda b, h: (b, 0, h)),
        out_shape=jax.ShapeDtypeStruct((B, L, W), BF16),
        scratch_shapes=[
            pltpu.VMEM((2, nc, HEAD_DIM, 2 * HEAD_DIM), BF16),
            pltpu.VMEM((2, nc, 8, HEAD_DIM), F32),
        ],
        compiler_params=_cparams("parallel", "parallel"),
        name="mlstm",
    )(q, kt, v, a, og, kct, vc, ac, mh_norm_w.reshape(1, W))


def _attn_kernel(q_ref, k_ref, v_ref, kc_ref, vc_ref, o_ref):
    ones_l = jnp.ones((v_ref.shape[1], HEAD_DIM), BF16)
    ones_c = jnp.ones((vc_ref.shape[1], HEAD_DIM), BF16)
    for g in range(A_KV_HEADS):
        gs = slice(g * HEAD_DIM, (g + 1) * HEAD_DIM)
        k = k_ref[0, :, gs]
        kc = kc_ref[0, :, gs]
        v = jnp.concatenate([v_ref[0, :, gs], ones_l], axis=1)
        vc = jnp.concatenate([vc_ref[0, :, gs], ones_c], axis=1)
        for h in range(g * A_GROUP, (g + 1) * A_GROUP):
            hs = slice(h * HEAD_DIM, (h + 1) * HEAD_DIM)
            q = q_ref[0, :, hs]
            s1 = _dot_nt(q, k)
            s2 = _dot_nt(q, kc)
            m = jnp.maximum(jnp.max(s1, axis=1, keepdims=True), jnp.max(s2, axis=1, keepdims=True))
            ol = _dot(jnp.exp(s1 - m).astype(BF16), v) + _dot(jnp.exp(s2 - m).astype(BF16), vc)
            o_ref[0, :, hs] = (ol[:, :HEAD_DIM] / ol[:, HEAD_DIM:]).astype(o_ref.dtype)


def _attention(q, k, v, kc, vc):
    B, L, W = q.shape
    lctx = kc.shape[1]
    tq = min(L, 512)
    kvw = A_KV_HEADS * HEAD_DIM
    return pl.pallas_call(
        _attn_kernel,
        grid=(B, L // tq),
        in_specs=[
            pl.BlockSpec((1, tq, W), lambda b, i: (b, i, 0)),
            pl.BlockSpec((1, L, kvw), lambda b, i: (b, 0, 0)),
            pl.BlockSpec((1, L, kvw), lambda b, i: (b, 0, 0)),
            pl.BlockSpec((1, lctx, kvw), lambda b, i: (b, 0, 0)),
            pl.BlockSpec((1, lctx, kvw), lambda b, i: (b, 0, 0)),
        ],
        out_specs=pl.BlockSpec((1, tq, W), lambda b, i: (b, i, 0)),
        out_shape=jax.ShapeDtypeStruct((B, L, W), BF16),
        compiler_params=_cparams("parallel", "parallel"),
        name="attention",
    )(q, k, v, kc, vc)


def _merge_kernel(ym_ref, oa_ref, bg_ref, x_ref, mod_ref, wbm_ref, wba_ref, wo_ref, n2_ref, wr_ref, br_ref,
                  hmid_ref, f_ref, route_ref, w1_ref, w2_ref, cnt_ref, run_ref):
    D = x_ref.shape[-1]
    tm = x_ref.shape[1]

    @pl.when((pl.program_id(0) == 0) & (pl.program_id(1) == 0))
    def _():
        run_ref[...] = jnp.zeros_like(run_ref)

    ym = _dot(ym_ref[0], wbm_ref[...])
    ya = _dot(oa_ref[0], wba_ref[...])
    g0 = bg_ref[0, :, :D].astype(F32)
    g1 = bg_ref[0, :, D:].astype(F32)
    mix = _dot((g0 * ym + g1 * ya).astype(BF16), wo_ref[...])
    hmid = x_ref[0] + mod_ref[0, 2:3, :] * mix
    hmid_ref[0] = hmid
    var = jnp.mean(hmid * hmid, axis=-1, keepdims=True)
    y = hmid * lax.rsqrt(var + EPS) * n2_ref[...]
    f = y * (1.0 + mod_ref[0, 4:5, :]) + mod_ref[0, 3:4, :]
    f_ref[0] = f.astype(f_ref.dtype)

    fp = _split3(f)
    wr = wr_ref[...]
    wr_hi = wr.astype(BF16)
    wr_lo = (wr - wr_hi.astype(F32)).astype(BF16)
    logits = (_dot_nt(wr_hi, fp[0]) + _dot_nt(wr_hi, fp[1]) + _dot_nt(wr_lo, fp[0]) + _dot_nt(wr_hi, fp[2])
              + _dot_nt(wr_lo, fp[1])) + br_ref[...]
    row = lax.broadcasted_iota(jnp.int32, logits.shape, 0)
    big = jnp.int32(ROUTER_ROWS)
    neg = -jnp.inf
    is_grp = row < N_GROUPS
    gl = jnp.where(is_grp, logits, neg)
    gmax = jnp.max(gl, axis=0, keepdims=True)
    gsel = jnp.min(jnp.where(gl == gmax, row, big), axis=0, keepdims=True)
    p_grp = 1.0 / jnp.sum(jnp.where(is_grp, jnp.exp(logits - gmax), 0.0), axis=0, keepdims=True)
    lo = EXPERT_ROW0 + gsel * EXPERTS_PER_GROUP
    in_grp = (row >= lo) & (row < lo + EXPERTS_PER_GROUP)
    el = jnp.where(in_grp, logits, neg)
    v1 = jnp.max(el, axis=0, keepdims=True)
    i1 = jnp.min(jnp.where(el == v1, row, big), axis=0, keepdims=True)
    el2 = jnp.where(row == i1, neg, el)
    v2 = jnp.max(el2, axis=0, keepdims=True)
    i2 = jnp.min(jnp.where(el2 == v2, row, big), axis=0, keepdims=True)
    e21 = jnp.exp(v2 - v1)
    w1 = p_grp / (1.0 + e21)
    w2 = p_grp * e21 / (1.0 + e21)

    sel = jnp.where((row == i1) | (row == i2), 1.0, 0.0)
    tok_u = lax.broadcasted_iota(jnp.int32, (tm, tm), 0)
    tok_n = lax.broadcasted_iota(jnp.int32, (tm, tm), 1)
    earlier = jnp.where(tok_u < tok_n, 1.0, 0.0).astype(BF16)
    rank_all = run_ref[...] + _dot(sel.astype(BF16), earlier)
    rank1 = jnp.sum(jnp.where(row == i1, rank_all, 0.0), axis=0, keepdims=True)
    rank2 = jnp.sum(jnp.where(row == i2, rank_all, 0.0), axis=0, keepdims=True)
    run_ref[...] += jnp.sum(sel, axis=1, keepdims=True)
    cnt_ref[...] = jnp.broadcast_to(run_ref[...], cnt_ref.shape)
    route_ref[...] = jnp.zeros_like(route_ref)
    route_ref[0:1, :] = i1 - EXPERT_ROW0
    route_ref[1:2, :] = i2 - EXPERT_ROW0
    route_ref[8:9, :] = rank1.astype(jnp.int32)
    route_ref[9:10, :] = rank2.astype(jnp.int32)

    eye = jnp.where(tok_u == tok_n, 1.0, 0.0).astype(BF16)
    for w_row, w_ref in ((w1, w1_ref), (w2, w2_ref)):
        w_b = jnp.broadcast_to(w_row, (ROUTER_LANES, tm))
        w_hi = w_b.astype(BF16)
        w_ref[0] = _dot_nt(eye, w_hi) + _dot_nt(eye, (w_b - w_hi.astype(F32)).astype(BF16))


def _merge(ym, oa, bg, x, mod, wbm, wba, wo, n2, wr, br):
    B, L, D = x.shape
    tm = min(L, 512)
    nt = L // tm
    full = lambda b, i: (0, 0)
    tok = lambda b, i: (b, i, 0)
    return pl.pallas_call(
        _merge_kernel,
        grid=(B, L // tm),
        in_specs=[
            pl.BlockSpec((1, tm, D), tok),
            pl.BlockSpec((1, tm, D), tok),
            pl.BlockSpec((1, tm, 2 * D), tok),
            pl.BlockSpec((1, tm, D), tok),
            pl.BlockSpec((1, 6, D), lambda b, i: (b, 0, 0)),
            pl.BlockSpec((D, D), full),
            pl.BlockSpec((D, D), full),
            pl.BlockSpec((D, D), full),
            pl.BlockSpec((1, D), full),
            pl.BlockSpec((ROUTER_ROWS, D), full),
            pl.BlockSpec((ROUTER_ROWS, 1), full),
        ],
        out_specs=[
            pl.BlockSpec((1, tm, D), tok),
            pl.BlockSpec((1, tm, D), tok),
            pl.BlockSpec((ROUTE_ROWS, tm), lambda b, i: (0, b * nt + i)),
            pl.BlockSpec((1, tm, ROUTER_LANES), tok),
            pl.BlockSpec((1, tm, ROUTER_LANES), tok),
            pl.BlockSpec((ROUTER_ROWS, ROUTER_LANES), full),
        ],
        out_shape=[
            jax.ShapeDtypeStruct((B, L, D), F32),
            jax.ShapeDtypeStruct((B, L, D), F32),
            jax.ShapeDtypeStruct((ROUTE_ROWS, B * L), jnp.int32),
            jax.ShapeDtypeStruct((B, L, ROUTER_LANES), F32),
            jax.ShapeDtypeStruct((B, L, ROUTER_LANES), F32),
            jax.ShapeDtypeStruct((ROUTER_ROWS, ROUTER_LANES), F32),
        ],
        scratch_shapes=[pltpu.VMEM((ROUTER_ROWS, 1), F32)],
        compiler_params=_cparams("arbitrary", "arbitrary"),
        name="merge_router",
    )(ym, oa, bg, x, mod, wbm, wba, wo, n2.reshape(1, D), wr, br)


def _load_slots(off_ref, r_ref, s_vmem, s_smem, rsem):
    expert = r_ref[0:SLOT_ROWS, :]
    slot = r_ref[SLOT_ROWS:2 * SLOT_ROWS, :]
    for e in range(1, N_EXPERTS):
        slot = slot + jnp.where(expert == e, off_ref[e], 0)
    s_vmem[...] = slot
    cp = pltpu.make_async_copy(s_vmem, s_smem, rsem)
    cp.start()
    cp.wait()


def _dispatch_kernel(off_ref, nu_ref, r_ref, f_ref, xs_ref, s_vmem, s_smem, zbuf, sem, rsem, *, tm):
    T = EXPERT_TILE

    @pl.when(pl.program_id(0) == 0)
    def _():
        zbuf[...] = jnp.zeros_like(zbuf)

        def zero_copy(row0):
            return pltpu.make_async_copy(zbuf, xs_ref.at[pl.ds(pl.multiple_of(row0, T), T), :], sem)

        def for_each_zero_tile(fn):
            def unused(t, carry):
                fn(zero_copy(t * T))
                return carry

            lax.fori_loop(nu_ref[0], xs_ref.shape[0] // T, unused, 0)
            for e in range(N_EXPERTS):
                @pl.when(off_ref[e + 1] > off_ref[e])
                def _():
                    fn(zero_copy(off_ref[e + 1] - T))

        for_each_zero_tile(lambda cp: cp.start())
        for_each_zero_tile(lambda cp: cp.wait())

    _load_slots(off_ref, r_ref, s_vmem, s_smem, rsem)
    for t in range(tm):
        for k in range(2):
            pltpu.make_async_copy(f_ref.at[pl.ds(t, 1), :], xs_ref.at[pl.ds(s_smem[k, t], 1), :], sem).start(priority=k)
    for _ in range(2):
        pltpu.make_async_copy(f_ref, f_ref, sem).wait()


def _dispatch(off, n_used, route, f, n_rows):
    N, D = f.shape
    tm = min(N, DISPATCH_TILE)
    return pl.pallas_call(
        functools.partial(_dispatch_kernel, tm=tm),
        grid_spec=pltpu.PrefetchScalarGridSpec(
            num_scalar_prefetch=2,
            grid=(N // tm,),
            in_specs=[
                pl.BlockSpec((ROUTE_ROWS, tm), lambda i, off, nu: (0, i)),
                pl.BlockSpec((tm, D), lambda i, off, nu: (i, 0)),
            ],
            out_specs=pl.BlockSpec(memory_space=pl.ANY),
            scratch_shapes=[
                pltpu.VMEM((SLOT_ROWS, tm), jnp.int32),
                pltpu.SMEM((SLOT_ROWS, tm), jnp.int32),
                pltpu.VMEM((EXPERT_TILE, D), F32),
                pltpu.SemaphoreType.DMA(()),
                pltpu.SemaphoreType.DMA(()),
            ],
        ),
        out_shape=jax.ShapeDtypeStruct((n_rows, D), F32),
        compiler_params=_cparams("arbitrary"),
        name="moe_dispatch",
    )(off, n_used, route, f)


def _expert_kernel(te_ref, nu_ref, xs_ref, wg_ref, wu_ref, wd_ref, ys_ref):
    used = pl.program_id(0) < nu_ref[0]

    @pl.when(used)
    def _():
        x = xs_ref[...].astype(BF16)
        hidden = (_silu(_dot(x, wg_ref[0])) * _dot(x, wu_ref[0])).astype(BF16)
        ys_ref[...] = _dot(hidden, wd_ref[0])

    @pl.when(jnp.logical_not(used))
    def _():
        ys_ref[...] = jnp.zeros_like(ys_ref)


def _experts(tile_expert, n_used, xs, wg, wu, wd):
    P, D = xs.shape
    E, _, DE = wg.shape
    T = EXPERT_TILE
    row_map = lambda t, te, nu: (t, 0)
    w_map = lambda t, te, nu: (te[t], 0, 0)
    return pl.pallas_call(
        _expert_kernel,
        grid_spec=pltpu.PrefetchScalarGridSpec(
            num_scalar_prefetch=2,
            grid=(P // T,),
            in_specs=[
                pl.BlockSpec((T, D), row_map),
                pl.BlockSpec((1, D, DE), w_map),
                pl.BlockSpec((1, D, DE), w_map),
                pl.BlockSpec((1, DE, D), w_map),
            ],
            out_specs=pl.BlockSpec((T, D), row_map),
        ),
        out_shape=jax.ShapeDtypeStruct((P, D), F32),
        compiler_params=_cparams("arbitrary"),
        name="moe_experts",
    )(tile_expert, n_used, xs, wg, wu, wd)


def _combine_kernel(off_ref, r_ref, hmid_ref, mod_ref, w1_ref, w2_ref, ys_ref, o_ref, s_vmem, s_smem, y1, y2, sem,
                    rsem, *, tm):
    _load_slots(off_ref, r_ref, s_vmem, s_smem, rsem)
    half = tm // 2
    for t in range(tm):
        for k, buf in ((0, y1), (1, y2)):
            pltpu.make_async_copy(ys_ref.at[pl.ds(s_smem[k, t], 1), :], buf.at[pl.ds(t, 1), :],
                                  sem.at[t // half]).start(priority=k)
    D = o_ref.shape[-1]
    for hf in range(2):
        rows = pl.ds(hf * half, half)
        for buf in (y1, y2):
            pltpu.make_async_copy(buf.at[rows, :], buf.at[rows, :], sem.at[hf]).wait()
        w1 = w1_ref[0, rows, :]
        w2 = w2_ref[0, rows, :]
        for c in range(D // ROUTER_LANES):
            sl = slice(c * ROUTER_LANES, (c + 1) * ROUTER_LANES)
            moe = w1 * y1[rows, sl] + w2 * y2[rows, sl]
            o_ref[0, rows, sl] = hmid_ref[0, rows, sl] + mod_ref[0, 5:6, sl] * moe


def _combine(off, route, hmid, mod, w1, w2, ys):
    B, L, D = hmid.shape
    tm = min(L, MOE_TOKEN_TILE)
    nt = L // tm
    tok = lambda b, i, off: (b, i, 0)
    return pl.pallas_call(
        functools.partial(_combine_kernel, tm=tm),
        grid_spec=pltpu.PrefetchScalarGridSpec(
            num_scalar_prefetch=1,
            grid=(B, nt),
            in_specs=[
                pl.BlockSpec((ROUTE_ROWS, tm), lambda b, i, off: (0, b * nt + i)),
                pl.BlockSpec((1, tm, D), tok),
                pl.BlockSpec((1, 6, D), lambda b, i, off: (b, 0, 0)),
                pl.BlockSpec((1, tm, ROUTER_LANES), tok),
                pl.BlockSpec((1, tm, ROUTER_LANES), tok),
                pl.BlockSpec(memory_space=pl.ANY),
            ],
            out_specs=pl.BlockSpec((1, tm, D), tok),
            scratch_shapes=[
                pltpu.VMEM((SLOT_ROWS, tm), jnp.int32),
                pltpu.SMEM((SLOT_ROWS, tm), jnp.int32),
                pltpu.VMEM((tm, D), F32),
                pltpu.VMEM((tm, D), F32),
                pltpu.SemaphoreType.DMA((2,)),
                pltpu.SemaphoreType.DMA(()),
            ],
        ),
        out_shape=jax.ShapeDtypeStruct((B, L, D), F32),
        compiler_params=_cparams("arbitrary", "arbitrary"),
        name="moe_combine",
    )(off, route, hmid, mod, w1, w2, ys)


def _expert_layout(cnt, n_pairs):
    T = EXPERT_TILE
    counts = cnt[EXPERT_ROW0:, 0].astype(jnp.int32)
    ends = jnp.cumsum((counts + T - 1) // T).astype(jnp.int32)
    off = jnp.concatenate([jnp.zeros((1,), jnp.int32), ends * T])
    n_tiles = n_pairs // T + N_EXPERTS
    t_idx = jnp.arange(n_tiles, dtype=jnp.int32)
    tile_expert = jnp.minimum(jnp.sum(t_idx[:, None] >= ends[None, :], axis=1), N_EXPERTS - 1).astype(jnp.int32)
    n_used = ends[-1]
    tile_expert = jnp.where(t_idx < n_used, tile_expert, jnp.take(tile_expert, n_used - 1))
    return off, tile_expert, n_used.reshape(1), n_tiles * T


def _gate_weights(w_gate, b_gate):
    D = w_gate.shape[0]
    wt = w_gate.T.reshape(4, M_HEADS, D).transpose(1, 0, 2)
    wt = jnp.concatenate([wt, wt[:, 1:2], wt[:, 3:4], jnp.zeros((M_HEADS, GATE_ROWS - 6, D), wt.dtype)], axis=1)
    bt = b_gate.reshape(4, M_HEADS).T
    bt = jnp.concatenate([bt, bt[:, 1:2], bt[:, 3:4], jnp.zeros((M_HEADS, GATE_ROWS - 6), bt.dtype)], axis=1)
    return wt.reshape(M_HEADS * GATE_ROWS, D), bt.reshape(M_HEADS * GATE_ROWS)


def kernel(x, c, ctx, c_ctx, w_ada, b_ada, norm1_w, w_in, b_mgate, q_norm_w, k_norm_w, mh_norm_w, w_branch_m,
           w_branch_a, w_out, norm2_w, w_rg, b_rg, w_re, b_re, w_e_gate, w_e_up, w_e_down):
    B, L, D = x.shape
    depth = w_ada.shape[0]
    assert depth == 1, "context-stream update between layers is not implemented"
    l = 0
    mw = M_HEADS * HEAD_DIM
    aq = A_HEADS * HEAD_DIM
    akv = A_KV_HEADS * HEAD_DIM
    o_mq, o_mk, o_mv, o_og = 0, mw, 2 * mw, 3 * mw
    o_g = 4 * mw
    o_aq = o_g + 4 * M_HEADS
    o_ak = o_aq + aq
    o_av = o_ak + akv
    o_bg = o_av + akv
    scale = HEAD_DIM ** -0.5

    rows = ((B + 1 + 7) // 8) * 8
    cc = jnp.concatenate([c, c_ctx[None, :], jnp.zeros((rows - B - 1, D), F32)], axis=0)
    mod = _ada_mod(cc, w_ada[l], b_ada[l]).reshape(rows, 6, D)

    w = w_in[l]
    wb = lambda a, b: w[:, a:b].astype(BF16)
    cos, sin = _rope_tables(L)
    wgt, bgt = _gate_weights(w[:, o_g:o_aq], b_mgate[l])
    wgt, bgt = wgt.astype(BF16), bgt.reshape(-1, 1)
    wkt = w[:, o_mk:o_mv].T.astype(BF16)
    partner = _rope_partner(jnp.arange(HEAD_DIM))
    qw, kw = q_norm_w[l] * scale, k_norm_w[l]
    plain = lambda a, b, act="none", sc=1.0: ("plain", [wb(a, b)], dict(act=act, scale=sc))
    item_mkt = ("kt", [wkt], {})
    item_gates = ("gates", [wgt, bgt], {})

    mq, mkt, mv, og, gt = _fused_proj(
        x, mod, norm1_w[l], lambda b: b,
        [plain(o_mq, o_mk, sc=scale), item_mkt, plain(o_mv, o_og), plain(o_og, o_g, "sigmoid"), item_gates],
        "proj_mlstm")
    qa, ka, va, bg = _fused_proj(
        x, mod, norm1_w[l], lambda b: b,
        [("normrope", [wb(o_aq, o_ak), cos * qw, sin * qw[partner]], dict(rope=True)),
         ("normrope", [wb(o_ak, o_av), cos * kw, sin * kw[partner]], dict(rope=True)),
         plain(o_av, o_bg), plain(o_bg, o_bg + 2 * D, "sigmoid")],
        "proj_attn")

    lctx = ctx.shape[1]
    cmkt, cmv, cgt, cka, cva = _fused_proj(
        ctx, mod, norm1_w[l], lambda b: B,
        [item_mkt, plain(o_mv, o_og), item_gates,
         ("normrope", [wb(o_ak, o_av), jnp.broadcast_to(kw, (lctx, HEAD_DIM))], dict(rope=False)),
         plain(o_av, o_bg)],
        "proj_ctx")

    ga = _gate_prep(gt, "gate_prep")
    gca = _gate_prep(cgt, "gate_prep_ctx")
    ym = _mlstm(mq, mkt, mv, ga, og, cmkt, cmv, gca, mh_norm_w[l])
    oa = _attention(qa, ka, va, cka, cva)

    pad_w = jnp.zeros((EXPERT_ROW0 - N_GROUPS, D), F32)
    pad_b = jnp.zeros((EXPERT_ROW0 - N_GROUPS,), F32)
    wr = jnp.concatenate([w_rg[l].T, pad_w, w_re[l].T], axis=0)
    br = jnp.concatenate([b_rg[l], pad_b, b_re[l]]).reshape(ROUTER_ROWS, 1)
    hmid, f, route, w1, w2, cnt = _merge(ym, oa, bg, x, mod, w_branch_m[l].astype(BF16), w_branch_a[l].astype(BF16),
                                         w_out[l].astype(BF16), norm2_w[l], wr, br)

    off, tile_expert, n_used, n_rows = _expert_layout(cnt, 2 * B * L)
    xs = _dispatch(off, n_used, route, f.reshape(B * L, D), n_rows)
    ys = _experts(tile_expert, n_used, xs, w_e_gate[l].astype(BF16), w_e_up[l].astype(BF16),
                  w_e_down[l].astype(BF16))
    return _combine(off, route, hmid, mod, w1, w2, ys)
```

```python
import functools

import jax
import jax.numpy as jnp
from jax import lax
from jax.experimental import pallas as pl
from jax.experimental.pallas import tpu as pltpu

F32 = jnp.float32
BF16 = jnp.bfloat16

EPS = 1e-6
GRID_W = 64
ROPE_THETA = 10000.0
HEAD_DIM = 128
M_HEADS = 8
A_HEADS = 8
A_KV_HEADS = 2
A_GROUP = A_HEADS // A_KV_HEADS
N_GROUPS = 4
EXPERTS_PER_GROUP = 8
N_EXPERTS = N_GROUPS * EXPERTS_PER_GROUP
M_INIT = -1e30
MLSTM_CHUNK = 128
MLSTM_OUT_UNROLL = 16
GATE_ROWS = 8
ROUTER_LANES = 128
EXPERT_ROW0 = 8
ROUTER_ROWS = EXPERT_ROW0 + N_EXPERTS
EXPERT_TILE = 512
MOE_TOKEN_TILE = 1024
DISPATCH_TILE = 2048
ROUTE_ROWS = 16
SLOT_ROWS = 8
VMEM_LIMIT_BYTES = 48 * 1024 * 1024

NT_DIMS = (((1,), (1,)), ((), ()))


def _cparams(*sem, flags=None):
    return pltpu.CompilerParams(dimension_semantics=sem, vmem_limit_bytes=VMEM_LIMIT_BYTES, flags=flags)


def _dot(a, b):
    return jnp.dot(a, b, preferred_element_type=F32)


def _dot_nt(a, b):
    return lax.dot_general(a, b, NT_DIMS, preferred_element_type=F32)


def _split3(x):
    hi = x.astype(BF16)
    r1 = x - hi.astype(F32)
    mid = r1.astype(BF16)
    lo = (r1 - mid.astype(F32)).astype(BF16)
    return hi, mid, lo


def _silu(x):
    return x * jax.nn.sigmoid(x)


def _ada_kernel(c_ref, w_ref, b_ref, o_ref):
    s = _silu(c_ref[...])
    parts = _split3(s)
    w = w_ref[...]
    w_hi = w.astype(BF16)
    w_lo = (w - w_hi.astype(F32)).astype(BF16)
    acc = _dot(parts[0], w_hi) + _dot(parts[1], w_hi) + _dot(parts[0], w_lo)
    o_ref[...] = acc + b_ref[...]


def _ada_mod(cc, w, b):
    rows, d = cc.shape
    n = w.shape[1]
    tn = min(n, 1536)
    return pl.pallas_call(
        _ada_kernel,
        grid=(n // tn,),
        in_specs=[
            pl.BlockSpec((rows, d), lambda j: (0, 0)),
            pl.BlockSpec((d, tn), lambda j: (0, j)),
            pl.BlockSpec((1, tn), lambda j: (0, j)),
        ],
        out_specs=pl.BlockSpec((rows, tn), lambda j: (0, j)),
        out_shape=jax.ShapeDtypeStruct((rows, n), F32),
        compiler_params=_cparams("parallel"),
        name="ada_mod",
    )(cc, w, b.reshape(1, n))


def _norm_mod_kernel(x_ref, mod_ref, nw_ref, o_ref):
    x = x_ref[0]
    var = jnp.mean(x * x, axis=-1, keepdims=True)
    y = x * lax.rsqrt(var + EPS) * nw_ref[...]
    sh = mod_ref[0, 0:1, :]
    sc = mod_ref[0, 1:2, :]
    o_ref[0] = (y * (1.0 + sc) + sh).astype(o_ref.dtype)


def _norm_mod(x, mod, nw, mod_row):
    B, L, D = x.shape
    tm = min(L, 512)
    return pl.pallas_call(
        _norm_mod_kernel,
        grid=(B, L // tm),
        in_specs=[
            pl.BlockSpec((1, tm, D), lambda b, i: (b, i, 0)),
            pl.BlockSpec((1, 6, D), lambda b, i: (mod_row(b), 0, 0)),
            pl.BlockSpec((1, D), lambda b, i: (0, 0)),
        ],
        out_specs=pl.BlockSpec((1, tm, D), lambda b, i: (b, i, 0)),
        out_shape=jax.ShapeDtypeStruct((B, L, D), BF16),
        compiler_params=_cparams("parallel", "parallel"),
        name="norm_mod",
    )(x, mod, nw.reshape(1, D))


def _proj_kernel(x_ref, w_ref, o_ref, *, act, scale):
    acc = _dot(x_ref[0], w_ref[...])
    if act == "sigmoid":
        acc = jax.nn.sigmoid(acc)
    if scale != 1.0:
        acc = acc * scale
    o_ref[0] = acc.astype(o_ref.dtype)


def _proj(xn, w, *, act="none", scale=1.0, name):
    B, L, D = xn.shape
    N = w.shape[1]
    tm = min(L, 512)
    tn = min(N, 1024)
    return pl.pallas_call(
        functools.partial(_proj_kernel, act=act, scale=scale),
        grid=(B, N // tn, L // tm),
        in_specs=[
            pl.BlockSpec((1, tm, D), lambda b, j, i: (b, i, 0)),
            pl.BlockSpec((D, tn), lambda b, j, i: (0, j)),
        ],
        out_specs=pl.BlockSpec((1, tm, tn), lambda b, j, i: (b, i, j)),
        out_shape=jax.ShapeDtypeStruct((B, L, N), BF16),
        compiler_params=_cparams("parallel", "parallel", "parallel"),
        name=name,
    )(xn, w)


def _proj_t_kernel(w_ref, x_ref, o_ref, *, nchunk, chunk):
    acc = _dot_nt(w_ref[...], x_ref[0])
    for c in range(nchunk):
        o_ref[0, c] = acc[:, c * chunk:(c + 1) * chunk].astype(o_ref.dtype)


def _proj_t(xn, wt, *, chunk, name):
    B, L, D = xn.shape
    R = wt.shape[0]
    tm = min(L, 512)
    nchunk = tm // chunk
    return pl.pallas_call(
        functools.partial(_proj_t_kernel, nchunk=nchunk, chunk=chunk),
        grid=(B, L // tm),
        in_specs=[
            pl.BlockSpec((R, D), lambda b, i: (0, 0)),
            pl.BlockSpec((1, tm, D), lambda b, i: (b, i, 0)),
        ],
        out_specs=pl.BlockSpec((1, nchunk, R, chunk), lambda b, i: (b, i, 0, 0)),
        out_shape=jax.ShapeDtypeStruct((B, L // chunk, R, chunk), BF16),
        compiler_params=_cparams("parallel", "parallel"),
        name=name,
    )(wt, xn)


def _proj_gates_kernel(w_ref, x_ref, b_ref, o_ref, *, nchunk, chunk):
    acc = _dot_nt(w_ref[...], x_ref[0]) + b_ref[...]
    for h in range(M_HEADS):
        for c in range(nchunk):
            o_ref[0, h, c] = acc[h * GATE_ROWS:(h + 1) * GATE_ROWS, c * chunk:(c + 1) * chunk]


def _proj_gates(xn, wt, bias, *, chunk, name):
    B, L, D = xn.shape
    R = wt.shape[0]
    tm = min(L, 512)
    nchunk = tm // chunk
    return pl.pallas_call(
        functools.partial(_proj_gates_kernel, nchunk=nchunk, chunk=chunk),
        grid=(B, L // tm),
        in_specs=[
            pl.BlockSpec((R, D), lambda b, i: (0, 0)),
            pl.BlockSpec((1, tm, D), lambda b, i: (b, i, 0)),
            pl.BlockSpec((R, 1), lambda b, i: (0, 0)),
        ],
        out_specs=pl.BlockSpec((1, M_HEADS, nchunk, GATE_ROWS, chunk), lambda b, i: (b, 0, i, 0, 0)),
        out_shape=jax.ShapeDtypeStruct((B, M_HEADS, L // chunk, GATE_ROWS, chunk), F32),
        compiler_params=_cparams("parallel", "parallel"),
        name=name,
    )(wt, xn, bias.reshape(R, 1))


def _rope_partner(j):
    quarter = HEAD_DIM // 4
    return jnp.where((j % (2 * quarter)) < quarter, j + quarter, j - quarter)


def _split2_lanes(x):
    hi = x.astype(BF16)
    return jnp.concatenate([hi, (x - hi.astype(F32)).astype(BF16)], axis=1)


def _proj_normrope_kernel(x_ref, w_ref, cw_ref, sw_ref, o_ref, *, nheads, rope):
    acc = _dot(x_ref[0], w_ref[...])
    cw = cw_ref[...]
    avg = jnp.full((2 * HEAD_DIM, HEAD_DIM), 1.0 / HEAD_DIM, BF16)
    if rope:
        sw = sw_ref[...]
        src = lax.broadcasted_iota(jnp.int32, (2 * HEAD_DIM, HEAD_DIM), 0) % HEAD_DIM
        dst = lax.broadcasted_iota(jnp.int32, (2 * HEAD_DIM, HEAD_DIM), 1)
        perm = jnp.where(src == _rope_partner(dst), 1.0, 0.0).astype(BF16)
    for h in range(nheads):
        a = acc[:, h * HEAD_DIM:(h + 1) * HEAD_DIM]
        var = _dot(_split2_lanes(a * a), avg)
        y = a * cw
        if rope:
            y = y + _dot(_split2_lanes(a), perm) * sw
        o_ref[0, :, h * HEAD_DIM:(h + 1) * HEAD_DIM] = (y * lax.rsqrt(var + EPS)).astype(o_ref.dtype)


def _proj_normrope(xn, w, cw, sw, *, rope, name):
    B, L, D = xn.shape
    N = w.shape[1]
    nheads = N // HEAD_DIM
    tm = min(L, 512)
    return pl.pallas_call(
        functools.partial(_proj_normrope_kernel, nheads=nheads, rope=rope),
        grid=(B, L // tm),
        in_specs=[
            pl.BlockSpec((1, tm, D), lambda b, i: (b, i, 0)),
            pl.BlockSpec((D, N), lambda b, i: (0, 0)),
            pl.BlockSpec((tm, HEAD_DIM), lambda b, i: (i, 0)),
            pl.BlockSpec((tm, HEAD_DIM), lambda b, i: (i, 0)),
        ],
        out_specs=pl.BlockSpec((1, tm, N), lambda b, i: (b, i, 0)),
        out_shape=jax.ShapeDtypeStruct((B, L, N), BF16),
        compiler_params=_cparams("parallel", "parallel"),
        name=name,
    )(xn, w, cw, sw)


def _fused_proj_kernel(*refs, plan):
    x_ref, mod_ref, nw_ref = refs[:3]
    n_inputs = sum(n for _, n, _ in plan)
    in_refs = refs[3:3 + n_inputs]
    out_refs = refs[3 + n_inputs:]
    x = x_ref[0]
    var = jnp.mean(x * x, axis=-1, keepdims=True)
    y = x * lax.rsqrt(var + EPS) * nw_ref[...]
    xn = (y * (1.0 + mod_ref[0, 1:2, :]) + mod_ref[0, 0:1, :]).astype(BF16)
    tm = xn.shape[0]
    C = MLSTM_CHUNK
    pos = 0
    for (kind, n, prm), o_ref in zip(plan, out_refs):
        ins = in_refs[pos:pos + n]
        pos += n
        if kind == "plain":
            acc = _dot(xn, ins[0][...])
            if prm["act"] == "sigmoid":
                acc = jax.nn.sigmoid(acc)
            if prm["scale"] != 1.0:
                acc = acc * prm["scale"]
            o_ref[0] = acc.astype(o_ref.dtype)
        elif kind == "kt":
            acc = _dot_nt(ins[0][...], xn)
            for c in range(tm // C):
                o_ref[0, c] = acc[:, c * C:(c + 1) * C].astype(o_ref.dtype)
        elif kind == "gates":
            acc = _dot_nt(ins[0][...], xn) + ins[1][...]
            for h in range(M_HEADS):
                for c in range(tm // C):
                    o_ref[0, h, c] = acc[h * GATE_ROWS:(h + 1) * GATE_ROWS, c * C:(c + 1) * C]
        else:
            acc = _dot(xn, ins[0][...])
            cw = ins[1][...]
            avg = jnp.full((2 * HEAD_DIM, HEAD_DIM), 1.0 / HEAD_DIM, BF16)
            if prm["rope"]:
                sw = ins[2][...]
                src = lax.broadcasted_iota(jnp.int32, (2 * HEAD_DIM, HEAD_DIM), 0) % HEAD_DIM
                dst = lax.broadcasted_iota(jnp.int32, (2 * HEAD_DIM, HEAD_DIM), 1)
                perm = jnp.where(src == _rope_partner(dst), 1.0, 0.0).astype(BF16)
            for h in range(acc.shape[1] // HEAD_DIM):
                a = acc[:, h * HEAD_DIM:(h + 1) * HEAD_DIM]
                hvar = jnp.mean(a * a, axis=-1, keepdims=True)
                yh = a * cw
                if prm["rope"]:
                    yh = yh + _dot(_split2_lanes(a), perm) * sw
                o_ref[0, :, h * HEAD_DIM:(h + 1) * HEAD_DIM] = (yh * lax.rsqrt(hvar + EPS)).astype(o_ref.dtype)


def _fused_proj(x, mod, nw, mod_row, items, name):
    B, L, D = x.shape
    tm = min(L, 512)
    C = MLSTM_CHUNK
    full2 = lambda b, i: (0, 0)
    in_specs = [
        pl.BlockSpec((1, tm, D), lambda b, i: (b, i, 0)),
        pl.BlockSpec((1, 6, D), lambda b, i: (mod_row(b), 0, 0)),
        pl.BlockSpec((1, D), full2),
    ]
    args = [x, mod, nw.reshape(1, D)]
    out_specs, out_shape, plan = [], [], []
    for kind, arrays, prm in items:
        plan.append((kind, len(arrays), prm))
        args.extend(arrays)
        w = arrays[0]
        if kind in ("plain", "normrope"):
            N = w.shape[1]
            in_specs.append(pl.BlockSpec((D, N), full2))
            for t in arrays[1:]:
                in_specs.append(pl.BlockSpec((tm, HEAD_DIM), lambda b, i: (i, 0)))
            out_specs.append(pl.BlockSpec((1, tm, N), lambda b, i: (b, i, 0)))
            out_shape.append(jax.ShapeDtypeStruct((B, L, N), BF16))
        elif kind == "kt":
            R = w.shape[0]
            in_specs.append(pl.BlockSpec((R, D), full2))
            out_specs.append(pl.BlockSpec((1, tm // C, R, C), lambda b, i: (b, i, 0, 0)))
            out_shape.append(jax.ShapeDtypeStruct((B, L // C, R, C), BF16))
        else:
            R = w.shape[0]
            in_specs.append(pl.BlockSpec((R, D), full2))
            in_specs.append(pl.BlockSpec((R, 1), full2))
            out_specs.append(pl.BlockSpec((1, M_HEADS, tm // C, GATE_ROWS, C), lambda b, i: (b, 0, i, 0, 0)))
            out_shape.append(jax.ShapeDtypeStruct((B, M_HEADS, L // C, GATE_ROWS, C), F32))
    return pl.pallas_call(
        functools.partial(_fused_proj_kernel, plan=tuple(plan)),
        grid=(B, L // tm),
        in_specs=in_specs,
        out_specs=out_specs,
        out_shape=out_shape,
        compiler_params=_cparams("parallel", "parallel"),
        name=name,
    )(*args)


def _rope_tables(L):
    rows = L // GRID_W
    row = jnp.repeat(jnp.arange(rows), GRID_W).astype(F32)
    col = jnp.tile(jnp.arange(GRID_W), rows).astype(F32)
    half = HEAD_DIM // 2
    inv = ROPE_THETA ** (-jnp.arange(0, half, 2, dtype=F32) / half)
    ang_r = row[:, None] * inv[None, :]
    ang_c = col[:, None] * inv[None, :]
    cr, sr, cc, sc = jnp.cos(ang_r), jnp.sin(ang_r), jnp.cos(ang_c), jnp.sin(ang_c)
    return jnp.concatenate([cr, cr, cc, cc], axis=-1), jnp.concatenate([-sr, sr, -sc, sc], axis=-1)


def _log_sigmoid(x):
    return jnp.minimum(x, 0.0) - jnp.log1p(jnp.exp(-jnp.abs(x)))


def _tri_masks(n):
    row_i = lax.broadcasted_iota(jnp.int32, (n, n), 0)
    col_i = lax.broadcasted_iota(jnp.int32, (n, n), 1)
    return row_i <= col_i, row_i >= col_i


def _gate_prep_kernel(g_ref, a_ref):
    _, H, nc, R8, C = g_ref.shape
    rows = H * nc * R8
    g = g_ref[0].reshape(rows, C)
    le, ge = _tri_masks(C)
    upper = jnp.where(le, 1.0, 0.0).astype(BF16)
    lower = jnp.where(ge, 1.0, 0.0).astype(BF16)
    lf = _log_sigmoid(g)
    parts = _split3(lf)
    pre = _dot(parts[0], upper) + _dot(parts[1], upper) + _dot(parts[2], upper)
    suf = _dot(parts[0], lower) + _dot(parts[1], lower) + _dot(parts[2], lower)
    rtype = lax.broadcasted_iota(jnp.int32, (rows, C), 0) % R8
    a = jnp.where(rtype == 1, pre, jnp.where(rtype == 3, suf, jnp.where(rtype >= 4, lf, g)))
    a_ref[0] = a.reshape(H, nc, R8, C)


def _gate_prep(g, name):
    B, H, nc, R8, C = g.shape
    blk = pl.BlockSpec((1, H, nc, R8, C), lambda b: (b, 0, 0, 0, 0))
    return pl.pallas_call(
        _gate_prep_kernel,
        grid=(B,),
        in_specs=[blk],
        out_specs=blk,
        out_shape=jax.ShapeDtypeStruct(g.shape, F32),
        compiler_params=_cparams("parallel"),
        name=name,
    )(g)


def _mlstm_kernel(q_ref, kt_ref, v_ref, a_ref, og_ref, kct_ref, vc_ref, ac_ref, nw_ref, o_ref, cin_ref, min_ref,
                  *, nc, nctx):
    C = MLSTM_CHUNK
    D = HEAD_DIM
    le, ge = _tri_masks(C)
    le_bf = jnp.where(le, 1.0, 0.0).astype(BF16)
    ge_bf = jnp.where(ge, 1.0, 0.0).astype(BF16)
    ones_blk = jnp.ones((C, D), BF16)

    def gates(ref, j, fwd):
        ir, fr = (0, 1) if fwd else (2, 3)
        b_row = ref[0, 0, j, fr:fr + 1, :]
        a_row = ref[0, 0, j, ir:ir + 1, :] - b_row
        return a_row, (b_row[:, C - 1:C] if fwd else b_row[:, 0:1])

    def vaug_of(ref, j):
        return jnp.concatenate([ref[0, j * C:(j + 1) * C, :], ones_blk], axis=1)

    def update(cst, m_prev, kt, vaug, a_row, btot):
        mc = jnp.maximum(jnp.max(a_row, axis=1, keepdims=True), m_prev)
        w_in = jnp.exp(a_row - mc)
        ktw = (kt.astype(F32) * w_in).astype(BF16)
        return jnp.exp(m_prev - mc) * cst + _dot(ktw, vaug), btot + mc

    for d, fwd in ((0, True), (1, False)):
        cst = jnp.zeros((D, 2 * D), F32)
        m = jnp.full((1, 1), M_INIT, F32)
        for j in (range(nctx) if fwd else reversed(range(nctx))):
            a_row, btot = gates(ac_ref, j, fwd)
            cst, m = update(cst, m, kct_ref[0, j], vaug_of(vc_ref, j), a_row, btot)
        order = list(range(nc) if fwd else reversed(range(nc)))
        for idx, j in enumerate(order):
            cin_ref[d, j] = cst.astype(BF16)
            min_ref[d, j] = jnp.broadcast_to(m, min_ref.shape[2:])
            if idx + 1 < nc:
                a_row, btot = gates(a_ref, j, fwd)
                cst, m = update(cst, m, kt_ref[0, j], vaug_of(v_ref, j), a_row, btot)

    def out_step(d, j, fwd, q, kt, vaug):
        a_row, _ = gates(a_ref, j, fwd)
        m_prev = min_ref[d, j][0:1, 0:1]
        nlf = -a_ref[0, 0, j, (4 if fwd else 5):(5 if fwd else 6), :]
        nlf_hi = nlf.astype(BF16)
        nlf_lo = (nlf - nlf_hi.astype(F32)).astype(BF16)
        vis = ge_bf if fwd else le_bf
        nb = _dot_nt(vis, jnp.broadcast_to(nlf_hi, (D, C))) + _dot_nt(vis, jnp.broadcast_to(nlf_lo, (D, C)))
        a_mat = jnp.where(ge if fwd else le, a_row, -jnp.inf)
        m_q = jnp.maximum(jnp.max(a_mat, axis=1, keepdims=True), m_prev)
        bm = jnp.broadcast_to(m_q, (C, D))
        w_intra = jnp.exp(a_mat - bm)
        p = (_dot(q, kt) * w_intra).astype(BF16)
        qc = _dot(q, cin_ref[d, j])
        pv = _dot(p, vaug)
        w_inter = jnp.exp(m_prev - bm)
        num = w_inter * qc[:, :D] + pv[:, :D]
        den = w_inter * qc[:, D:] + pv[:, D:]
        return num / jnp.maximum(jnp.abs(den), jnp.exp(nb - bm))

    nw = nw_ref[...]

    def chunk_out(j, carry):
        sl = pl.ds(pl.multiple_of(j * C, C), C)
        q = q_ref[0, sl, :]
        kt = kt_ref[0, j]
        vaug = jnp.concatenate([v_ref[0, sl, :], ones_blk], axis=1)
        h = out_step(0, j, True, q, kt, vaug) + out_step(1, j, False, q, kt, vaug)
        var = jnp.mean(h * h, axis=-1, keepdims=True)
        y = h * lax.rsqrt(var + EPS) * nw
        o_ref[0, sl, :] = (og_ref[0, sl, :].astype(F32) * y).astype(o_ref.dtype)
        return carry

    lax.fori_loop(0, nc, chunk_out, 0, unroll=MLSTM_OUT_UNROLL)


def _mlstm(q, kt, v, a, og, kct, vc, ac, mh_norm_w):
    B, L, W = q.shape
    C = MLSTM_CHUNK
    assert C == HEAD_DIM
    nc = L // C
    lctx = vc.shape[1]
    nctx = lctx // C
    H = W // HEAD_DIM
    return pl.pallas_call(
        functools.partial(_mlstm_kernel, nc=nc, nctx=nctx),
        grid=(B, H),
        in_specs=[
            pl.BlockSpec((1, L, HEAD_DIM), lambda b, h: (b, 0, h)),
            pl.BlockSpec((1, nc, HEAD_DIM, C), lambda b, h: (b, 0, h, 0)),
            pl.BlockSpec((1, L, HEAD_DIM), lambda b, h: (b, 0, h)),
            pl.BlockSpec((1, 1, nc, GATE_ROWS, C), lambda b, h: (b, h, 0, 0, 0)),
            pl.BlockSpec((1, L, HEAD_DIM), lambda b, h: (b, 0, h)),
            pl.BlockSpec((1, nctx, HEAD_DIM, C), lambda b, h: (b, 0, h, 0)),
            pl.BlockSpec((1, lctx, HEAD_DIM), lambda b, h: (b, 0, h)),
            pl.BlockSpec((1, 1, nctx, GATE_ROWS, C), lambda b, h: (b, h, 0, 0, 0)),
            pl.BlockSpec((1, HEAD_DIM), lambda b, h: (0, h)),
        ],
        out_specs=pl.BlockSpec((1, L, HEAD_DIM), lambda b, h: (b, 0, h)),
        out_shape=jax.ShapeDtypeStruct((B, L, W), BF16),
        scratch_shapes=[
            pltpu.VMEM((2, nc, HEAD_DIM, 2 * HEAD_DIM), BF16),
            pltpu.VMEM((2, nc, 8, HEAD_DIM), F32),
        ],
        compiler_params=_cparams("parallel", "parallel"),
        name="mlstm",
    )(q, kt, v, a, og, kct, vc, ac, mh_norm_w.reshape(1, W))


def _attn_kernel(q_ref, k_ref, v_ref, kc_ref, vc_ref, o_ref):
    ones_l = jnp.ones((v_ref.shape[1], HEAD_DIM), BF16)
    ones_c = jnp.ones((vc_ref.shape[1], HEAD_DIM), BF16)
    for g in range(A_KV_HEADS):
        gs = slice(g * HEAD_DIM, (g + 1) * HEAD_DIM)
        k = k_ref[0, :, gs]
        kc = kc_ref[0, :, gs]
        v = jnp.concatenate([v_ref[0, :, gs], ones_l], axis=1)
        vc = jnp.concatenate([vc_ref[0, :, gs], ones_c], axis=1)
        for h in range(g * A_GROUP, (g + 1) * A_GROUP):
            hs = slice(h * HEAD_DIM, (h + 1) * HEAD_DIM)
            q = q_ref[0, :, hs]
            s1 = _dot_nt(q, k)
            s2 = _dot_nt(q, kc)
            m = jnp.maximum(jnp.max(s1, axis=1, keepdims=True), jnp.max(s2, axis=1, keepdims=True))
            ol = _dot(jnp.exp(s1 - m).astype(BF16), v) + _dot(jnp.exp(s2 - m).astype(BF16), vc)
            o_ref[0, :, hs] = (ol[:, :HEAD_DIM] / ol[:, HEAD_DIM:]).astype(o_ref.dtype)


def _attention(q, k, v, kc, vc):
    B, L, W = q.shape
    lctx = kc.shape[1]
    tq = min(L, 512)
    kvw = A_KV_HEADS * HEAD_DIM
    return pl.pallas_call(
        _attn_kernel,
        grid=(B, L // tq),
        in_specs=[
            pl.BlockSpec((1, tq, W), lambda b, i: (b, i, 0)),
            pl.BlockSpec((1, L, kvw), lambda b, i: (b, 0, 0)),
            pl.BlockSpec((1, L, kvw), lambda b, i: (b, 0, 0)),
            pl.BlockSpec((1, lctx, kvw), lambda b, i: (b, 0, 0)),
            pl.BlockSpec((1, lctx, kvw), lambda b, i: (b, 0, 0)),
        ],
        out_specs=pl.BlockSpec((1, tq, W), lambda b, i: (b, i, 0)),
        out_shape=jax.ShapeDtypeStruct((B, L, W), BF16),
        compiler_params=_cparams("parallel", "parallel"),
        name="attention",
    )(q, k, v, kc, vc)


def _merge_kernel(ym_ref, oa_ref, bg_ref, x_ref, mod_ref, wbm_ref, wba_ref, wo_ref, n2_ref, wr_ref, br_ref,
                  hmid_ref, f_ref, route_ref, w1_ref, w2_ref, cnt_ref, run_ref, fprev_ref):
    step = pl.program_id(0)

    @pl.when(step == 0)
    def _():
        run_ref[...] = jnp.zeros_like(run_ref)
        fprev_ref[...] = jnp.zeros_like(fprev_ref)

    D = x_ref.shape[-1]
    tm = x_ref.shape[1]
    f_prev = fprev_ref[...]
    live = jnp.where(step > 0, 1.0, 0.0)
    run = run_ref[...]

    f_hi = f_prev.astype(BF16)
    f_lo = (f_prev - f_hi.astype(F32)).astype(BF16)
    wr = wr_ref[...]
    wr_hi = wr.astype(BF16)
    wr_lo = (wr - wr_hi.astype(F32)).astype(BF16)
    ym = _dot(ym_ref[0], wbm_ref[...])
    logits = _dot_nt(wr_hi, f_hi) + _dot_nt(wr_hi, f_lo) + _dot_nt(wr_lo, f_hi) + br_ref[...]
    row = lax.broadcasted_iota(jnp.int32, logits.shape, 0)
    big = jnp.int32(ROUTER_ROWS)
    neg = -jnp.inf
    is_grp = row < N_GROUPS
    gl = jnp.where(is_grp, logits, neg)
    gmax = jnp.max(gl, axis=0, keepdims=True)
    gsel = jnp.min(jnp.where(gl == gmax, row, big), axis=0, keepdims=True)
    p_grp = 1.0 / jnp.sum(jnp.where(is_grp, jnp.exp(logits - gmax), 0.0), axis=0, keepdims=True)
    lo = EXPERT_ROW0 + gsel * EXPERTS_PER_GROUP
    in_grp = (row >= lo) & (row < lo + EXPERTS_PER_GROUP)
    el = jnp.where(in_grp, logits, neg)
    v1 = jnp.max(el, axis=0, keepdims=True)
    i1 = jnp.min(jnp.where(el == v1, row, big), axis=0, keepdims=True)
    el2 = jnp.where(row == i1, neg, el)
    v2 = jnp.max(el2, axis=0, keepdims=True)
    i2 = jnp.min(jnp.where(el2 == v2, row, big), axis=0, keepdims=True)
    e21 = jnp.exp(v2 - v1)
    w1 = p_grp / (1.0 + e21)
    w2 = p_grp * e21 / (1.0 + e21)
    ya = _dot(oa_ref[0], wba_ref[...])

    sel = jnp.where((row == i1) | (row == i2), live, 0.0)
    tok_u = lax.broadcasted_iota(jnp.int32, (tm, tm), 0)
    tok_n = lax.broadcasted_iota(jnp.int32, (tm, tm), 1)
    earlier = jnp.where(tok_u < tok_n, 1.0, 0.0).astype(BF16)
    rank_all = run + _dot(sel.astype(BF16), earlier)
    rank1 = jnp.sum(jnp.where(row == i1, rank_all, 0.0), axis=0, keepdims=True)
    rank2 = jnp.sum(jnp.where(row == i2, rank_all, 0.0), axis=0, keepdims=True)
    route_ref[...] = jnp.zeros_like(route_ref)
    route_ref[0:1, :] = i1 - EXPERT_ROW0
    route_ref[1:2, :] = i2 - EXPERT_ROW0
    route_ref[8:9, :] = rank1.astype(jnp.int32)
    route_ref[9:10, :] = rank2.astype(jnp.int32)
    run = run + jnp.sum(sel, axis=1, keepdims=True)
    run_ref[...] = run
    cnt_ref[...] = jnp.broadcast_to(run, cnt_ref.shape)
    g0 = bg_ref[0, :, :D].astype(F32)
    g1 = bg_ref[0, :, D:].astype(F32)
    mix = _dot((g0 * ym + g1 * ya).astype(BF16), wo_ref[...])

    eye = jnp.where(tok_u == tok_n, 1.0, 0.0).astype(BF16)
    for w_row, w_ref in ((w1, w1_ref), (w2, w2_ref)):
        w_b = jnp.broadcast_to(w_row, (ROUTER_LANES, tm))
        w_hi = w_b.astype(BF16)
        w_ref[0] = _dot_nt(eye, w_hi) + _dot_nt(eye, (w_b - w_hi.astype(F32)).astype(BF16))

    hmid = x_ref[0] + mod_ref[0, 2:3, :] * mix
    hmid_ref[0] = hmid
    var = jnp.mean(hmid * hmid, axis=-1, keepdims=True)
    y = hmid * lax.rsqrt(var + EPS) * n2_ref[...]
    f = y * (1.0 + mod_ref[0, 4:5, :]) + mod_ref[0, 3:4, :]
    f_ref[0] = f.astype(f_ref.dtype)
    fprev_ref[...] = f


def _merge(ym, oa, bg, x, mod, wbm, wba, wo, n2, wr, br):
    B, L, D = x.shape
    tm = min(L, 512)
    nt = L // tm
    n = B * nt
    full = lambda s: (0, 0)
    main = lambda s: jnp.minimum(s, n - 1)
    tok = lambda s: (main(s) // nt, main(s) % nt, 0)
    prev = lambda s: jnp.maximum(s - 1, 0)
    tok_prev = lambda s: (prev(s) // nt, prev(s) % nt, 0)
    return pl.pallas_call(
        _merge_kernel,
        grid=(n + 1,),
        in_specs=[
            pl.BlockSpec((1, tm, D), tok),
            pl.BlockSpec((1, tm, D), tok),
            pl.BlockSpec((1, tm, 2 * D), tok),
            pl.BlockSpec((1, tm, D), tok),
            pl.BlockSpec((1, 6, D), lambda s: (main(s) // nt, 0, 0)),
            pl.BlockSpec((D, D), full),
            pl.BlockSpec((D, D), full),
            pl.BlockSpec((D, D), full),
            pl.BlockSpec((1, D), full),
            pl.BlockSpec((ROUTER_ROWS, D), full),
            pl.BlockSpec((ROUTER_ROWS, 1), full),
        ],
        out_specs=[
            pl.BlockSpec((1, tm, D), tok),
            pl.BlockSpec((1, tm, D), tok),
            pl.BlockSpec((ROUTE_ROWS, tm), lambda s: (0, prev(s))),
            pl.BlockSpec((1, tm, ROUTER_LANES), tok_prev),
            pl.BlockSpec((1, tm, ROUTER_LANES), tok_prev),
            pl.BlockSpec((ROUTER_ROWS, ROUTER_LANES), full),
        ],
        out_shape=[
            jax.ShapeDtypeStruct((B, L, D), F32),
            jax.ShapeDtypeStruct((B, L, D), F32),
            jax.ShapeDtypeStruct((ROUTE_ROWS, B * L), jnp.int32),
            jax.ShapeDtypeStruct((B, L, ROUTER_LANES), F32),
            jax.ShapeDtypeStruct((B, L, ROUTER_LANES), F32),
            jax.ShapeDtypeStruct((ROUTER_ROWS, ROUTER_LANES), F32),
        ],
        scratch_shapes=[pltpu.VMEM((ROUTER_ROWS, 1), F32), pltpu.VMEM((tm, D), F32)],
        compiler_params=_cparams("arbitrary"),
        name="merge_router",
    )(ym, oa, bg, x, mod, wbm, wba, wo, n2.reshape(1, D), wr, br)


def _load_slots(off_ref, r_ref, s_vmem, s_smem, rsem):
    expert = r_ref[0:SLOT_ROWS, :]
    slot = r_ref[SLOT_ROWS:2 * SLOT_ROWS, :]
    for e in range(1, N_EXPERTS):
        slot = slot + jnp.where(expert == e, off_ref[e], 0)
    s_vmem[...] = slot
    cp = pltpu.make_async_copy(s_vmem, s_smem, rsem)
    cp.start()
    cp.wait()


def _dispatch_kernel(off_ref, nu_ref, r_ref, f_ref, xs_ref, s_vmem, s_smem, zbuf, sem, rsem, *, tm):
    T = EXPERT_TILE

    @pl.when(pl.program_id(0) == 0)
    def _():
        zbuf[...] = jnp.zeros_like(zbuf)

        def zero_copy(row0):
            return pltpu.make_async_copy(zbuf, xs_ref.at[pl.ds(pl.multiple_of(row0, T), T), :], sem)

        def for_each_zero_tile(fn):
            def unused(t, carry):
                fn(zero_copy(t * T))
                return carry

            lax.fori_loop(nu_ref[0], xs_ref.shape[0] // T, unused, 0)
            for e in range(N_EXPERTS):
                @pl.when(off_ref[e + 1] > off_ref[e])
                def _():
                    fn(zero_copy(off_ref[e + 1] - T))

        for_each_zero_tile(lambda cp: cp.start())
        for_each_zero_tile(lambda cp: cp.wait())

    _load_slots(off_ref, r_ref, s_vmem, s_smem, rsem)
    for t in range(tm):
        for k in range(2):
            pltpu.make_async_copy(f_ref.at[pl.ds(t, 1), :], xs_ref.at[pl.ds(s_smem[k, t], 1), :], sem).start(priority=k)
    for _ in range(2):
        pltpu.make_async_copy(f_ref, f_ref, sem).wait()


def _dispatch(off, n_used, route, f, n_rows):
    N, D = f.shape
    tm = min(N, DISPATCH_TILE)
    return pl.pallas_call(
        functools.partial(_dispatch_kernel, tm=tm),
        grid_spec=pltpu.PrefetchScalarGridSpec(
            num_scalar_prefetch=2,
            grid=(N // tm,),
            in_specs=[
                pl.BlockSpec((ROUTE_ROWS, tm), lambda i, off, nu: (0, i)),
                pl.BlockSpec((tm, D), lambda i, off, nu: (i, 0)),
            ],
            out_specs=pl.BlockSpec(memory_space=pl.ANY),
            scratch_shapes=[
                pltpu.VMEM((SLOT_ROWS, tm), jnp.int32),
                pltpu.SMEM((SLOT_ROWS, tm), jnp.int32),
                pltpu.VMEM((EXPERT_TILE, D), F32),
                pltpu.SemaphoreType.DMA(()),
                pltpu.SemaphoreType.DMA(()),
            ],
        ),
        out_shape=jax.ShapeDtypeStruct((n_rows, D), F32),
        compiler_params=_cparams("arbitrary"),
        name="moe_dispatch",
    )(off, n_used, route, f)


def _expert_kernel(te_ref, nu_ref, xs_ref, wg_ref, wu_ref, wd_ref, ys_ref):
    used = pl.program_id(0) < nu_ref[0]

    @pl.when(used)
    def _():
        x = xs_ref[...].astype(BF16)
        hidden = (_silu(_dot(x, wg_ref[0].astype(BF16))) * _dot(x, wu_ref[0].astype(BF16))).astype(BF16)
        ys_ref[...] = _dot(hidden, wd_ref[0].astype(BF16))

    @pl.when(jnp.logical_not(used))
    def _():
        ys_ref[...] = jnp.zeros_like(ys_ref)


def _experts(tile_expert, n_used, xs, wg, wu, wd):
    P, D = xs.shape
    E, _, DE = wg.shape
    T = EXPERT_TILE
    row_map = lambda t, te, nu: (t, 0)
    w_map = lambda t, te, nu: (te[t], 0, 0)
    return pl.pallas_call(
        _expert_kernel,
        grid_spec=pltpu.PrefetchScalarGridSpec(
            num_scalar_prefetch=2,
            grid=(P // T,),
            in_specs=[
                pl.BlockSpec((T, D), row_map),
                pl.BlockSpec((1, D, DE), w_map),
                pl.BlockSpec((1, D, DE), w_map),
                pl.BlockSpec((1, DE, D), w_map),
            ],
            out_specs=pl.BlockSpec((T, D), row_map),
        ),
        out_shape=jax.ShapeDtypeStruct((P, D), F32),
        compiler_params=_cparams("arbitrary"),
        name="moe_experts",
    )(tile_expert, n_used, xs, wg, wu, wd)


def _combine_kernel(off_ref, r_ref, hmid_ref, mod_ref, w1_ref, w2_ref, ys_ref, o_ref, s_vmem, s_smem, y1, y2, sem,
                    rsem, *, tm):
    _load_slots(off_ref, r_ref, s_vmem, s_smem, rsem)
    half = tm // 2
    for t in range(tm):
        for k, buf in ((0, y1), (1, y2)):
            pltpu.make_async_copy(ys_ref.at[pl.ds(s_smem[k, t], 1), :], buf.at[pl.ds(t, 1), :],
                                  sem.at[t // half]).start(priority=k)
    D = o_ref.shape[-1]
    for hf in range(2):
        rows = pl.ds(hf * half, half)
        for buf in (y1, y2):
            pltpu.make_async_copy(buf.at[rows, :], buf.at[rows, :], sem.at[hf]).wait()
        w1 = w1_ref[0, rows, :]
        w2 = w2_ref[0, rows, :]
        for c in range(D // ROUTER_LANES):
            sl = slice(c * ROUTER_LANES, (c + 1) * ROUTER_LANES)
            moe = w1 * y1[rows, sl] + w2 * y2[rows, sl]
            o_ref[0, rows, sl] = hmid_ref[0, rows, sl] + mod_ref[0, 5:6, sl] * moe


def _combine(off, route, hmid, mod, w1, w2, ys):
    B, L, D = hmid.shape
    tm = min(L, MOE_TOKEN_TILE)
    nt = L // tm
    tok = lambda b, i, off: (b, i, 0)
    return pl.pallas_call(
        functools.partial(_combine_kernel, tm=tm),
        grid_spec=pltpu.PrefetchScalarGridSpec(
            num_scalar_prefetch=1,
            grid=(B, nt),
            in_specs=[
                pl.BlockSpec((ROUTE_ROWS, tm), lambda b, i, off: (0, b * nt + i)),
                pl.BlockSpec((1, tm, D), tok),
                pl.BlockSpec((1, 6, D), lambda b, i, off: (b, 0, 0)),
                pl.BlockSpec((1, tm, ROUTER_LANES), tok),
                pl.BlockSpec((1, tm, ROUTER_LANES), tok),
                pl.BlockSpec(memory_space=pl.ANY),
            ],
            out_specs=pl.BlockSpec((1, tm, D), tok),
            scratch_shapes=[
                pltpu.VMEM((SLOT_ROWS, tm), jnp.int32),
                pltpu.SMEM((SLOT_ROWS, tm), jnp.int32),
                pltpu.VMEM((tm, D), F32),
                pltpu.VMEM((tm, D), F32),
                pltpu.SemaphoreType.DMA((2,)),
                pltpu.SemaphoreType.DMA(()),
            ],
        ),
        out_shape=jax.ShapeDtypeStruct((B, L, D), F32),
        compiler_params=_cparams("arbitrary", "arbitrary"),
        name="moe_combine",
    )(off, route, hmid, mod, w1, w2, ys)


def _expert_layout(cnt, n_pairs):
    T = EXPERT_TILE
    counts = cnt[EXPERT_ROW0:, 0].astype(jnp.int32)
    ends = jnp.cumsum((counts + T - 1) // T).astype(jnp.int32)
    off = jnp.concatenate([jnp.zeros((1,), jnp.int32), ends * T])
    n_tiles = n_pairs // T + N_EXPERTS
    t_idx = jnp.arange(n_tiles, dtype=jnp.int32)
    tile_expert = jnp.minimum(jnp.sum(t_idx[:, None] >= ends[None, :], axis=1), N_EXPERTS - 1).astype(jnp.int32)
    n_used = ends[-1]
    tile_expert = jnp.where(t_idx < n_used, tile_expert, jnp.take(tile_expert, n_used - 1))
    return off, tile_expert, n_used.reshape(1), n_tiles * T


def _gate_weights(w_gate, b_gate):
    D = w_gate.shape[0]
    wt = w_gate.T.reshape(4, M_HEADS, D).transpose(1, 0, 2)
    wt = jnp.concatenate([wt, wt[:, 1:2], wt[:, 3:4], jnp.zeros((M_HEADS, GATE_ROWS - 6, D), wt.dtype)], axis=1)
    bt = b_gate.reshape(4, M_HEADS).T
    bt = jnp.concatenate([bt, bt[:, 1:2], bt[:, 3:4], jnp.zeros((M_HEADS, GATE_ROWS - 6), bt.dtype)], axis=1)
    return wt.reshape(M_HEADS * GATE_ROWS, D), bt.reshape(M_HEADS * GATE_ROWS)


def kernel(x, c, ctx, c_ctx, w_ada, b_ada, norm1_w, w_in, b_mgate, q_norm_w, k_norm_w, mh_norm_w, w_branch_m,
           w_branch_a, w_out, norm2_w, w_rg, b_rg, w_re, b_re, w_e_gate, w_e_up, w_e_down):
    B, L, D = x.shape
    depth = w_ada.shape[0]
    assert depth == 1, "context-stream update between layers is not implemented"
    l = 0
    mw = M_HEADS * HEAD_DIM
    aq = A_HEADS * HEAD_DIM
    akv = A_KV_HEADS * HEAD_DIM
    o_mq, o_mk, o_mv, o_og = 0, mw, 2 * mw, 3 * mw
    o_g = 4 * mw
    o_aq = o_g + 4 * M_HEADS
    o_ak = o_aq + aq
    o_av = o_ak + akv
    o_bg = o_av + akv
    scale = HEAD_DIM ** -0.5

    rows = ((B + 1 + 7) // 8) * 8
    cc = jnp.concatenate([c, c_ctx[None, :], jnp.zeros((rows - B - 1, D), F32)], axis=0)
    mod = _ada_mod(cc, w_ada[l], b_ada[l]).reshape(rows, 6, D)

    w = w_in[l]
    wb = lambda a, b: w[:, a:b].astype(BF16)
    cos, sin = _rope_tables(L)
    wgt, bgt = _gate_weights(w[:, o_g:o_aq], b_mgate[l])
    wgt, bgt = wgt.astype(BF16), bgt.reshape(-1, 1)
    wkt = w[:, o_mk:o_mv].T.astype(BF16)
    partner = _rope_partner(jnp.arange(HEAD_DIM))
    qw, kw = q_norm_w[l] * scale, k_norm_w[l]
    plain = lambda a, b, act="none", sc=1.0: ("plain", [wb(a, b)], dict(act=act, scale=sc))
    item_mkt = ("kt", [wkt], {})
    item_gates = ("gates", [wgt, bgt], {})

    mq, mkt, mv, og, gt = _fused_proj(
        x, mod, norm1_w[l], lambda b: b,
        [plain(o_mq, o_mk, sc=scale), item_mkt, plain(o_mv, o_og), plain(o_og, o_g, "sigmoid"), item_gates],
        "proj_mlstm")
    qa, ka, va, bg = _fused_proj(
        x, mod, norm1_w[l], lambda b: b,
        [("normrope", [wb(o_aq, o_ak), cos * qw, sin * qw[partner]], dict(rope=True)),
         ("normrope", [wb(o_ak, o_av), cos * kw, sin * kw[partner]], dict(rope=True)),
         plain(o_av, o_bg), plain(o_bg, o_bg + 2 * D, "sigmoid")],
        "proj_attn")

    lctx = ctx.shape[1]
    cmkt, cmv, cgt, cka, cva = _fused_proj(
        ctx, mod, norm1_w[l], lambda b: B,
        [item_mkt, plain(o_mv, o_og), item_gates,
         ("normrope", [wb(o_ak, o_av), jnp.broadcast_to(kw, (lctx, HEAD_DIM))], dict(rope=False)),
         plain(o_av, o_bg)],
        "proj_ctx")

    ga = _gate_prep(gt, "gate_prep")
    gca = _gate_prep(cgt, "gate_prep_ctx")
    ym = _mlstm(mq, mkt, mv, ga, og, cmkt, cmv, gca, mh_norm_w[l])
    oa = _attention(qa, ka, va, cka, cva)

    pad_w = jnp.zeros((EXPERT_ROW0 - N_GROUPS, D), F32)
    pad_b = jnp.zeros((EXPERT_ROW0 - N_GROUPS,), F32)
    wr = jnp.concatenate([w_rg[l].T, pad_w, w_re[l].T], axis=0)
    br = jnp.concatenate([b_rg[l], pad_b, b_re[l]]).reshape(ROUTER_ROWS, 1)
    hmid, f, route, w1, w2, cnt = _merge(ym, oa, bg, x, mod, w_branch_m[l].astype(BF16), w_branch_a[l].astype(BF16),
                                         w_out[l].astype(BF16), norm2_w[l], wr, br)

    off, tile_expert, n_used, n_rows = _expert_layout(cnt, 2 * B * L)
    xs = _dispatch(off, n_used, route, f.reshape(B * L, D), n_rows)
    ys = _experts(tile_expert, n_used, xs, w_e_gate[l], w_e_up[l], w_e_down[l])
    return _combine(off, route, hmid, mod, w1, w2, ys)
```

```python
import functools

import jax
import jax.numpy as jnp
from jax import lax
from jax.experimental import pallas as pl
from jax.experimental.pallas import tpu as pltpu

F32 = jnp.float32
BF16 = jnp.bfloat16

EPS = 1e-6
GRID_W = 64
ROPE_THETA = 10000.0
HEAD_DIM = 128
M_HEADS = 8
A_HEADS = 8
A_KV_HEADS = 2
A_GROUP = A_HEADS // A_KV_HEADS
N_GROUPS = 4
EXPERTS_PER_GROUP = 8
N_EXPERTS = N_GROUPS * EXPERTS_PER_GROUP
M_INIT = -1e30
MLSTM_CHUNK = 128
GATE_ROWS = 8
ROUTER_LANES = 128
EXPERT_ROW0 = 8
ROUTER_ROWS = EXPERT_ROW0 + N_EXPERTS
EXPERT_TILE = 512
MOE_TOKEN_TILE = 1024
DISPATCH_TILE = 2048
ROUTE_ROWS = 16
SLOT_ROWS = 8
VMEM_LIMIT_BYTES = 48 * 1024 * 1024

NT_DIMS = (((1,), (1,)), ((), ()))


def _cparams(*sem, flags=None):
    return pltpu.CompilerParams(dimension_semantics=sem, vmem_limit_bytes=VMEM_LIMIT_BYTES, flags=flags)


def _dot(a, b):
    return jnp.dot(a, b, preferred_element_type=F32)


def _dot_nt(a, b):
    return lax.dot_general(a, b, NT_DIMS, preferred_element_type=F32)


def _split3(x):
    hi = x.astype(BF16)
    r1 = x - hi.astype(F32)
    mid = r1.astype(BF16)
    lo = (r1 - mid.astype(F32)).astype(BF16)
    return hi, mid, lo


def _silu(x):
    return x * jax.nn.sigmoid(x)


def _ada_kernel(c_ref, w_ref, b_ref, o_ref):
    s = _silu(c_ref[...])
    parts = _split3(s)
    w = w_ref[...]
    w_hi = w.astype(BF16)
    w_lo = (w - w_hi.astype(F32)).astype(BF16)
    acc = _dot(parts[0], w_hi) + _dot(parts[1], w_hi) + _dot(parts[0], w_lo)
    o_ref[...] = acc + b_ref[...]


def _ada_mod(cc, w, b):
    rows, d = cc.shape
    n = w.shape[1]
    tn = min(n, 1536)
    return pl.pallas_call(
        _ada_kernel,
        grid=(n // tn,),
        in_specs=[
            pl.BlockSpec((rows, d), lambda j: (0, 0)),
            pl.BlockSpec((d, tn), lambda j: (0, j)),
            pl.BlockSpec((1, tn), lambda j: (0, j)),
        ],
        out_specs=pl.BlockSpec((rows, tn), lambda j: (0, j)),
        out_shape=jax.ShapeDtypeStruct((rows, n), F32),
        compiler_params=_cparams("parallel"),
        name="ada_mod",
    )(cc, w, b.reshape(1, n))


def _rope_partner(j):
    quarter = HEAD_DIM // 4
    return jnp.where((j % (2 * quarter)) < quarter, j + quarter, j - quarter)


def _split2_lanes(x):
    hi = x.astype(BF16)
    return jnp.concatenate([hi, (x - hi.astype(F32)).astype(BF16)], axis=1)


def _fused_proj_kernel(*refs, plan):
    x_ref, mod_ref, nw_ref = refs[:3]
    n_inputs = sum(n for _, n, _ in plan)
    in_refs = refs[3:3 + n_inputs]
    out_refs = refs[3 + n_inputs:]
    x = x_ref[0]
    var = jnp.mean(x * x, axis=-1, keepdims=True)
    y = x * lax.rsqrt(var + EPS) * nw_ref[...]
    xn = (y * (1.0 + mod_ref[0, 1:2, :]) + mod_ref[0, 0:1, :]).astype(BF16)
    tm = xn.shape[0]
    C = MLSTM_CHUNK
    pos = 0
    for (kind, n, prm), o_ref in zip(plan, out_refs):
        ins = in_refs[pos:pos + n]
        pos += n
        if kind == "plain":
            acc = _dot(xn, ins[0][...])
            if prm["act"] == "sigmoid":
                acc = jax.nn.sigmoid(acc)
            if prm["scale"] != 1.0:
                acc = acc * prm["scale"]
            o_ref[0] = acc.astype(o_ref.dtype)
        elif kind == "kt":
            acc = _dot_nt(ins[0][...], xn)
            for c in range(tm // C):
                o_ref[0, c] = acc[:, c * C:(c + 1) * C].astype(o_ref.dtype)
        elif kind == "gates":
            acc = _dot_nt(ins[0][...], xn) + ins[1][...]
            for h in range(M_HEADS):
                for c in range(tm // C):
                    o_ref[0, h, c] = acc[h * GATE_ROWS:(h + 1) * GATE_ROWS, c * C:(c + 1) * C]
        else:
            acc = _dot(xn, ins[0][...])
            cw = ins[1][...]
            if prm["rope"]:
                sw = ins[2][...]
                src = lax.broadcasted_iota(jnp.int32, (2 * HEAD_DIM, HEAD_DIM), 0) % HEAD_DIM
                dst = lax.broadcasted_iota(jnp.int32, (2 * HEAD_DIM, HEAD_DIM), 1)
                perm = jnp.where(src == _rope_partner(dst), 1.0, 0.0).astype(BF16)
            for h in range(acc.shape[1] // HEAD_DIM):
                a = acc[:, h * HEAD_DIM:(h + 1) * HEAD_DIM]
                hvar = jnp.mean(a * a, axis=-1, keepdims=True)
                yh = a * cw
                if prm["rope"]:
                    yh = yh + _dot(_split2_lanes(a), perm) * sw
                o_ref[0, :, h * HEAD_DIM:(h + 1) * HEAD_DIM] = (yh * lax.rsqrt(hvar + EPS)).astype(o_ref.dtype)


def _fused_proj(x, mod, nw, mod_row, items, name):
    B, L, D = x.shape
    tm = min(L, 512)
    C = MLSTM_CHUNK
    full2 = lambda b, i: (0, 0)
    in_specs = [
        pl.BlockSpec((1, tm, D), lambda b, i: (b, i, 0)),
        pl.BlockSpec((1, 6, D), lambda b, i: (mod_row(b), 0, 0)),
        pl.BlockSpec((1, D), full2),
    ]
    args = [x, mod, nw.reshape(1, D)]
    out_specs, out_shape, plan = [], [], []
    for kind, arrays, prm in items:
        plan.append((kind, len(arrays), prm))
        args.extend(arrays)
        w = arrays[0]
        if kind in ("plain", "normrope"):
            N = w.shape[1]
            in_specs.append(pl.BlockSpec((D, N), full2))
            for t in arrays[1:]:
                in_specs.append(pl.BlockSpec((tm, HEAD_DIM), lambda b, i: (i, 0)))
            out_specs.append(pl.BlockSpec((1, tm, N), lambda b, i: (b, i, 0)))
            out_shape.append(jax.ShapeDtypeStruct((B, L, N), BF16))
        elif kind == "kt":
            R = w.shape[0]
            in_specs.append(pl.BlockSpec((R, D), full2))
            out_specs.append(pl.BlockSpec((1, tm // C, R, C), lambda b, i: (b, i, 0, 0)))
            out_shape.append(jax.ShapeDtypeStruct((B, L // C, R, C), BF16))
        else:
            R = w.shape[0]
            in_specs.append(pl.BlockSpec((R, D), full2))
            in_specs.append(pl.BlockSpec((R, 1), full2))
            out_specs.append(pl.BlockSpec((1, M_HEADS, tm // C, GATE_ROWS, C), lambda b, i: (b, 0, i, 0, 0)))
            out_shape.append(jax.ShapeDtypeStruct((B, M_HEADS, L // C, GATE_ROWS, C), F32))
    return pl.pallas_call(
        functools.partial(_fused_proj_kernel, plan=tuple(plan)),
        grid=(B, L // tm),
        in_specs=in_specs,
        out_specs=out_specs,
        out_shape=out_shape,
        compiler_params=_cparams("parallel", "parallel"),
        name=name,
    )(*args)


def _rope_tables(L):
    rows = L // GRID_W
    row = jnp.repeat(jnp.arange(rows), GRID_W).astype(F32)
    col = jnp.tile(jnp.arange(GRID_W), rows).astype(F32)
    half = HEAD_DIM // 2
    inv = ROPE_THETA ** (-jnp.arange(0, half, 2, dtype=F32) / half)
    ang_r = row[:, None] * inv[None, :]
    ang_c = col[:, None] * inv[None, :]
    cr, sr, cc, sc = jnp.cos(ang_r), jnp.sin(ang_r), jnp.cos(ang_c), jnp.sin(ang_c)
    return jnp.concatenate([cr, cr, cc, cc], axis=-1), jnp.concatenate([-sr, sr, -sc, sc], axis=-1)


def _log_sigmoid(x):
    return jnp.minimum(x, 0.0) - jnp.log1p(jnp.exp(-jnp.abs(x)))


def _tri_masks(n):
    row_i = lax.broadcasted_iota(jnp.int32, (n, n), 0)
    col_i = lax.broadcasted_iota(jnp.int32, (n, n), 1)
    return row_i <= col_i, row_i >= col_i


def _gate_prep_kernel(g_ref, a_ref):
    _, H, nc, R8, C = g_ref.shape
    rows = H * nc * R8
    g = g_ref[0].reshape(rows, C)
    le, ge = _tri_masks(C)
    upper = jnp.where(le, 1.0, 0.0).astype(BF16)
    lower = jnp.where(ge, 1.0, 0.0).astype(BF16)
    lf = _log_sigmoid(g)
    parts = _split3(lf)
    pre = _dot(parts[0], upper) + _dot(parts[1], upper) + _dot(parts[2], upper)
    suf = _dot(parts[0], lower) + _dot(parts[1], lower) + _dot(parts[2], lower)
    rtype = lax.broadcasted_iota(jnp.int32, (rows, C), 0) % R8
    a = jnp.where(rtype == 1, pre, jnp.where(rtype == 3, suf, jnp.where(rtype >= 4, lf, g)))
    a_ref[0] = a.reshape(H, nc, R8, C)


def _gate_prep(g, name):
    B, H, nc, R8, C = g.shape
    blk = pl.BlockSpec((1, H, nc, R8, C), lambda b: (b, 0, 0, 0, 0))
    return pl.pallas_call(
        _gate_prep_kernel,
        grid=(B,),
        in_specs=[blk],
        out_specs=blk,
        out_shape=jax.ShapeDtypeStruct(g.shape, F32),
        compiler_params=_cparams("parallel"),
        name=name,
    )(g)


def _mlstm_kernel(q_ref, kt_ref, v_ref, a_ref, og_ref, kct_ref, vc_ref, ac_ref, nw_ref, o_ref, cin_ref, min_ref,
                  *, nc, nctx):
    C = MLSTM_CHUNK
    D = HEAD_DIM
    le, ge = _tri_masks(C)
    le_bf = jnp.where(le, 1.0, 0.0).astype(BF16)
    ge_bf = jnp.where(ge, 1.0, 0.0).astype(BF16)
    ones_blk = jnp.ones((C, D), BF16)

    def gates(ref, j, fwd):
        ir, fr = (0, 1) if fwd else (2, 3)
        b_row = ref[0, 0, j, fr:fr + 1, :]
        a_row = ref[0, 0, j, ir:ir + 1, :] - b_row
        return a_row, (b_row[:, C - 1:C] if fwd else b_row[:, 0:1])

    def vaug_of(ref, j):
        return jnp.concatenate([ref[0, j * C:(j + 1) * C, :], ones_blk], axis=1)

    def update(cst, m_prev, kt, vaug, a_row, btot):
        mc = jnp.maximum(jnp.max(a_row, axis=1, keepdims=True), m_prev)
        w_in = jnp.exp(a_row - mc)
        ktw = (kt.astype(F32) * w_in).astype(BF16)
        return jnp.exp(m_prev - mc) * cst + _dot(ktw, vaug), btot + mc

    for d, fwd in ((0, True), (1, False)):
        cst = jnp.zeros((D, 2 * D), F32)
        m = jnp.full((1, 1), M_INIT, F32)
        for j in (range(nctx) if fwd else reversed(range(nctx))):
            a_row, btot = gates(ac_ref, j, fwd)
            cst, m = update(cst, m, kct_ref[0, j], vaug_of(vc_ref, j), a_row, btot)
        order = list(range(nc) if fwd else reversed(range(nc)))
        for idx, j in enumerate(order):
            cin_ref[d, j] = cst.astype(BF16)
            min_ref[d, j] = jnp.broadcast_to(m, min_ref.shape[2:])
            if idx + 1 < nc:
                a_row, btot = gates(a_ref, j, fwd)
                cst, m = update(cst, m, kt_ref[0, j], vaug_of(v_ref, j), a_row, btot)

    def out_step(d, j, fwd, q, kt, vaug):
        a_row, _ = gates(a_ref, j, fwd)
        m_prev = min_ref[d, j][0:1, 0:1]
        nlf = -a_ref[0, 0, j, (4 if fwd else 5):(5 if fwd else 6), :]
        nlf_hi = nlf.astype(BF16)
        nlf_lo = (nlf - nlf_hi.astype(F32)).astype(BF16)
        vis = ge_bf if fwd else le_bf
        nb = _dot_nt(vis, jnp.broadcast_to(nlf_hi, (D, C))) + _dot_nt(vis, jnp.broadcast_to(nlf_lo, (D, C)))
        a_mat = jnp.where(ge if fwd else le, a_row, -jnp.inf)
        m_q = jnp.maximum(jnp.max(a_mat, axis=1, keepdims=True), m_prev)
        bm = jnp.broadcast_to(m_q, (C, D))
        w_intra = jnp.exp(a_mat - bm)
        p = (_dot(q, kt) * w_intra).astype(BF16)
        qc = _dot(q, cin_ref[d, j])
        pv = _dot(p, vaug)
        w_inter = jnp.exp(m_prev - bm)
        num = w_inter * qc[:, :D] + pv[:, :D]
        den = w_inter * qc[:, D:] + pv[:, D:]
        return num / jnp.maximum(jnp.abs(den), jnp.exp(nb - bm))

    nw = nw_ref[...]

    def chunk_out(j, carry):
        sl = pl.ds(pl.multiple_of(j * C, C), C)
        q = q_ref[0, sl, :]
        kt = kt_ref[0, j]
        vaug = jnp.concatenate([v_ref[0, sl, :], ones_blk], axis=1)
        h = out_step(0, j, True, q, kt, vaug) + out_step(1, j, False, q, kt, vaug)
        var = jnp.mean(h * h, axis=-1, keepdims=True)
        y = h * lax.rsqrt(var + EPS) * nw
        o_ref[0, sl, :] = (og_ref[0, sl, :].astype(F32) * y).astype(o_ref.dtype)
        return carry

    lax.fori_loop(0, nc, chunk_out, 0, unroll=nc)


def _mlstm(q, kt, v, a, og, kct, vc, ac, mh_norm_w):
    B, L, W = q.shape
    C = MLSTM_CHUNK
    assert C == HEAD_DIM
    nc = L // C
    lctx = vc.shape[1]
    nctx = lctx // C
    H = W // HEAD_DIM
    return pl.pallas_call(
        functools.partial(_mlstm_kernel, nc=nc, nctx=nctx),
        grid=(B, H),
        in_specs=[
            pl.BlockSpec((1, L, HEAD_DIM), lambda b, h: (b, 0, h)),
            pl.BlockSpec((1, nc, HEAD_DIM, C), lambda b, h: (b, 0, h, 0)),
            pl.BlockSpec((1, L, HEAD_DIM), lambda b, h: (b, 0, h)),
            pl.BlockSpec((1, 1, nc, GATE_ROWS, C), lambda b, h: (b, h, 0, 0, 0)),
            pl.BlockSpec((1, L, HEAD_DIM), lambda b, h: (b, 0, h)),
            pl.BlockSpec((1, nctx, HEAD_DIM, C), lambda b, h: (b, 0, h, 0)),
            pl.BlockSpec((1, lctx, HEAD_DIM), lambda b, h: (b, 0, h)),
            pl.BlockSpec((1, 1, nctx, GATE_ROWS, C), lambda b, h: (b, h, 0, 0, 0)),
            pl.BlockSpec((1, HEAD_DIM), lambda b, h: (0, h)),
        ],
        out_specs=pl.BlockSpec((1, L, HEAD_DIM), lambda b, h: (b, 0, h)),
        out_shape=jax.ShapeDtypeStruct((B, L, W), BF16),
        scratch_shapes=[
            pltpu.VMEM((2, nc, HEAD_DIM, 2 * HEAD_DIM), BF16),
            pltpu.VMEM((2, nc, 8, HEAD_DIM), F32),
        ],
        compiler_params=_cparams("parallel", "parallel"),
        name="mlstm",
    )(q, kt, v, a, og, kct, vc, ac, mh_norm_w.reshape(1, W))


def _attn_kernel(q_ref, k_ref, v_ref, kc_ref, vc_ref, o_ref):
    ones_l = jnp.ones((v_ref.shape[1], HEAD_DIM), BF16)
    ones_c = jnp.ones((vc_ref.shape[1], HEAD_DIM), BF16)
    for g in range(A_KV_HEADS):
        gs = slice(g * HEAD_DIM, (g + 1) * HEAD_DIM)
        k = k_ref[0, :, gs]
        kc = kc_ref[0, :, gs]
        v = jnp.concatenate([v_ref[0, :, gs], ones_l], axis=1)
        vc = jnp.concatenate([vc_ref[0, :, gs], ones_c], axis=1)
        for h in range(g * A_GROUP, (g + 1) * A_GROUP):
            hs = slice(h * HEAD_DIM, (h + 1) * HEAD_DIM)
            q = q_ref[0, :, hs]
            s1 = _dot_nt(q, k)
            s2 = _dot_nt(q, kc)
            m = jnp.maximum(jnp.max(s1, axis=1, keepdims=True), jnp.max(s2, axis=1, keepdims=True))
            ol = _dot(jnp.exp(s1 - m).astype(BF16), v) + _dot(jnp.exp(s2 - m).astype(BF16), vc)
            o_ref[0, :, hs] = (ol[:, :HEAD_DIM] / ol[:, HEAD_DIM:]).astype(o_ref.dtype)


def _attention(q, k, v, kc, vc):
    B, L, W = q.shape
    lctx = kc.shape[1]
    tq = min(L, 1024)
    kvw = A_KV_HEADS * HEAD_DIM
    return pl.pallas_call(
        _attn_kernel,
        grid=(B, L // tq),
        in_specs=[
            pl.BlockSpec((1, tq, W), lambda b, i: (b, i, 0)),
            pl.BlockSpec((1, L, kvw), lambda b, i: (b, 0, 0)),
            pl.BlockSpec((1, L, kvw), lambda b, i: (b, 0, 0)),
            pl.BlockSpec((1, lctx, kvw), lambda b, i: (b, 0, 0)),
            pl.BlockSpec((1, lctx, kvw), lambda b, i: (b, 0, 0)),
        ],
        out_specs=pl.BlockSpec((1, tq, W), lambda b, i: (b, i, 0)),
        out_shape=jax.ShapeDtypeStruct((B, L, W), BF16),
        compiler_params=_cparams("parallel", "parallel"),
        name="attention",
    )(q, k, v, kc, vc)


def _merge_kernel(ym_ref, oa_ref, bg_ref, x_ref, mod_ref, wbm_ref, wba_ref, wo_ref, n2_ref, wr_ref, br_ref,
                  hmid_ref, f_ref, route_ref, w1_ref, w2_ref, cnt_ref, run_ref, fprev_ref):
    step = pl.program_id(0)

    @pl.when(step == 0)
    def _():
        run_ref[...] = jnp.zeros_like(run_ref)
        fprev_ref[...] = jnp.zeros_like(fprev_ref)

    D = x_ref.shape[-1]
    tm = x_ref.shape[1]
    f_prev = fprev_ref[...]
    live = jnp.where(step > 0, 1.0, 0.0)
    run = run_ref[...]

    f_hi = f_prev.astype(BF16)
    f_lo = (f_prev - f_hi.astype(F32)).astype(BF16)
    wr = wr_ref[...]
    wr_hi = wr.astype(BF16)
    wr_lo = (wr - wr_hi.astype(F32)).astype(BF16)
    ym = _dot(ym_ref[0], wbm_ref[...])
    logits = _dot_nt(wr_hi, f_hi) + _dot_nt(wr_hi, f_lo) + _dot_nt(wr_lo, f_hi) + br_ref[...]
    row = lax.broadcasted_iota(jnp.int32, logits.shape, 0)
    big = jnp.int32(ROUTER_ROWS)
    neg = -jnp.inf
    is_grp = row < N_GROUPS
    gl = jnp.where(is_grp, logits, neg)
    gmax = jnp.max(gl, axis=0, keepdims=True)
    gsel = jnp.min(jnp.where(gl == gmax, row, big), axis=0, keepdims=True)
    p_grp = 1.0 / jnp.sum(jnp.where(is_grp, jnp.exp(logits - gmax), 0.0), axis=0, keepdims=True)
    lo = EXPERT_ROW0 + gsel * EXPERTS_PER_GROUP
    in_grp = (row >= lo) & (row < lo + EXPERTS_PER_GROUP)
    el = jnp.where(in_grp, logits, neg)
    v1 = jnp.max(el, axis=0, keepdims=True)
    i1 = jnp.min(jnp.where(el == v1, row, big), axis=0, keepdims=True)
    el2 = jnp.where(row == i1, neg, el)
    v2 = jnp.max(el2, axis=0, keepdims=True)
    i2 = jnp.min(jnp.where(el2 == v2, row, big), axis=0, keepdims=True)
    e21 = jnp.exp(v2 - v1)
    w1 = p_grp / (1.0 + e21)
    w2 = p_grp * e21 / (1.0 + e21)
    ya = _dot(oa_ref[0], wba_ref[...])

    sel = jnp.where((row == i1) | (row == i2), live, 0.0)
    tok_u = lax.broadcasted_iota(jnp.int32, (tm, tm), 0)
    tok_n = lax.broadcasted_iota(jnp.int32, (tm, tm), 1)
    earlier = jnp.where(tok_u < tok_n, 1.0, 0.0).astype(BF16)
    rank_all = run + _dot(sel.astype(BF16), earlier)
    rank1 = jnp.sum(jnp.where(row == i1, rank_all, 0.0), axis=0, keepdims=True)
    rank2 = jnp.sum(jnp.where(row == i2, rank_all, 0.0), axis=0, keepdims=True)
    route_ref[...] = jnp.zeros_like(route_ref)
    route_ref[0:1, :] = i1 - EXPERT_ROW0
    route_ref[1:2, :] = i2 - EXPERT_ROW0
    route_ref[8:9, :] = rank1.astype(jnp.int32)
    route_ref[9:10, :] = rank2.astype(jnp.int32)
    run = run + jnp.sum(sel, axis=1, keepdims=True)
    run_ref[...] = run
    cnt_ref[...] = jnp.broadcast_to(run, cnt_ref.shape)
    g0 = bg_ref[0, :, :D].astype(F32)
    g1 = bg_ref[0, :, D:].astype(F32)
    mix = _dot((g0 * ym + g1 * ya).astype(BF16), wo_ref[...])

    eye = jnp.where(tok_u == tok_n, 1.0, 0.0).astype(BF16)
    for w_row, w_ref in ((w1, w1_ref), (w2, w2_ref)):
        w_b = jnp.broadcast_to(w_row, (ROUTER_LANES, tm))
        w_hi = w_b.astype(BF16)
        w_ref[0] = _dot_nt(eye, w_hi) + _dot_nt(eye, (w_b - w_hi.astype(F32)).astype(BF16))

    hmid = x_ref[0] + mod_ref[0, 2:3, :] * mix
    hmid_ref[0] = hmid
    var = jnp.mean(hmid * hmid, axis=-1, keepdims=True)
    y = hmid * lax.rsqrt(var + EPS) * n2_ref[...]
    f = y * (1.0 + mod_ref[0, 4:5, :]) + mod_ref[0, 3:4, :]
    f_ref[0] = f.astype(f_ref.dtype)
    fprev_ref[...] = f


def _merge(ym, oa, bg, x, mod, wbm, wba, wo, n2, wr, br):
    B, L, D = x.shape
    tm = min(L, 512)
    nt = L // tm
    n = B * nt
    full = lambda s: (0, 0)
    main = lambda s: jnp.minimum(s, n - 1)
    tok = lambda s: (main(s) // nt, main(s) % nt, 0)
    prev = lambda s: jnp.maximum(s - 1, 0)
    tok_prev = lambda s: (prev(s) // nt, prev(s) % nt, 0)
    return pl.pallas_call(
        _merge_kernel,
        grid=(n + 1,),
        in_specs=[
            pl.BlockSpec((1, tm, D), tok),
            pl.BlockSpec((1, tm, D), tok),
            pl.BlockSpec((1, tm, 2 * D), tok),
            pl.BlockSpec((1, tm, D), tok),
            pl.BlockSpec((1, 6, D), lambda s: (main(s) // nt, 0, 0)),
            pl.BlockSpec((D, D), full),
            pl.BlockSpec((D, D), full),
            pl.BlockSpec((D, D), full),
            pl.BlockSpec((1, D), full),
            pl.BlockSpec((ROUTER_ROWS, D), full),
            pl.BlockSpec((ROUTER_ROWS, 1), full),
        ],
        out_specs=[
            pl.BlockSpec((1, tm, D), tok),
            pl.BlockSpec((1, tm, D), tok),
            pl.BlockSpec((ROUTE_ROWS, tm), lambda s: (0, prev(s))),
            pl.BlockSpec((1, tm, ROUTER_LANES), tok_prev),
            pl.BlockSpec((1, tm, ROUTER_LANES), tok_prev),
            pl.BlockSpec((ROUTER_ROWS, ROUTER_LANES), full),
        ],
        out_shape=[
            jax.ShapeDtypeStruct((B, L, D), F32),
            jax.ShapeDtypeStruct((B, L, D), F32),
            jax.ShapeDtypeStruct((ROUTE_ROWS, B * L), jnp.int32),
            jax.ShapeDtypeStruct((B, L, ROUTER_LANES), F32),
            jax.ShapeDtypeStruct((B, L, ROUTER_LANES), F32),
            jax.ShapeDtypeStruct((ROUTER_ROWS, ROUTER_LANES), F32),
        ],
        scratch_shapes=[pltpu.VMEM((ROUTER_ROWS, 1), F32), pltpu.VMEM((tm, D), F32)],
        compiler_params=_cparams("arbitrary"),
        name="merge_router",
    )(ym, oa, bg, x, mod, wbm, wba, wo, n2.reshape(1, D), wr, br)


def _load_slots(off_ref, r_ref, s_vmem, s_smem, rsem):
    expert = r_ref[0:SLOT_ROWS, :]
    slot = r_ref[SLOT_ROWS:2 * SLOT_ROWS, :]
    for e in range(1, N_EXPERTS):
        slot = slot + jnp.where(expert == e, off_ref[e], 0)
    s_vmem[...] = slot
    cp = pltpu.make_async_copy(s_vmem, s_smem, rsem)
    cp.start()
    cp.wait()


def _dispatch_kernel(off_ref, nu_ref, r_ref, f_ref, xs_ref, s_vmem, s_smem, zbuf, sem, rsem, *, tm):
    T = EXPERT_TILE

    @pl.when(pl.program_id(0) == 0)
    def _():
        zbuf[...] = jnp.zeros_like(zbuf)

        def zero_copy(row0):
            return pltpu.make_async_copy(zbuf, xs_ref.at[pl.ds(pl.multiple_of(row0, T), T), :], sem)

        def for_each_zero_tile(fn):
            def unused(t, carry):
                fn(zero_copy(t * T))
                return carry

            lax.fori_loop(nu_ref[0], xs_ref.shape[0] // T, unused, 0)
            for e in range(N_EXPERTS):
                @pl.when(off_ref[e + 1] > off_ref[e])
                def _():
                    fn(zero_copy(off_ref[e + 1] - T))

        for_each_zero_tile(lambda cp: cp.start())
        for_each_zero_tile(lambda cp: cp.wait())

    _load_slots(off_ref, r_ref, s_vmem, s_smem, rsem)
    for t in range(tm):
        for k in range(2):
            pltpu.make_async_copy(f_ref.at[pl.ds(t, 1), :], xs_ref.at[pl.ds(s_smem[k, t], 1), :], sem).start(priority=k)
    for _ in range(2):
        pltpu.make_async_copy(f_ref, f_ref, sem).wait()


def _dispatch(off, n_used, route, f, n_rows):
    N, D = f.shape
    tm = min(N, DISPATCH_TILE)
    return pl.pallas_call(
        functools.partial(_dispatch_kernel, tm=tm),
        grid_spec=pltpu.PrefetchScalarGridSpec(
            num_scalar_prefetch=2,
            grid=(N // tm,),
            in_specs=[
                pl.BlockSpec((ROUTE_ROWS, tm), lambda i, off, nu: (0, i)),
                pl.BlockSpec((tm, D), lambda i, off, nu: (i, 0)),
            ],
            out_specs=pl.BlockSpec(memory_space=pl.ANY),
            scratch_shapes=[
                pltpu.VMEM((SLOT_ROWS, tm), jnp.int32),
                pltpu.SMEM((SLOT_ROWS, tm), jnp.int32),
                pltpu.VMEM((EXPERT_TILE, D), F32),
                pltpu.SemaphoreType.DMA(()),
                pltpu.SemaphoreType.DMA(()),
            ],
        ),
        out_shape=jax.ShapeDtypeStruct((n_rows, D), F32),
        compiler_params=_cparams("arbitrary"),
        name="moe_dispatch",
    )(off, n_used, route, f)


def _expert_kernel(te_ref, nu_ref, xs_ref, wg_ref, wu_ref, wd_ref, ys_ref):
    used = pl.program_id(0) < nu_ref[0]

    @pl.when(used)
    def _():
        x = xs_ref[...].astype(BF16)
        hidden = (_silu(_dot(x, wg_ref[0].astype(BF16))) * _dot(x, wu_ref[0].astype(BF16))).astype(BF16)
        ys_ref[...] = _dot(hidden, wd_ref[0].astype(BF16))

    @pl.when(jnp.logical_not(used))
    def _():
        ys_ref[...] = jnp.zeros_like(ys_ref)


def _experts(tile_expert, n_used, xs, wg, wu, wd):
    P, D = xs.shape
    E, _, DE = wg.shape
    T = EXPERT_TILE
    row_map = lambda t, te, nu: (t, 0)
    w_map = lambda t, te, nu: (te[t], 0, 0)
    return pl.pallas_call(
        _expert_kernel,
        grid_spec=pltpu.PrefetchScalarGridSpec(
            num_scalar_prefetch=2,
            grid=(P // T,),
            in_specs=[
                pl.BlockSpec((T, D), row_map),
                pl.BlockSpec((1, D, DE), w_map),
                pl.BlockSpec((1, D, DE), w_map),
                pl.BlockSpec((1, DE, D), w_map),
            ],
            out_specs=pl.BlockSpec((T, D), row_map),
        ),
        out_shape=jax.ShapeDtypeStruct((P, D), F32),
        compiler_params=_cparams("arbitrary"),
        name="moe_experts",
    )(tile_expert, n_used, xs, wg, wu, wd)


def _combine_kernel(off_ref, r_ref, hmid_ref, mod_ref, w1_ref, w2_ref, ys_ref, o_ref, s_vmem, s_smem, y1, y2, sem,
                    rsem, *, tm):
    _load_slots(off_ref, r_ref, s_vmem, s_smem, rsem)
    half = tm // 2
    for t in range(tm):
        for k, buf in ((0, y1), (1, y2)):
            pltpu.make_async_copy(ys_ref.at[pl.ds(s_smem[k, t], 1), :], buf.at[pl.ds(t, 1), :],
                                  sem.at[t // half]).start(priority=k)
    D = o_ref.shape[-1]
    for hf in range(2):
        rows = pl.ds(hf * half, half)
        for buf in (y1, y2):
            pltpu.make_async_copy(buf.at[rows, :], buf.at[rows, :], sem.at[hf]).wait()
        w1 = w1_ref[0, rows, :]
        w2 = w2_ref[0, rows, :]
        for c in range(D // ROUTER_LANES):
            sl = slice(c * ROUTER_LANES, (c + 1) * ROUTER_LANES)
            moe = w1 * y1[rows, sl] + w2 * y2[rows, sl]
            o_ref[0, rows, sl] = hmid_ref[0, rows, sl] + mod_ref[0, 5:6, sl] * moe


def _combine(off, route, hmid, mod, w1, w2, ys):
    B, L, D = hmid.shape
    tm = min(L, MOE_TOKEN_TILE)
    nt = L // tm
    tok = lambda b, i, off: (b, i, 0)
    return pl.pallas_call(
        functools.partial(_combine_kernel, tm=tm),
        grid_spec=pltpu.PrefetchScalarGridSpec(
            num_scalar_prefetch=1,
            grid=(B, nt),
            in_specs=[
                pl.BlockSpec((ROUTE_ROWS, tm), lambda b, i, off: (0, b * nt + i)),
                pl.BlockSpec((1, tm, D), tok),
                pl.BlockSpec((1, 6, D), lambda b, i, off: (b, 0, 0)),
                pl.BlockSpec((1, tm, ROUTER_LANES), tok),
                pl.BlockSpec((1, tm, ROUTER_LANES), tok),
                pl.BlockSpec(memory_space=pl.ANY),
            ],
            out_specs=pl.BlockSpec((1, tm, D), tok),
            scratch_shapes=[
                pltpu.VMEM((SLOT_ROWS, tm), jnp.int32),
                pltpu.SMEM((SLOT_ROWS, tm), jnp.int32),
                pltpu.VMEM((tm, D), F32),
                pltpu.VMEM((tm, D), F32),
                pltpu.SemaphoreType.DMA((2,)),
                pltpu.SemaphoreType.DMA(()),
            ],
        ),
        out_shape=jax.ShapeDtypeStruct((B, L, D), F32),
        compiler_params=_cparams("arbitrary", "arbitrary"),
        name="moe_combine",
    )(off, route, hmid, mod, w1, w2, ys)


def _expert_layout(cnt, n_pairs):
    T = EXPERT_TILE
    counts = cnt[EXPERT_ROW0:, 0].astype(jnp.int32)
    ends = jnp.cumsum((counts + T - 1) // T).astype(jnp.int32)
    off = jnp.concatenate([jnp.zeros((1,), jnp.int32), ends * T])
    n_tiles = n_pairs // T + N_EXPERTS
    t_idx = jnp.arange(n_tiles, dtype=jnp.int32)
    tile_expert = jnp.minimum(jnp.sum(t_idx[:, None] >= ends[None, :], axis=1), N_EXPERTS - 1).astype(jnp.int32)
    n_used = ends[-1]
    tile_expert = jnp.where(t_idx < n_used, tile_expert, jnp.take(tile_expert, n_used - 1))
    return off, tile_expert, n_used.reshape(1), n_tiles * T


def _gate_weights(w_gate, b_gate):
    D = w_gate.shape[0]
    wt = w_gate.T.reshape(4, M_HEADS, D).transpose(1, 0, 2)
    wt = jnp.concatenate([wt, wt[:, 1:2], wt[:, 3:4], jnp.zeros((M_HEADS, GATE_ROWS - 6, D), wt.dtype)], axis=1)
    bt = b_gate.reshape(4, M_HEADS).T
    bt = jnp.concatenate([bt, bt[:, 1:2], bt[:, 3:4], jnp.zeros((M_HEADS, GATE_ROWS - 6), bt.dtype)], axis=1)
    return wt.reshape(M_HEADS * GATE_ROWS, D), bt.reshape(M_HEADS * GATE_ROWS)


def kernel(x, c, ctx, c_ctx, w_ada, b_ada, norm1_w, w_in, b_mgate, q_norm_w, k_norm_w, mh_norm_w, w_branch_m,
           w_branch_a, w_out, norm2_w, w_rg, b_rg, w_re, b_re, w_e_gate, w_e_up, w_e_down):
    B, L, D = x.shape
    depth = w_ada.shape[0]
    assert depth == 1, "context-stream update between layers is not implemented"
    l = 0
    mw = M_HEADS * HEAD_DIM
    aq = A_HEADS * HEAD_DIM
    akv = A_KV_HEADS * HEAD_DIM
    o_mq, o_mk, o_mv, o_og = 0, mw, 2 * mw, 3 * mw
    o_g = 4 * mw
    o_aq = o_g + 4 * M_HEADS
    o_ak = o_aq + aq
    o_av = o_ak + akv
    o_bg = o_av + akv
    scale = HEAD_DIM ** -0.5

    rows = ((B + 1 + 7) // 8) * 8
    cc = jnp.concatenate([c, c_ctx[None, :], jnp.zeros((rows - B - 1, D), F32)], axis=0)
    mod = _ada_mod(cc, w_ada[l], b_ada[l]).reshape(rows, 6, D)

    w = w_in[l]
    wb = lambda a, b: w[:, a:b].astype(BF16)
    cos, sin = _rope_tables(L)
    wgt, bgt = _gate_weights(w[:, o_g:o_aq], b_mgate[l])
    wgt, bgt = wgt.astype(BF16), bgt.reshape(-1, 1)
    wkt = w[:, o_mk:o_mv].T.astype(BF16)
    partner = _rope_partner(jnp.arange(HEAD_DIM))
    qw, kw = q_norm_w[l] * scale, k_norm_w[l]
    plain = lambda a, b, act="none", sc=1.0: ("plain", [wb(a, b)], dict(act=act, scale=sc))
    item_mkt = ("kt", [wkt], {})
    item_gates = ("gates", [wgt, bgt], {})

    mq, mkt, mv, og, gt = _fused_proj(
        x, mod, norm1_w[l], lambda b: b,
        [plain(o_mq, o_mk, sc=scale), item_mkt, plain(o_mv, o_og), plain(o_og, o_g, "sigmoid"), item_gates],
        "proj_mlstm")
    qa, ka, va, bg = _fused_proj(
        x, mod, norm1_w[l], lambda b: b,
        [("normrope", [wb(o_aq, o_ak), cos * qw, sin * qw[partner]], dict(rope=True)),
         ("normrope", [wb(o_ak, o_av), cos * kw, sin * kw[partner]], dict(rope=True)),
         plain(o_av, o_bg), plain(o_bg, o_bg + 2 * D, "sigmoid")],
        "proj_attn")

    lctx = ctx.shape[1]
    cmkt, cmv, cgt, cka, cva = _fused_proj(
        ctx, mod, norm1_w[l], lambda b: B,
        [item_mkt, plain(o_mv, o_og), item_gates,
         ("normrope", [wb(o_ak, o_av), jnp.broadcast_to(kw, (lctx, HEAD_DIM))], dict(rope=False)),
         plain(o_av, o_bg)],
        "proj_ctx")

    ga = _gate_prep(gt, "gate_prep")
    gca = _gate_prep(cgt, "gate_prep_ctx")
    ym = _mlstm(mq, mkt, mv, ga, og, cmkt, cmv, gca, mh_norm_w[l])
    oa = _attention(qa, ka, va, cka, cva)

    pad_w = jnp.zeros((EXPERT_ROW0 - N_GROUPS, D), F32)
    pad_b = jnp.zeros((EXPERT_ROW0 - N_GROUPS,), F32)
    wr = jnp.concatenate([w_rg[l].T, pad_w, w_re[l].T], axis=0)
    br = jnp.concatenate([b_rg[l], pad_b, b_re[l]]).reshape(ROUTER_ROWS, 1)
    hmid, f, route, w1, w2, cnt = _merge(ym, oa, bg, x, mod, w_branch_m[l].astype(BF16), w_branch_a[l].astype(BF16),
                                         w_out[l].astype(BF16), norm2_w[l], wr, br)

    off, tile_expert, n_used, n_rows = _expert_layout(cnt, 2 * B * L)
    xs = _dispatch(off, n_used, route, f.reshape(B * L, D), n_rows)
    ys = _experts(tile_expert, n_used, xs, w_e_gate[l], w_e_up[l], w_e_down[l])
    return _combine(off, route, hmid, mod, w1, w2, ys)
```

```python
import functools

import jax
import jax.numpy as jnp
from jax import lax
from jax.experimental import pallas as pl
from jax.experimental.pallas import tpu as pltpu

F32 = jnp.float32
BF16 = jnp.bfloat16

EPS = 1e-6
GRID_W = 64
ROPE_THETA = 10000.0
HEAD_DIM = 128
M_HEADS = 8
A_HEADS = 8
A_KV_HEADS = 2
A_GROUP = A_HEADS // A_KV_HEADS
N_GROUPS = 4
EXPERTS_PER_GROUP = 8
N_EXPERTS = N_GROUPS * EXPERTS_PER_GROUP
M_INIT = -1e30
MLSTM_CHUNK = 128
GATE_ROWS = 8
ROUTER_LANES = 128
EXPERT_ROW0 = 8
ROUTER_ROWS = EXPERT_ROW0 + N_EXPERTS
EXPERT_TILE = 512
MOE_TOKEN_TILE = 1024
DISPATCH_TILE = 2048
ROUTE_ROWS = 16
SLOT_ROWS = 8
VMEM_LIMIT_BYTES = 48 * 1024 * 1024

NT_DIMS = (((1,), (1,)), ((), ()))


def _cparams(*sem, flags=None):
    return pltpu.CompilerParams(dimension_semantics=sem, vmem_limit_bytes=VMEM_LIMIT_BYTES, flags=flags)


def _dot(a, b):
    return jnp.dot(a, b, preferred_element_type=F32)


def _dot_nt(a, b):
    return lax.dot_general(a, b, NT_DIMS, preferred_element_type=F32)


def _split3(x):
    hi = x.astype(BF16)
    r1 = x - hi.astype(F32)
    mid = r1.astype(BF16)
    lo = (r1 - mid.astype(F32)).astype(BF16)
    return hi, mid, lo


def _silu(x):
    return x * jax.nn.sigmoid(x)


def _ada_kernel(c_ref, w_ref, b_ref, o_ref):
    s = _silu(c_ref[...])
    parts = _split3(s)
    w = w_ref[...]
    w_hi = w.astype(BF16)
    w_lo = (w - w_hi.astype(F32)).astype(BF16)
    acc = _dot(parts[0], w_hi) + _dot(parts[1], w_hi) + _dot(parts[0], w_lo)
    o_ref[...] = acc + b_ref[...]


def _ada_mod(cc, w, b):
    rows, d = cc.shape
    n = w.shape[1]
    tn = min(n, 1536)
    return pl.pallas_call(
        _ada_kernel,
        grid=(n // tn,),
        in_specs=[
            pl.BlockSpec((rows, d), lambda j: (0, 0)),
            pl.BlockSpec((d, tn), lambda j: (0, j)),
            pl.BlockSpec((1, tn), lambda j: (0, j)),
        ],
        out_specs=pl.BlockSpec((rows, tn), lambda j: (0, j)),
        out_shape=jax.ShapeDtypeStruct((rows, n), F32),
        compiler_params=_cparams("parallel"),
        name="ada_mod",
    )(cc, w, b.reshape(1, n))


def _rope_partner(j):
    quarter = HEAD_DIM // 4
    return jnp.where((j % (2 * quarter)) < quarter, j + quarter, j - quarter)


def _split2_lanes(x):
    hi = x.astype(BF16)
    return jnp.concatenate([hi, (x - hi.astype(F32)).astype(BF16)], axis=1)


def _fused_proj_kernel(*refs, plan):
    x_ref, mod_ref, nw_ref = refs[:3]
    n_inputs = sum(n for _, n, _ in plan)
    in_refs = refs[3:3 + n_inputs]
    out_refs = refs[3 + n_inputs:]
    x = x_ref[0]
    var = jnp.mean(x * x, axis=-1, keepdims=True)
    y = x * lax.rsqrt(var + EPS) * nw_ref[...]
    xn = (y * (1.0 + mod_ref[0, 1:2, :]) + mod_ref[0, 0:1, :]).astype(BF16)
    tm = xn.shape[0]
    C = MLSTM_CHUNK
    pos = 0
    for (kind, n, prm), o_ref in zip(plan, out_refs):
        ins = in_refs[pos:pos + n]
        pos += n
        if kind == "plain":
            acc = _dot(xn, ins[0][...])
            if prm["act"] == "sigmoid":
                acc = jax.nn.sigmoid(acc)
            if prm["scale"] != 1.0:
                acc = acc * prm["scale"]
            o_ref[0] = acc.astype(o_ref.dtype)
        elif kind == "kt":
            acc = _dot_nt(ins[0][...], xn)
            for c in range(tm // C):
                o_ref[0, c] = acc[:, c * C:(c + 1) * C].astype(o_ref.dtype)
        elif kind == "gates":
            acc = _dot_nt(ins[0][...], xn) + ins[1][...]
            for h in range(M_HEADS):
                for c in range(tm // C):
                    o_ref[0, h, c] = acc[h * GATE_ROWS:(h + 1) * GATE_ROWS, c * C:(c + 1) * C]
        else:
            acc = _dot(xn, ins[0][...])
            cw = ins[1][...]
            if prm["rope"]:
                sw = ins[2][...]
                lane = lax.broadcasted_iota(jnp.int32, (tm, HEAD_DIM), 1)
                up = _rope_partner(lane) > lane
            for h in range(acc.shape[1] // HEAD_DIM):
                a = acc[:, h * HEAD_DIM:(h + 1) * HEAD_DIM]
                hvar = jnp.mean(a * a, axis=-1, keepdims=True)
                yh = a * cw
                if prm["rope"]:
                    quarter = HEAD_DIM // 4
                    partner = jnp.where(up, pltpu.roll(a, HEAD_DIM - quarter, 1), pltpu.roll(a, quarter, 1))
                    yh = yh + partner * sw
                o_ref[0, :, h * HEAD_DIM:(h + 1) * HEAD_DIM] = (yh * lax.rsqrt(hvar + EPS)).astype(o_ref.dtype)


def _fused_proj(x, mod, nw, mod_row, items, name):
    B, L, D = x.shape
    tm = min(L, 512)
    C = MLSTM_CHUNK
    full2 = lambda b, i: (0, 0)
    in_specs = [
        pl.BlockSpec((1, tm, D), lambda b, i: (b, i, 0)),
        pl.BlockSpec((1, 6, D), lambda b, i: (mod_row(b), 0, 0)),
        pl.BlockSpec((1, D), full2),
    ]
    args = [x, mod, nw.reshape(1, D)]
    out_specs, out_shape, plan = [], [], []
    for kind, arrays, prm in items:
        plan.append((kind, len(arrays), prm))
        args.extend(arrays)
        w = arrays[0]
        if kind in ("plain", "normrope"):
            N = w.shape[1]
            in_specs.append(pl.BlockSpec((D, N), full2))
            for t in arrays[1:]:
                in_specs.append(pl.BlockSpec((tm, HEAD_DIM), lambda b, i: (i, 0)))
            out_specs.append(pl.BlockSpec((1, tm, N), lambda b, i: (b, i, 0)))
            out_shape.append(jax.ShapeDtypeStruct((B, L, N), BF16))
        elif kind == "kt":
            R = w.shape[0]
            in_specs.append(pl.BlockSpec((R, D), full2))
            out_specs.append(pl.BlockSpec((1, tm // C, R, C), lambda b, i: (b, i, 0, 0)))
            out_shape.append(jax.ShapeDtypeStruct((B, L // C, R, C), BF16))
        else:
            R = w.shape[0]
            in_specs.append(pl.BlockSpec((R, D), full2))
            in_specs.append(pl.BlockSpec((R, 1), full2))
            out_specs.append(pl.BlockSpec((1, M_HEADS, tm // C, GATE_ROWS, C), lambda b, i: (b, 0, i, 0, 0)))
            out_shape.append(jax.ShapeDtypeStruct((B, M_HEADS, L // C, GATE_ROWS, C), F32))
    return pl.pallas_call(
        functools.partial(_fused_proj_kernel, plan=tuple(plan)),
        grid=(B, L // tm),
        in_specs=in_specs,
        out_specs=out_specs,
        out_shape=out_shape,
        compiler_params=_cparams("parallel", "parallel"),
        name=name,
    )(*args)


def _rope_tables(L):
    rows = L // GRID_W
    row = jnp.repeat(jnp.arange(rows), GRID_W).astype(F32)
    col = jnp.tile(jnp.arange(GRID_W), rows).astype(F32)
    half = HEAD_DIM // 2
    inv = ROPE_THETA ** (-jnp.arange(0, half, 2, dtype=F32) / half)
    ang_r = row[:, None] * inv[None, :]
    ang_c = col[:, None] * inv[None, :]
    cr, sr, cc, sc = jnp.cos(ang_r), jnp.sin(ang_r), jnp.cos(ang_c), jnp.sin(ang_c)
    return jnp.concatenate([cr, cr, cc, cc], axis=-1), jnp.concatenate([-sr, sr, -sc, sc], axis=-1)


def _log_sigmoid(x):
    return jnp.minimum(x, 0.0) - jnp.log1p(jnp.exp(-jnp.abs(x)))


def _tri_masks(n):
    row_i = lax.broadcasted_iota(jnp.int32, (n, n), 0)
    col_i = lax.broadcasted_iota(jnp.int32, (n, n), 1)
    return row_i <= col_i, row_i >= col_i


def _gate_prep_kernel(g_ref, a_ref):
    _, H, nc, R8, C = g_ref.shape
    rows = H * nc * R8
    g = g_ref[0].reshape(rows, C)
    le, ge = _tri_masks(C)
    upper = jnp.where(le, 1.0, 0.0).astype(BF16)
    lower = jnp.where(ge, 1.0, 0.0).astype(BF16)
    lf = _log_sigmoid(g)
    parts = _split3(lf)
    pre = _dot(parts[0], upper) + _dot(parts[1], upper) + _dot(parts[2], upper)
    suf = _dot(parts[0], lower) + _dot(parts[1], lower) + _dot(parts[2], lower)
    rtype = lax.broadcasted_iota(jnp.int32, (rows, C), 0) % R8
    a = jnp.where(rtype == 1, pre, jnp.where(rtype == 3, suf, jnp.where(rtype >= 4, lf, g)))
    a_ref[0] = a.reshape(H, nc, R8, C)


def _gate_prep(g, name):
    B, H, nc, R8, C = g.shape
    blk = pl.BlockSpec((1, H, nc, R8, C), lambda b: (b, 0, 0, 0, 0))
    return pl.pallas_call(
        _gate_prep_kernel,
        grid=(B,),
        in_specs=[blk],
        out_specs=blk,
        out_shape=jax.ShapeDtypeStruct(g.shape, F32),
        compiler_params=_cparams("parallel"),
        name=name,
    )(g)


def _mlstm_kernel(q_ref, kt_ref, v_ref, a_ref, og_ref, kct_ref, vc_ref, ac_ref, nw_ref, o_ref, cin_ref, min_ref,
                  *, nc, nctx):
    C = MLSTM_CHUNK
    D = HEAD_DIM
    le, ge = _tri_masks(C)
    le_bf = jnp.where(le, 1.0, 0.0).astype(BF16)
    ge_bf = jnp.where(ge, 1.0, 0.0).astype(BF16)
    ones_blk = jnp.ones((C, D), BF16)

    def gates(ref, j, fwd):
        ir, fr = (0, 1) if fwd else (2, 3)
        b_row = ref[0, 0, j, fr:fr + 1, :]
        a_row = ref[0, 0, j, ir:ir + 1, :] - b_row
        return a_row, (b_row[:, C - 1:C] if fwd else b_row[:, 0:1])

    def vaug_of(ref, j):
        return jnp.concatenate([ref[0, j * C:(j + 1) * C, :], ones_blk], axis=1)

    def update(cst, m_prev, kt, vaug, a_row, btot):
        mc = jnp.maximum(jnp.max(a_row, axis=1, keepdims=True), m_prev)
        w_in = jnp.exp(a_row - mc)
        ktw = (kt.astype(F32) * w_in).astype(BF16)
        return jnp.exp(m_prev - mc) * cst + _dot(ktw, vaug), btot + mc

    for d, fwd in ((0, True), (1, False)):
        cst = jnp.zeros((D, 2 * D), F32)
        m = jnp.full((1, 1), M_INIT, F32)
        for j in (range(nctx) if fwd else reversed(range(nctx))):
            a_row, btot = gates(ac_ref, j, fwd)
            cst, m = update(cst, m, kct_ref[0, j], vaug_of(vc_ref, j), a_row, btot)
        order = list(range(nc) if fwd else reversed(range(nc)))
        for idx, j in enumerate(order):
            cin_ref[d, j] = cst.astype(BF16)
            min_ref[d, j] = jnp.broadcast_to(m, min_ref.shape[2:])
            if idx + 1 < nc:
                a_row, btot = gates(a_ref, j, fwd)
                cst, m = update(cst, m, kt_ref[0, j], vaug_of(v_ref, j), a_row, btot)

    def out_step(d, j, fwd, q, kt, vaug):
        a_row, _ = gates(a_ref, j, fwd)
        m_prev = min_ref[d, j][0:1, 0:1]
        nlf = -a_ref[0, 0, j, (4 if fwd else 5):(5 if fwd else 6), :]
        nlf_hi = nlf.astype(BF16)
        nlf_lo = (nlf - nlf_hi.astype(F32)).astype(BF16)
        vis = ge_bf if fwd else le_bf
        nb = _dot_nt(vis, jnp.broadcast_to(nlf_hi, (D, C))) + _dot_nt(vis, jnp.broadcast_to(nlf_lo, (D, C)))
        a_mat = jnp.where(ge if fwd else le, a_row, -jnp.inf)
        m_q = jnp.maximum(jnp.max(a_mat, axis=1, keepdims=True), m_prev)
        bm = jnp.broadcast_to(m_q, (C, D))
        w_intra = jnp.exp(a_mat - bm)
        p = (_dot(q, kt) * w_intra).astype(BF16)
        qc = _dot(q, cin_ref[d, j])
        pv = _dot(p, vaug)
        w_inter = jnp.exp(m_prev - bm)
        num = w_inter * qc[:, :D] + pv[:, :D]
        den = w_inter * qc[:, D:] + pv[:, D:]
        return num / jnp.maximum(jnp.abs(den), jnp.exp(nb - bm))

    nw = nw_ref[...]

    def chunk_out(j, carry):
        sl = pl.ds(pl.multiple_of(j * C, C), C)
        q = q_ref[0, sl, :]
        kt = kt_ref[0, j]
        vaug = jnp.concatenate([v_ref[0, sl, :], ones_blk], axis=1)
        h = out_step(0, j, True, q, kt, vaug) + out_step(1, j, False, q, kt, vaug)
        var = jnp.mean(h * h, axis=-1, keepdims=True)
        y = h * lax.rsqrt(var + EPS) * nw
        o_ref[0, sl, :] = (og_ref[0, sl, :].astype(F32) * y).astype(o_ref.dtype)
        return carry

    lax.fori_loop(0, nc, chunk_out, 0, unroll=nc)


def _mlstm(q, kt, v, a, og, kct, vc, ac, mh_norm_w):
    B, L, W = q.shape
    C = MLSTM_CHUNK
    assert C == HEAD_DIM
    nc = L // C
    lctx = vc.shape[1]
    nctx = lctx // C
    H = W // HEAD_DIM
    return pl.pallas_call(
        functools.partial(_mlstm_kernel, nc=nc, nctx=nctx),
        grid=(B, H),
        in_specs=[
            pl.BlockSpec((1, L, HEAD_DIM), lambda b, h: (b, 0, h)),
            pl.BlockSpec((1, nc, HEAD_DIM, C), lambda b, h: (b, 0, h, 0)),
            pl.BlockSpec((1, L, HEAD_DIM), lambda b, h: (b, 0, h)),
            pl.BlockSpec((1, 1, nc, GATE_ROWS, C), lambda b, h: (b, h, 0, 0, 0)),
            pl.BlockSpec((1, L, HEAD_DIM), lambda b, h: (b, 0, h)),
            pl.BlockSpec((1, nctx, HEAD_DIM, C), lambda b, h: (b, 0, h, 0)),
            pl.BlockSpec((1, lctx, HEAD_DIM), lambda b, h: (b, 0, h)),
            pl.BlockSpec((1, 1, nctx, GATE_ROWS, C), lambda b, h: (b, h, 0, 0, 0)),
            pl.BlockSpec((1, HEAD_DIM), lambda b, h: (0, h)),
        ],
        out_specs=pl.BlockSpec((1, L, HEAD_DIM), lambda b, h: (b, 0, h)),
        out_shape=jax.ShapeDtypeStruct((B, L, W), BF16),
        scratch_shapes=[
            pltpu.VMEM((2, nc, HEAD_DIM, 2 * HEAD_DIM), BF16),
            pltpu.VMEM((2, nc, 8, HEAD_DIM), F32),
        ],
        compiler_params=_cparams("parallel", "parallel"),
        name="mlstm",
    )(q, kt, v, a, og, kct, vc, ac, mh_norm_w.reshape(1, W))


def _attn_kernel(q_ref, k_ref, v_ref, kc_ref, vc_ref, o_ref):
    ones_l = jnp.ones((v_ref.shape[1], HEAD_DIM), BF16)
    ones_c = jnp.ones((vc_ref.shape[1], HEAD_DIM), BF16)
    for g in range(A_KV_HEADS):
        gs = slice(g * HEAD_DIM, (g + 1) * HEAD_DIM)
        k = k_ref[0, :, gs]
        kc = kc_ref[0, :, gs]
        v = jnp.concatenate([v_ref[0, :, gs], ones_l], axis=1)
        vc = jnp.concatenate([vc_ref[0, :, gs], ones_c], axis=1)
        for h in range(g * A_GROUP, (g + 1) * A_GROUP):
            hs = slice(h * HEAD_DIM, (h + 1) * HEAD_DIM)
            q = q_ref[0, :, hs]
            s1 = _dot_nt(q, k)
            s2 = _dot_nt(q, kc)
            m = jnp.maximum(jnp.max(s1, axis=1, keepdims=True), jnp.max(s2, axis=1, keepdims=True))
            ol = _dot(jnp.exp(s1 - m).astype(BF16), v) + _dot(jnp.exp(s2 - m).astype(BF16), vc)
            o_ref[0, :, hs] = (ol[:, :HEAD_DIM] / ol[:, HEAD_DIM:]).astype(o_ref.dtype)


def _attention(q, k, v, kc, vc):
    B, L, W = q.shape
    lctx = kc.shape[1]
    tq = min(L, 1024)
    kvw = A_KV_HEADS * HEAD_DIM
    return pl.pallas_call(
        _attn_kernel,
        grid=(B, L // tq),
        in_specs=[
            pl.BlockSpec((1, tq, W), lambda b, i: (b, i, 0)),
            pl.BlockSpec((1, L, kvw), lambda b, i: (b, 0, 0)),
            pl.BlockSpec((1, L, kvw), lambda b, i: (b, 0, 0)),
            pl.BlockSpec((1, lctx, kvw), lambda b, i: (b, 0, 0)),
            pl.BlockSpec((1, lctx, kvw), lambda b, i: (b, 0, 0)),
        ],
        out_specs=pl.BlockSpec((1, tq, W), lambda b, i: (b, i, 0)),
        out_shape=jax.ShapeDtypeStruct((B, L, W), BF16),
        compiler_params=_cparams("parallel", "parallel"),
        name="attention",
    )(q, k, v, kc, vc)


def _merge_kernel(ym_ref, oa_ref, bg_ref, x_ref, mod_ref, wbm_ref, wba_ref, wo_ref, n2_ref, wr_ref, br_ref,
                  hmid_ref, f_ref, route_ref, w1_ref, w2_ref, cnt_ref, run_ref, fprev_ref):
    step = pl.program_id(0)

    @pl.when(step == 0)
    def _():
        run_ref[...] = jnp.zeros_like(run_ref)
        fprev_ref[...] = jnp.zeros_like(fprev_ref)

    D = x_ref.shape[-1]
    tm = x_ref.shape[1]
    f_prev = fprev_ref[...]
    live = jnp.where(step > 0, 1.0, 0.0)
    run = run_ref[...]

    f_hi = f_prev.astype(BF16)
    f_lo = (f_prev - f_hi.astype(F32)).astype(BF16)
    wr = wr_ref[...]
    wr_hi = wr.astype(BF16)
    wr_lo = (wr - wr_hi.astype(F32)).astype(BF16)
    ym = _dot(ym_ref[0], wbm_ref[...])
    logits = _dot_nt(wr_hi, f_hi) + _dot_nt(wr_hi, f_lo) + _dot_nt(wr_lo, f_hi) + br_ref[...]
    row = lax.broadcasted_iota(jnp.int32, logits.shape, 0)
    big = jnp.int32(ROUTER_ROWS)
    neg = -jnp.inf
    is_grp = row < N_GROUPS
    gl = jnp.where(is_grp, logits, neg)
    gmax = jnp.max(gl, axis=0, keepdims=True)
    gsel = jnp.min(jnp.where(gl == gmax, row, big), axis=0, keepdims=True)
    p_grp = 1.0 / jnp.sum(jnp.where(is_grp, jnp.exp(logits - gmax), 0.0), axis=0, keepdims=True)
    lo = EXPERT_ROW0 + gsel * EXPERTS_PER_GROUP
    in_grp = (row >= lo) & (row < lo + EXPERTS_PER_GROUP)
    el = jnp.where(in_grp, logits, neg)
    v1 = jnp.max(el, axis=0, keepdims=True)
    i1 = jnp.min(jnp.where(el == v1, row, big), axis=0, keepdims=True)
    el2 = jnp.where(row == i1, neg, el)
    v2 = jnp.max(el2, axis=0, keepdims=True)
    i2 = jnp.min(jnp.where(el2 == v2, row, big), axis=0, keepdims=True)
    e21 = jnp.exp(v2 - v1)
    w1 = p_grp / (1.0 + e21)
    w2 = p_grp * e21 / (1.0 + e21)
    ya = _dot(oa_ref[0], wba_ref[...])

    sel = jnp.where((row == i1) | (row == i2), live, 0.0)
    tok_u = lax.broadcasted_iota(jnp.int32, (tm, tm), 0)
    tok_n = lax.broadcasted_iota(jnp.int32, (tm, tm), 1)
    earlier = jnp.where(tok_u < tok_n, 1.0, 0.0).astype(BF16)
    rank_all = run + _dot(sel.astype(BF16), earlier)
    rank1 = jnp.sum(jnp.where(row == i1, rank_all, 0.0), axis=0, keepdims=True)
    rank2 = jnp.sum(jnp.where(row == i2, rank_all, 0.0), axis=0, keepdims=True)
    route_ref[...] = jnp.zeros_like(route_ref)
    route_ref[0:1, :] = i1 - EXPERT_ROW0
    route_ref[1:2, :] = i2 - EXPERT_ROW0
    route_ref[8:9, :] = rank1.astype(jnp.int32)
    route_ref[9:10, :] = rank2.astype(jnp.int32)
    run = run + jnp.sum(sel, axis=1, keepdims=True)
    run_ref[...] = run
    cnt_ref[...] = jnp.broadcast_to(run, cnt_ref.shape)
    g0 = bg_ref[0, :, :D].astype(F32)
    g1 = bg_ref[0, :, D:].astype(F32)
    mix = _dot((g0 * ym + g1 * ya).astype(BF16), wo_ref[...])

    for w_row, w_ref in ((w1, w1_ref), (w2, w2_ref)):
        w_ref[0] = jnp.transpose(jnp.broadcast_to(w_row, (ROUTER_LANES, tm)))

    hmid = x_ref[0] + mod_ref[0, 2:3, :] * mix
    hmid_ref[0] = hmid
    var = jnp.mean(hmid * hmid, axis=-1, keepdims=True)
    y = hmid * lax.rsqrt(var + EPS) * n2_ref[...]
    f = y * (1.0 + mod_ref[0, 4:5, :]) + mod_ref[0, 3:4, :]
    f_ref[0] = f.astype(f_ref.dtype)
    fprev_ref[...] = f


def _merge(ym, oa, bg, x, mod, wbm, wba, wo, n2, wr, br):
    B, L, D = x.shape
    tm = min(L, 512)
    nt = L // tm
    n = B * nt
    full = lambda s: (0, 0)
    main = lambda s: jnp.minimum(s, n - 1)
    tok = lambda s: (main(s) // nt, main(s) % nt, 0)
    prev = lambda s: jnp.maximum(s - 1, 0)
    tok_prev = lambda s: (prev(s) // nt, prev(s) % nt, 0)
    return pl.pallas_call(
        _merge_kernel,
        grid=(n + 1,),
        in_specs=[
            pl.BlockSpec((1, tm, D), tok),
            pl.BlockSpec((1, tm, D), tok),
            pl.BlockSpec((1, tm, 2 * D), tok),
            pl.BlockSpec((1, tm, D), tok),
            pl.BlockSpec((1, 6, D), lambda s: (main(s) // nt, 0, 0)),
            pl.BlockSpec((D, D), full),
            pl.BlockSpec((D, D), full),
            pl.BlockSpec((D, D), full),
            pl.BlockSpec((1, D), full),
            pl.BlockSpec((ROUTER_ROWS, D), full),
            pl.BlockSpec((ROUTER_ROWS, 1), full),
        ],
        out_specs=[
            pl.BlockSpec((1, tm, D), tok),
            pl.BlockSpec((1, tm, D), tok),
            pl.BlockSpec((ROUTE_ROWS, tm), lambda s: (0, prev(s))),
            pl.BlockSpec((1, tm, ROUTER_LANES), tok_prev),
            pl.BlockSpec((1, tm, ROUTER_LANES), tok_prev),
            pl.BlockSpec((ROUTER_ROWS, ROUTER_LANES), full),
        ],
        out_shape=[
            jax.ShapeDtypeStruct((B, L, D), F32),
            jax.ShapeDtypeStruct((B, L, D), F32),
            jax.ShapeDtypeStruct((ROUTE_ROWS, B * L), jnp.int32),
            jax.ShapeDtypeStruct((B, L, ROUTER_LANES), F32),
            jax.ShapeDtypeStruct((B, L, ROUTER_LANES), F32),
            jax.ShapeDtypeStruct((ROUTER_ROWS, ROUTER_LANES), F32),
        ],
        scratch_shapes=[pltpu.VMEM((ROUTER_ROWS, 1), F32), pltpu.VMEM((tm, D), F32)],
        compiler_params=_cparams("arbitrary"),
        name="merge_router",
    )(ym, oa, bg, x, mod, wbm, wba, wo, n2.reshape(1, D), wr, br)


def _load_slots(off_ref, r_ref, s_vmem, s_smem, rsem):
    expert = r_ref[0:SLOT_ROWS, :]
    slot = r_ref[SLOT_ROWS:2 * SLOT_ROWS, :]
    for e in range(1, N_EXPERTS):
        slot = slot + jnp.where(expert == e, off_ref[e], 0)
    s_vmem[...] = slot
    cp = pltpu.make_async_copy(s_vmem, s_smem, rsem)
    cp.start()
    cp.wait()


def _dispatch_kernel(off_ref, nu_ref, r_ref, f_ref, xs_ref, s_vmem, s_smem, zbuf, sem, rsem, *, tm):
    T = EXPERT_TILE

    @pl.when(pl.program_id(0) == 0)
    def _():
        zbuf[...] = jnp.zeros_like(zbuf)

        def zero_copy(row0):
            return pltpu.make_async_copy(zbuf, xs_ref.at[pl.ds(pl.multiple_of(row0, T), T), :], sem)

        def for_each_zero_tile(fn):
            def unused(t, carry):
                fn(zero_copy(t * T))
                return carry

            lax.fori_loop(nu_ref[0], xs_ref.shape[0] // T, unused, 0)
            for e in range(N_EXPERTS):
                @pl.when(off_ref[e + 1] > off_ref[e])
                def _():
                    fn(zero_copy(off_ref[e + 1] - T))

        for_each_zero_tile(lambda cp: cp.start())
        for_each_zero_tile(lambda cp: cp.wait())

    _load_slots(off_ref, r_ref, s_vmem, s_smem, rsem)
    for t in range(tm):
        for k in range(2):
            pltpu.make_async_copy(f_ref.at[pl.ds(t, 1), :], xs_ref.at[pl.ds(s_smem[k, t], 1), :], sem).start(priority=k)
    for _ in range(2):
        pltpu.make_async_copy(f_ref, f_ref, sem).wait()


def _dispatch(off, n_used, route, f, n_rows):
    N, D = f.shape
    tm = min(N, DISPATCH_TILE)
    return pl.pallas_call(
        functools.partial(_dispatch_kernel, tm=tm),
        grid_spec=pltpu.PrefetchScalarGridSpec(
            num_scalar_prefetch=2,
            grid=(N // tm,),
            in_specs=[
                pl.BlockSpec((ROUTE_ROWS, tm), lambda i, off, nu: (0, i)),
                pl.BlockSpec((tm, D), lambda i, off, nu: (i, 0)),
            ],
            out_specs=pl.BlockSpec(memory_space=pl.ANY),
            scratch_shapes=[
                pltpu.VMEM((SLOT_ROWS, tm), jnp.int32),
                pltpu.SMEM((SLOT_ROWS, tm), jnp.int32),
                pltpu.VMEM((EXPERT_TILE, D), F32),
                pltpu.SemaphoreType.DMA(()),
                pltpu.SemaphoreType.DMA(()),
            ],
        ),
        out_shape=jax.ShapeDtypeStruct((n_rows, D), F32),
        compiler_params=_cparams("arbitrary"),
        name="moe_dispatch",
    )(off, n_used, route, f)


def _expert_kernel(te_ref, nu_ref, xs_ref, wg_ref, wu_ref, wd_ref, ys_ref):
    used = pl.program_id(0) < nu_ref[0]

    @pl.when(used)
    def _():
        x = xs_ref[...].astype(BF16)
        hidden = (_silu(_dot(x, wg_ref[0].astype(BF16))) * _dot(x, wu_ref[0].astype(BF16))).astype(BF16)
        ys_ref[...] = _dot(hidden, wd_ref[0].astype(BF16))

    @pl.when(jnp.logical_not(used))
    def _():
        ys_ref[...] = jnp.zeros_like(ys_ref)


def _experts(tile_expert, n_used, xs, wg, wu, wd):
    P, D = xs.shape
    E, _, DE = wg.shape
    T = EXPERT_TILE
    row_map = lambda t, te, nu: (t, 0)
    w_map = lambda t, te, nu: (te[t], 0, 0)
    return pl.pallas_call(
        _expert_kernel,
        grid_spec=pltpu.PrefetchScalarGridSpec(
            num_scalar_prefetch=2,
            grid=(P // T,),
            in_specs=[
                pl.BlockSpec((T, D), row_map),
                pl.BlockSpec((1, D, DE), w_map),
                pl.BlockSpec((1, D, DE), w_map),
                pl.BlockSpec((1, DE, D), w_map),
            ],
            out_specs=pl.BlockSpec((T, D), row_map),
        ),
        out_shape=jax.ShapeDtypeStruct((P, D), F32),
        compiler_params=_cparams("arbitrary"),
        name="moe_experts",
    )(tile_expert, n_used, xs, wg, wu, wd)


def _combine_kernel(off_ref, r_ref, hmid_ref, mod_ref, w1_ref, w2_ref, ys_ref, o_ref, s_vmem, s_smem, y1, y2, sem,
                    rsem, *, tm):
    _load_slots(off_ref, r_ref, s_vmem, s_smem, rsem)
    half = tm // 2
    for t in range(tm):
        for k, buf in ((0, y1), (1, y2)):
            pltpu.make_async_copy(ys_ref.at[pl.ds(s_smem[k, t], 1), :], buf.at[pl.ds(t, 1), :],
                                  sem.at[t // half]).start(priority=k)
    D = o_ref.shape[-1]
    for hf in range(2):
        rows = pl.ds(hf * half, half)
        for buf in (y1, y2):
            pltpu.make_async_copy(buf.at[rows, :], buf.at[rows, :], sem.at[hf]).wait()
        w1 = w1_ref[0, rows, :]
        w2 = w2_ref[0, rows, :]
        for c in range(D // ROUTER_LANES):
            sl = slice(c * ROUTER_LANES, (c + 1) * ROUTER_LANES)
            moe = w1 * y1[rows, sl] + w2 * y2[rows, sl]
            o_ref[0, rows, sl] = hmid_ref[0, rows, sl] + mod_ref[0, 5:6, sl] * moe


def _combine(off, route, hmid, mod, w1, w2, ys):
    B, L, D = hmid.shape
    tm = min(L, MOE_TOKEN_TILE)
    nt = L // tm
    tok = lambda b, i, off: (b, i, 0)
    return pl.pallas_call(
        functools.partial(_combine_kernel, tm=tm),
        grid_spec=pltpu.PrefetchScalarGridSpec(
            num_scalar_prefetch=1,
            grid=(B, nt),
            in_specs=[
                pl.BlockSpec((ROUTE_ROWS, tm), lambda b, i, off: (0, b * nt + i)),
                pl.BlockSpec((1, tm, D), tok),
                pl.BlockSpec((1, 6, D), lambda b, i, off: (b, 0, 0)),
                pl.BlockSpec((1, tm, ROUTER_LANES), tok),
                pl.BlockSpec((1, tm, ROUTER_LANES), tok),
                pl.BlockSpec(memory_space=pl.ANY),
            ],
            out_specs=pl.BlockSpec((1, tm, D), tok),
            scratch_shapes=[
                pltpu.VMEM((SLOT_ROWS, tm), jnp.int32),
                pltpu.SMEM((SLOT_ROWS, tm), jnp.int32),
                pltpu.VMEM((tm, D), F32),
                pltpu.VMEM((tm, D), F32),
                pltpu.SemaphoreType.DMA((2,)),
                pltpu.SemaphoreType.DMA(()),
            ],
        ),
        out_shape=jax.ShapeDtypeStruct((B, L, D), F32),
        compiler_params=_cparams("arbitrary", "arbitrary"),
        name="moe_combine",
    )(off, route, hmid, mod, w1, w2, ys)


def _expert_layout(cnt, n_pairs):
    T = EXPERT_TILE
    counts = cnt[EXPERT_ROW0:, 0].astype(jnp.int32)
    ends = jnp.cumsum((counts + T - 1) // T).astype(jnp.int32)
    off = jnp.concatenate([jnp.zeros((1,), jnp.int32), ends * T])
    n_tiles = n_pairs // T + N_EXPERTS
    t_idx = jnp.arange(n_tiles, dtype=jnp.int32)
    tile_expert = jnp.minimum(jnp.sum(t_idx[:, None] >= ends[None, :], axis=1), N_EXPERTS - 1).astype(jnp.int32)
    n_used = ends[-1]
    tile_expert = jnp.where(t_idx < n_used, tile_expert, jnp.take(tile_expert, n_used - 1))
    return off, tile_expert, n_used.reshape(1), n_tiles * T


def _gate_weights(w_gate, b_gate):
    D = w_gate.shape[0]
    wt = w_gate.T.reshape(4, M_HEADS, D).transpose(1, 0, 2)
    wt = jnp.concatenate([wt, wt[:, 1:2], wt[:, 3:4], jnp.zeros((M_HEADS, GATE_ROWS - 6, D), wt.dtype)], axis=1)
    bt = b_gate.reshape(4, M_HEADS).T
    bt = jnp.concatenate([bt, bt[:, 1:2], bt[:, 3:4], jnp.zeros((M_HEADS, GATE_ROWS - 6), bt.dtype)], axis=1)
    return wt.reshape(M_HEADS * GATE_ROWS, D), bt.reshape(M_HEADS * GATE_ROWS)


def kernel(x, c, ctx, c_ctx, w_ada, b_ada, norm1_w, w_in, b_mgate, q_norm_w, k_norm_w, mh_norm_w, w_branch_m,
           w_branch_a, w_out, norm2_w, w_rg, b_rg, w_re, b_re, w_e_gate, w_e_up, w_e_down):
    B, L, D = x.shape
    depth = w_ada.shape[0]
    assert depth == 1, "context-stream update between layers is not implemented"
    l = 0
    mw = M_HEADS * HEAD_DIM
    aq = A_HEADS * HEAD_DIM
    akv = A_KV_HEADS * HEAD_DIM
    o_mq, o_mk, o_mv, o_og = 0, mw, 2 * mw, 3 * mw
    o_g = 4 * mw
    o_aq = o_g + 4 * M_HEADS
    o_ak = o_aq + aq
    o_av = o_ak + akv
    o_bg = o_av + akv
    scale = HEAD_DIM ** -0.5

    rows = ((B + 1 + 7) // 8) * 8
    cc = jnp.concatenate([c, c_ctx[None, :], jnp.zeros((rows - B - 1, D), F32)], axis=0)
    mod = _ada_mod(cc, w_ada[l], b_ada[l]).reshape(rows, 6, D)

    w = w_in[l]
    wb = lambda a, b: w[:, a:b].astype(BF16)
    cos, sin = _rope_tables(L)
    wgt, bgt = _gate_weights(w[:, o_g:o_aq], b_mgate[l])
    wgt, bgt = wgt.astype(BF16), bgt.reshape(-1, 1)
    wkt = w[:, o_mk:o_mv].T.astype(BF16)
    partner = _rope_partner(jnp.arange(HEAD_DIM))
    qw, kw = q_norm_w[l] * scale, k_norm_w[l]
    plain = lambda a, b, act="none", sc=1.0: ("plain", [wb(a, b)], dict(act=act, scale=sc))
    item_mkt = ("kt", [wkt], {})
    item_gates = ("gates", [wgt, bgt], {})

    mq, mkt, mv, og, gt = _fused_proj(
        x, mod, norm1_w[l], lambda b: b,
        [plain(o_mq, o_mk, sc=scale), item_mkt, plain(o_mv, o_og), plain(o_og, o_g, "sigmoid"), item_gates],
        "proj_mlstm")
    qa, ka, va, bg = _fused_proj(
        x, mod, norm1_w[l], lambda b: b,
        [("normrope", [wb(o_aq, o_ak), cos * qw, sin * qw[partner]], dict(rope=True)),
         ("normrope", [wb(o_ak, o_av), cos * kw, sin * kw[partner]], dict(rope=True)),
         plain(o_av, o_bg), plain(o_bg, o_bg + 2 * D, "sigmoid")],
        "proj_attn")

    lctx = ctx.shape[1]
    cmkt, cmv, cgt, cka, cva = _fused_proj(
        ctx, mod, norm1_w[l], lambda b: B,
        [item_mkt, plain(o_mv, o_og), item_gates,
         ("normrope", [wb(o_ak, o_av), jnp.broadcast_to(kw, (lctx, HEAD_DIM))], dict(rope=False)),
         plain(o_av, o_bg)],
        "proj_ctx")

    ga = _gate_prep(gt, "gate_prep")
    gca = _gate_prep(cgt, "gate_prep_ctx")
    ym = _mlstm(mq, mkt, mv, ga, og, cmkt, cmv, gca, mh_norm_w[l])
    oa = _attention(qa, ka, va, cka, cva)

    pad_w = jnp.zeros((EXPERT_ROW0 - N_GROUPS, D), F32)
    pad_b = jnp.zeros((EXPERT_ROW0 - N_GROUPS,), F32)
    wr = jnp.concatenate([w_rg[l].T, pad_w, w_re[l].T], axis=0)
    br = jnp.concatenate([b_rg[l], pad_b, b_re[l]]).reshape(ROUTER_ROWS, 1)
    hmid, f, route, w1, w2, cnt = _merge(ym, oa, bg, x, mod, w_branch_m[l].astype(BF16), w_branch_a[l].astype(BF16),
                                         w_out[l].astype(BF16), norm2_w[l], wr, br)

    off, tile_expert, n_used, n_rows = _expert_layout(cnt, 2 * B * L)
    xs = _dispatch(off, n_used, route, f.reshape(B * L, D), n_rows)
    ys = _experts(tile_expert, n_used, xs, w_e_gate[l], w_e_up[l], w_e_down[l])
    return _combine(off, route, hmid, mod, w1, w2, ys)
```

```python
import functools

import jax
import jax.numpy as jnp
from jax import lax
from jax.experimental import pallas as pl
from jax.experimental.pallas import tpu as pltpu

F32 = jnp.float32
BF16 = jnp.bfloat16

EPS = 1e-6
GRID_W = 64
ROPE_THETA = 10000.0
HEAD_DIM = 128
M_HEADS = 8
A_HEADS = 8
A_KV_HEADS = 2
A_GROUP = A_HEADS // A_KV_HEADS
N_GROUPS = 4
EXPERTS_PER_GROUP = 8
N_EXPERTS = N_GROUPS * EXPERTS_PER_GROUP
M_INIT = -1e30
MLSTM_CHUNK = 128
GATE_ROWS = 8
ROUTER_LANES = 128
EXPERT_ROW0 = 8
ROUTER_ROWS = EXPERT_ROW0 + N_EXPERTS
EXPERT_TILE = 512
MOE_TOKEN_TILE = 1024
DISPATCH_TILE = 2048
COMBINE_PARTS = 4
ROUTE_ROWS = 16
SLOT_ROWS = 8
VMEM_LIMIT_BYTES = 48 * 1024 * 1024

NT_DIMS = (((1,), (1,)), ((), ()))


def _cparams(*sem, flags=None):
    return pltpu.CompilerParams(dimension_semantics=sem, vmem_limit_bytes=VMEM_LIMIT_BYTES, flags=flags)


def _dot(a, b):
    return jnp.dot(a, b, preferred_element_type=F32)


def _dot_nt(a, b):
    return lax.dot_general(a, b, NT_DIMS, preferred_element_type=F32)


def _split3(x):
    hi = x.astype(BF16)
    r1 = x - hi.astype(F32)
    mid = r1.astype(BF16)
    lo = (r1 - mid.astype(F32)).astype(BF16)
    return hi, mid, lo


def _silu(x):
    return x * jax.nn.sigmoid(x)


def _ada_kernel(c_ref, w_ref, b_ref, o_ref):
    s = _silu(c_ref[...])
    parts = _split3(s)
    w = w_ref[...]
    w_hi = w.astype(BF16)
    w_lo = (w - w_hi.astype(F32)).astype(BF16)
    acc = _dot(parts[0], w_hi) + _dot(parts[1], w_hi) + _dot(parts[0], w_lo)
    o_ref[...] = acc + b_ref[...]


def _ada_mod(cc, w, b):
    rows, d = cc.shape
    n = w.shape[1]
    tn = min(n, 1536)
    return pl.pallas_call(
        _ada_kernel,
        grid=(n // tn,),
        in_specs=[
            pl.BlockSpec((rows, d), lambda j: (0, 0)),
            pl.BlockSpec((d, tn), lambda j: (0, j)),
            pl.BlockSpec((1, tn), lambda j: (0, j)),
        ],
        out_specs=pl.BlockSpec((rows, tn), lambda j: (0, j)),
        out_shape=jax.ShapeDtypeStruct((rows, n), F32),
        compiler_params=_cparams("parallel"),
        name="ada_mod",
    )(cc, w, b.reshape(1, n))


def _rope_partner(j):
    quarter = HEAD_DIM // 4
    return jnp.where((j % (2 * quarter)) < quarter, j + quarter, j - quarter)


def _split2_lanes(x):
    hi = x.astype(BF16)
    return jnp.concatenate([hi, (x - hi.astype(F32)).astype(BF16)], axis=1)


def _fused_proj_kernel(*refs, plan):
    x_ref, mod_ref, nw_ref = refs[:3]
    n_inputs = sum(n for _, n, _ in plan)
    in_refs = refs[3:3 + n_inputs]
    out_refs = refs[3 + n_inputs:]
    x = x_ref[0]
    var = jnp.mean(x * x, axis=-1, keepdims=True)
    y = x * lax.rsqrt(var + EPS) * nw_ref[...]
    xn = (y * (1.0 + mod_ref[0, 1:2, :]) + mod_ref[0, 0:1, :]).astype(BF16)
    tm = xn.shape[0]
    C = MLSTM_CHUNK
    pos = 0
    for (kind, n, prm), o_ref in zip(plan, out_refs):
        ins = in_refs[pos:pos + n]
        pos += n
        if kind == "plain":
            acc = _dot(xn, ins[0][...])
            if prm["act"] == "sigmoid":
                acc = jax.nn.sigmoid(acc)
            if prm["scale"] != 1.0:
                acc = acc * prm["scale"]
            o_ref[0] = acc.astype(o_ref.dtype)
        elif kind == "kt":
            acc = _dot_nt(ins[0][...], xn)
            for c in range(tm // C):
                o_ref[0, c] = acc[:, c * C:(c + 1) * C].astype(o_ref.dtype)
        elif kind == "gates":
            acc = _dot_nt(ins[0][...], xn) + ins[1][...]
            for h in range(M_HEADS):
                for c in range(tm // C):
                    o_ref[0, h, c] = acc[h * GATE_ROWS:(h + 1) * GATE_ROWS, c * C:(c + 1) * C]
        else:
            acc = _dot(xn, ins[0][...])
            cw = ins[1][...]
            if prm["rope"]:
                sw = ins[2][...]
                lane = lax.broadcasted_iota(jnp.int32, (tm, HEAD_DIM), 1)
                up = _rope_partner(lane) > lane
            for h in range(acc.shape[1] // HEAD_DIM):
                a = acc[:, h * HEAD_DIM:(h + 1) * HEAD_DIM]
                hvar = jnp.mean(a * a, axis=-1, keepdims=True)
                yh = a * cw
                if prm["rope"]:
                    quarter = HEAD_DIM // 4
                    partner = jnp.where(up, pltpu.roll(a, HEAD_DIM - quarter, 1), pltpu.roll(a, quarter, 1))
                    yh = yh + partner * sw
                o_ref[0, :, h * HEAD_DIM:(h + 1) * HEAD_DIM] = (yh * lax.rsqrt(hvar + EPS)).astype(o_ref.dtype)


def _fused_proj(x, mod, nw, mod_row, items, name):
    B, L, D = x.shape
    tm = min(L, 512)
    C = MLSTM_CHUNK
    full2 = lambda b, i: (0, 0)
    in_specs = [
        pl.BlockSpec((1, tm, D), lambda b, i: (b, i, 0)),
        pl.BlockSpec((1, 6, D), lambda b, i: (mod_row(b), 0, 0)),
        pl.BlockSpec((1, D), full2),
    ]
    args = [x, mod, nw.reshape(1, D)]
    out_specs, out_shape, plan = [], [], []
    for kind, arrays, prm in items:
        plan.append((kind, len(arrays), prm))
        args.extend(arrays)
        w = arrays[0]
        if kind in ("plain", "normrope"):
            N = w.shape[1]
            in_specs.append(pl.BlockSpec((D, N), full2))
            for t in arrays[1:]:
                in_specs.append(pl.BlockSpec((tm, HEAD_DIM), lambda b, i: (i, 0)))
            out_specs.append(pl.BlockSpec((1, tm, N), lambda b, i: (b, i, 0)))
            out_shape.append(jax.ShapeDtypeStruct((B, L, N), BF16))
        elif kind == "kt":
            R = w.shape[0]
            in_specs.append(pl.BlockSpec((R, D), full2))
            out_specs.append(pl.BlockSpec((1, tm // C, R, C), lambda b, i: (b, i, 0, 0)))
            out_shape.append(jax.ShapeDtypeStruct((B, L // C, R, C), BF16))
        else:
            R = w.shape[0]
            in_specs.append(pl.BlockSpec((R, D), full2))
            in_specs.append(pl.BlockSpec((R, 1), full2))
            out_specs.append(pl.BlockSpec((1, M_HEADS, tm // C, GATE_ROWS, C), lambda b, i: (b, 0, i, 0, 0)))
            out_shape.append(jax.ShapeDtypeStruct((B, M_HEADS, L // C, GATE_ROWS, C), F32))
    return pl.pallas_call(
        functools.partial(_fused_proj_kernel, plan=tuple(plan)),
        grid=(B, L // tm),
        in_specs=in_specs,
        out_specs=out_specs,
        out_shape=out_shape,
        compiler_params=_cparams("parallel", "parallel"),
        name=name,
    )(*args)


def _rope_tables(L):
    rows = L // GRID_W
    row = jnp.repeat(jnp.arange(rows), GRID_W).astype(F32)
    col = jnp.tile(jnp.arange(GRID_W), rows).astype(F32)
    half = HEAD_DIM // 2
    inv = ROPE_THETA ** (-jnp.arange(0, half, 2, dtype=F32) / half)
    ang_r = row[:, None] * inv[None, :]
    ang_c = col[:, None] * inv[None, :]
    cr, sr, cc, sc = jnp.cos(ang_r), jnp.sin(ang_r), jnp.cos(ang_c), jnp.sin(ang_c)
    return jnp.concatenate([cr, cr, cc, cc], axis=-1), jnp.concatenate([-sr, sr, -sc, sc], axis=-1)


def _log_sigmoid(x):
    return jnp.minimum(x, 0.0) - jnp.log1p(jnp.exp(-jnp.abs(x)))


def _tri_masks(n):
    row_i = lax.broadcasted_iota(jnp.int32, (n, n), 0)
    col_i = lax.broadcasted_iota(jnp.int32, (n, n), 1)
    return row_i <= col_i, row_i >= col_i


def _gate_prep_kernel(g_ref, a_ref):
    _, H, nc, R8, C = g_ref.shape
    rows = H * nc * R8
    g = g_ref[0].reshape(rows, C)
    le, ge = _tri_masks(C)
    upper = jnp.where(le, 1.0, 0.0).astype(BF16)
    lower = jnp.where(ge, 1.0, 0.0).astype(BF16)
    lf = _log_sigmoid(g)
    parts = _split3(lf)
    pre = _dot(parts[0], upper) + _dot(parts[1], upper) + _dot(parts[2], upper)
    suf = _dot(parts[0], lower) + _dot(parts[1], lower) + _dot(parts[2], lower)
    rtype = lax.broadcasted_iota(jnp.int32, (rows, C), 0) % R8
    a = jnp.where(rtype == 1, pre, jnp.where(rtype == 3, suf, jnp.where(rtype >= 4, lf, g)))
    a_ref[0] = a.reshape(H, nc, R8, C)


def _gate_prep(g, name):
    B, H, nc, R8, C = g.shape
    blk = pl.BlockSpec((1, H, nc, R8, C), lambda b: (b, 0, 0, 0, 0))
    return pl.pallas_call(
        _gate_prep_kernel,
        grid=(B,),
        in_specs=[blk],
        out_specs=blk,
        out_shape=jax.ShapeDtypeStruct(g.shape, F32),
        compiler_params=_cparams("parallel"),
        name=name,
    )(g)


def _mlstm_kernel(q_ref, kt_ref, v_ref, a_ref, og_ref, kct_ref, vc_ref, ac_ref, nw_ref, o_ref, cin_ref, min_ref,
                  *, nc, nctx):
    C = MLSTM_CHUNK
    D = HEAD_DIM
    le, ge = _tri_masks(C)
    le_bf = jnp.where(le, 1.0, 0.0).astype(BF16)
    ge_bf = jnp.where(ge, 1.0, 0.0).astype(BF16)
    ones_blk = jnp.ones((C, D), BF16)

    def gates(ref, j, fwd):
        ir, fr = (0, 1) if fwd else (2, 3)
        b_row = ref[0, 0, j, fr:fr + 1, :]
        a_row = ref[0, 0, j, ir:ir + 1, :] - b_row
        return a_row, (b_row[:, C - 1:C] if fwd else b_row[:, 0:1])

    def vaug_of(ref, j):
        return jnp.concatenate([ref[0, j * C:(j + 1) * C, :], ones_blk], axis=1)

    def update(cst, m_prev, kt, vaug, a_row, btot):
        mc = jnp.maximum(jnp.max(a_row, axis=1, keepdims=True), m_prev)
        w_in = jnp.exp(a_row - mc)
        ktw = (kt.astype(F32) * w_in).astype(BF16)
        return jnp.exp(m_prev - mc) * cst + _dot(ktw, vaug), btot + mc

    for d, fwd in ((0, True), (1, False)):
        cst = jnp.zeros((D, 2 * D), F32)
        m = jnp.full((1, 1), M_INIT, F32)
        for j in (range(nctx) if fwd else reversed(range(nctx))):
            a_row, btot = gates(ac_ref, j, fwd)
            cst, m = update(cst, m, kct_ref[0, j], vaug_of(vc_ref, j), a_row, btot)
        order = list(range(nc) if fwd else reversed(range(nc)))
        for idx, j in enumerate(order):
            cin_ref[d, j] = cst.astype(BF16)
            min_ref[d, j] = jnp.broadcast_to(m, min_ref.shape[2:])
            if idx + 1 < nc:
                a_row, btot = gates(a_ref, j, fwd)
                cst, m = update(cst, m, kt_ref[0, j], vaug_of(v_ref, j), a_row, btot)

    def out_step(d, j, fwd, q, kt, vaug):
        a_row, _ = gates(a_ref, j, fwd)
        m_prev = min_ref[d, j][0:1, 0:1]
        nlf = -a_ref[0, 0, j, (4 if fwd else 5):(5 if fwd else 6), :]
        nlf_hi = nlf.astype(BF16)
        nlf_lo = (nlf - nlf_hi.astype(F32)).astype(BF16)
        vis = ge_bf if fwd else le_bf
        nb = _dot_nt(vis, jnp.broadcast_to(nlf_hi, (D, C))) + _dot_nt(vis, jnp.broadcast_to(nlf_lo, (D, C)))
        a_mat = jnp.where(ge if fwd else le, a_row, -jnp.inf)
        m_q = jnp.maximum(jnp.max(a_mat, axis=1, keepdims=True), m_prev)
        bm = jnp.broadcast_to(m_q, (C, D))
        w_intra = jnp.exp(a_mat - bm)
        p = (_dot(q, kt) * w_intra).astype(BF16)
        qc = _dot(q, cin_ref[d, j])
        pv = _dot(p, vaug)
        w_inter = jnp.exp(m_prev - bm)
        num = w_inter * qc[:, :D] + pv[:, :D]
        den = w_inter * qc[:, D:] + pv[:, D:]
        return num / jnp.maximum(jnp.abs(den), jnp.exp(nb - bm))

    nw = nw_ref[...]

    def chunk_out(j, carry):
        sl = pl.ds(pl.multiple_of(j * C, C), C)
        q = q_ref[0, sl, :]
        kt = kt_ref[0, j]
        vaug = jnp.concatenate([v_ref[0, sl, :], ones_blk], axis=1)
        h = out_step(0, j, True, q, kt, vaug) + out_step(1, j, False, q, kt, vaug)
        var = jnp.mean(h * h, axis=-1, keepdims=True)
        y = h * lax.rsqrt(var + EPS) * nw
        o_ref[0, sl, :] = (og_ref[0, sl, :].astype(F32) * y).astype(o_ref.dtype)
        return carry

    lax.fori_loop(0, nc, chunk_out, 0, unroll=nc)


def _mlstm(q, kt, v, a, og, kct, vc, ac, mh_norm_w):
    B, L, W = q.shape
    C = MLSTM_CHUNK
    assert C == HEAD_DIM
    nc = L // C
    lctx = vc.shape[1]
    nctx = lctx // C
    H = W // HEAD_DIM
    return pl.pallas_call(
        functools.partial(_mlstm_kernel, nc=nc, nctx=nctx),
        grid=(B, H),
        in_specs=[
            pl.BlockSpec((1, L, HEAD_DIM), lambda b, h: (b, 0, h)),
            pl.BlockSpec((1, nc, HEAD_DIM, C), lambda b, h: (b, 0, h, 0)),
            pl.BlockSpec((1, L, HEAD_DIM), lambda b, h: (b, 0, h)),
            pl.BlockSpec((1, 1, nc, GATE_ROWS, C), lambda b, h: (b, h, 0, 0, 0)),
            pl.BlockSpec((1, L, HEAD_DIM), lambda b, h: (b, 0, h)),
            pl.BlockSpec((1, nctx, HEAD_DIM, C), lambda b, h: (b, 0, h, 0)),
            pl.BlockSpec((1, lctx, HEAD_DIM), lambda b, h: (b, 0, h)),
            pl.BlockSpec((1, 1, nctx, GATE_ROWS, C), lambda b, h: (b, h, 0, 0, 0)),
            pl.BlockSpec((1, HEAD_DIM), lambda b, h: (0, h)),
        ],
        out_specs=pl.BlockSpec((1, L, HEAD_DIM), lambda b, h: (b, 0, h)),
        out_shape=jax.ShapeDtypeStruct((B, L, W), BF16),
        scratch_shapes=[
            pltpu.VMEM((2, nc, HEAD_DIM, 2 * HEAD_DIM), BF16),
            pltpu.VMEM((2, nc, 8, HEAD_DIM), F32),
        ],
        compiler_params=_cparams("parallel", "parallel"),
        name="mlstm",
    )(q, kt, v, a, og, kct, vc, ac, mh_norm_w.reshape(1, W))


def _attn_kernel(q_ref, k_ref, v_ref, kc_ref, vc_ref, o_ref):
    ones_l = jnp.ones((v_ref.shape[1], HEAD_DIM), BF16)
    ones_c = jnp.ones((vc_ref.shape[1], HEAD_DIM), BF16)
    for g in range(A_KV_HEADS):
        gs = slice(g * HEAD_DIM, (g + 1) * HEAD_DIM)
        k = k_ref[0, :, gs]
        kc = kc_ref[0, :, gs]
        v = jnp.concatenate([v_ref[0, :, gs], ones_l], axis=1)
        vc = jnp.concatenate([vc_ref[0, :, gs], ones_c], axis=1)
        for h in range(g * A_GROUP, (g + 1) * A_GROUP):
            hs = slice(h * HEAD_DIM, (h + 1) * HEAD_DIM)
            q = q_ref[0, :, hs]
            s1 = _dot_nt(q, k)
            s2 = _dot_nt(q, kc)
            m = jnp.maximum(jnp.max(s1, axis=1, keepdims=True), jnp.max(s2, axis=1, keepdims=True))
            ol = _dot(jnp.exp(s1 - m).astype(BF16), v) + _dot(jnp.exp(s2 - m).astype(BF16), vc)
            o_ref[0, :, hs] = (ol[:, :HEAD_DIM] / ol[:, HEAD_DIM:]).astype(o_ref.dtype)


def _attention(q, k, v, kc, vc):
    B, L, W = q.shape
    lctx = kc.shape[1]
    tq = min(L, 1024)
    kvw = A_KV_HEADS * HEAD_DIM
    return pl.pallas_call(
        _attn_kernel,
        grid=(B, L // tq),
        in_specs=[
            pl.BlockSpec((1, tq, W), lambda b, i: (b, i, 0)),
            pl.BlockSpec((1, L, kvw), lambda b, i: (b, 0, 0)),
            pl.BlockSpec((1, L, kvw), lambda b, i: (b, 0, 0)),
            pl.BlockSpec((1, lctx, kvw), lambda b, i: (b, 0, 0)),
            pl.BlockSpec((1, lctx, kvw), lambda b, i: (b, 0, 0)),
        ],
        out_specs=pl.BlockSpec((1, tq, W), lambda b, i: (b, i, 0)),
        out_shape=jax.ShapeDtypeStruct((B, L, W), BF16),
        compiler_params=_cparams("parallel", "parallel"),
        name="attention",
    )(q, k, v, kc, vc)


def _merge_kernel(ym_ref, oa_ref, bg_ref, x_ref, mod_ref, wbm_ref, wba_ref, wo_ref, n2_ref, wr_ref, br_ref,
                  hmid_ref, f_ref, route_ref, w1_ref, w2_ref, cnt_ref, run_ref, fprev_ref):
    step = pl.program_id(0)

    @pl.when(step == 0)
    def _():
        run_ref[...] = jnp.zeros_like(run_ref)
        fprev_ref[...] = jnp.zeros_like(fprev_ref)

    D = x_ref.shape[-1]
    tm = x_ref.shape[1]
    f_prev = fprev_ref[...]
    live = jnp.where(step > 0, 1.0, 0.0)
    run = run_ref[...]

    f_hi = f_prev.astype(BF16)
    f_lo = (f_prev - f_hi.astype(F32)).astype(BF16)
    wr = wr_ref[...]
    wr_hi = wr.astype(BF16)
    wr_lo = (wr - wr_hi.astype(F32)).astype(BF16)
    ym = _dot(ym_ref[0], wbm_ref[...])
    logits = _dot_nt(wr_hi, f_hi) + _dot_nt(wr_hi, f_lo) + _dot_nt(wr_lo, f_hi) + br_ref[...]
    row = lax.broadcasted_iota(jnp.int32, logits.shape, 0)
    big = jnp.int32(ROUTER_ROWS)
    neg = -jnp.inf
    is_grp = row < N_GROUPS
    gl = jnp.where(is_grp, logits, neg)
    gmax = jnp.max(gl, axis=0, keepdims=True)
    gsel = jnp.min(jnp.where(gl == gmax, row, big), axis=0, keepdims=True)
    p_grp = 1.0 / jnp.sum(jnp.where(is_grp, jnp.exp(logits - gmax), 0.0), axis=0, keepdims=True)
    lo = EXPERT_ROW0 + gsel * EXPERTS_PER_GROUP
    in_grp = (row >= lo) & (row < lo + EXPERTS_PER_GROUP)
    el = jnp.where(in_grp, logits, neg)
    v1 = jnp.max(el, axis=0, keepdims=True)
    i1 = jnp.min(jnp.where(el == v1, row, big), axis=0, keepdims=True)
    el2 = jnp.where(row == i1, neg, el)
    v2 = jnp.max(el2, axis=0, keepdims=True)
    i2 = jnp.min(jnp.where(el2 == v2, row, big), axis=0, keepdims=True)
    e21 = jnp.exp(v2 - v1)
    w1 = p_grp / (1.0 + e21)
    w2 = p_grp * e21 / (1.0 + e21)
    ya = _dot(oa_ref[0], wba_ref[...])

    sel = jnp.where((row == i1) | (row == i2), live, 0.0)
    tok_u = lax.broadcasted_iota(jnp.int32, (tm, tm), 0)
    tok_n = lax.broadcasted_iota(jnp.int32, (tm, tm), 1)
    earlier = jnp.where(tok_u < tok_n, 1.0, 0.0).astype(BF16)
    rank_all = run + _dot(sel.astype(BF16), earlier)
    rank1 = jnp.sum(jnp.where(row == i1, rank_all, 0.0), axis=0, keepdims=True)
    rank2 = jnp.sum(jnp.where(row == i2, rank_all, 0.0), axis=0, keepdims=True)
    route_ref[...] = jnp.zeros_like(route_ref)
    route_ref[0:1, :] = i1 - EXPERT_ROW0
    route_ref[1:2, :] = i2 - EXPERT_ROW0
    route_ref[8:9, :] = rank1.astype(jnp.int32)
    route_ref[9:10, :] = rank2.astype(jnp.int32)
    run = run + jnp.sum(sel, axis=1, keepdims=True)
    run_ref[...] = run
    cnt_ref[...] = jnp.broadcast_to(run, cnt_ref.shape)
    g0 = bg_ref[0, :, :D].astype(F32)
    g1 = bg_ref[0, :, D:].astype(F32)
    mix = _dot((g0 * ym + g1 * ya).astype(BF16), wo_ref[...])

    for w_row, w_ref in ((w1, w1_ref), (w2, w2_ref)):
        w_ref[0] = jnp.transpose(jnp.broadcast_to(w_row, (ROUTER_LANES, tm)))

    hmid = x_ref[0] + mod_ref[0, 2:3, :] * mix
    hmid_ref[0] = hmid
    var = jnp.mean(hmid * hmid, axis=-1, keepdims=True)
    y = hmid * lax.rsqrt(var + EPS) * n2_ref[...]
    f = y * (1.0 + mod_ref[0, 4:5, :]) + mod_ref[0, 3:4, :]
    f_ref[0] = f.astype(f_ref.dtype)
    fprev_ref[...] = f


def _merge(ym, oa, bg, x, mod, wbm, wba, wo, n2, wr, br):
    B, L, D = x.shape
    tm = min(L, 512)
    nt = L // tm
    n = B * nt
    full = lambda s: (0, 0)
    main = lambda s: jnp.minimum(s, n - 1)
    tok = lambda s: (main(s) // nt, main(s) % nt, 0)
    prev = lambda s: jnp.maximum(s - 1, 0)
    tok_prev = lambda s: (prev(s) // nt, prev(s) % nt, 0)
    return pl.pallas_call(
        _merge_kernel,
        grid=(n + 1,),
        in_specs=[
            pl.BlockSpec((1, tm, D), tok),
            pl.BlockSpec((1, tm, D), tok),
            pl.BlockSpec((1, tm, 2 * D), tok),
            pl.BlockSpec((1, tm, D), tok),
            pl.BlockSpec((1, 6, D), lambda s: (main(s) // nt, 0, 0)),
            pl.BlockSpec((D, D), full),
            pl.BlockSpec((D, D), full),
            pl.BlockSpec((D, D), full),
            pl.BlockSpec((1, D), full),
            pl.BlockSpec((ROUTER_ROWS, D), full),
            pl.BlockSpec((ROUTER_ROWS, 1), full),
        ],
        out_specs=[
            pl.BlockSpec((1, tm, D), tok),
            pl.BlockSpec((1, tm, D), tok),
            pl.BlockSpec((ROUTE_ROWS, tm), lambda s: (0, prev(s))),
            pl.BlockSpec((1, tm, ROUTER_LANES), tok_prev),
            pl.BlockSpec((1, tm, ROUTER_LANES), tok_prev),
            pl.BlockSpec((ROUTER_ROWS, ROUTER_LANES), full),
        ],
        out_shape=[
            jax.ShapeDtypeStruct((B, L, D), F32),
            jax.ShapeDtypeStruct((B, L, D), F32),
            jax.ShapeDtypeStruct((ROUTE_ROWS, B * L), jnp.int32),
            jax.ShapeDtypeStruct((B, L, ROUTER_LANES), F32),
            jax.ShapeDtypeStruct((B, L, ROUTER_LANES), F32),
            jax.ShapeDtypeStruct((ROUTER_ROWS, ROUTER_LANES), F32),
        ],
        scratch_shapes=[pltpu.VMEM((ROUTER_ROWS, 1), F32), pltpu.VMEM((tm, D), F32)],
        compiler_params=_cparams("arbitrary"),
        name="merge_router",
    )(ym, oa, bg, x, mod, wbm, wba, wo, n2.reshape(1, D), wr, br)


def _load_slots(off_ref, r_ref, s_vmem, s_smem, rsem):
    expert = r_ref[0:SLOT_ROWS, :]
    slot = r_ref[SLOT_ROWS:2 * SLOT_ROWS, :]
    for e in range(1, N_EXPERTS):
        slot = slot + jnp.where(expert == e, off_ref[e], 0)
    s_vmem[...] = slot
    cp = pltpu.make_async_copy(s_vmem, s_smem, rsem)
    cp.start()
    cp.wait()


def _dispatch_kernel(off_ref, nu_ref, r_ref, f_ref, xs_ref, s_vmem, s_smem, zbuf, sem, rsem, *, tm):
    T = EXPERT_TILE

    @pl.when(pl.program_id(0) == 0)
    def _():
        zbuf[...] = jnp.zeros_like(zbuf)

        def zero_copy(row0):
            return pltpu.make_async_copy(zbuf, xs_ref.at[pl.ds(pl.multiple_of(row0, T), T), :], sem)

        def for_each_zero_tile(fn):
            def unused(t, carry):
                fn(zero_copy(t * T))
                return carry

            lax.fori_loop(nu_ref[0], xs_ref.shape[0] // T, unused, 0)
            for e in range(N_EXPERTS):
                @pl.when(off_ref[e + 1] > off_ref[e])
                def _():
                    fn(zero_copy(off_ref[e + 1] - T))

        for_each_zero_tile(lambda cp: cp.start())
        for_each_zero_tile(lambda cp: cp.wait())

    _load_slots(off_ref, r_ref, s_vmem, s_smem, rsem)
    for t in range(tm):
        for k in range(2):
            pltpu.make_async_copy(f_ref.at[pl.ds(t, 1), :], xs_ref.at[pl.ds(s_smem[k, t], 1), :], sem).start(priority=k)
    for _ in range(2):
        pltpu.make_async_copy(f_ref, f_ref, sem).wait()


def _dispatch(off, n_used, route, f, n_rows):
    N, D = f.shape
    tm = min(N, DISPATCH_TILE)
    return pl.pallas_call(
        functools.partial(_dispatch_kernel, tm=tm),
        grid_spec=pltpu.PrefetchScalarGridSpec(
            num_scalar_prefetch=2,
            grid=(N // tm,),
            in_specs=[
                pl.BlockSpec((ROUTE_ROWS, tm), lambda i, off, nu: (0, i)),
                pl.BlockSpec((tm, D), lambda i, off, nu: (i, 0)),
            ],
            out_specs=pl.BlockSpec(memory_space=pl.ANY),
            scratch_shapes=[
                pltpu.VMEM((SLOT_ROWS, tm), jnp.int32),
                pltpu.SMEM((SLOT_ROWS, tm), jnp.int32),
                pltpu.VMEM((EXPERT_TILE, D), F32),
                pltpu.SemaphoreType.DMA(()),
                pltpu.SemaphoreType.DMA(()),
            ],
        ),
        out_shape=jax.ShapeDtypeStruct((n_rows, D), F32),
        compiler_params=_cparams("arbitrary"),
        name="moe_dispatch",
    )(off, n_used, route, f)


def _expert_kernel(te_ref, nu_ref, xs_ref, wg_ref, wu_ref, wd_ref, ys_ref):
    used = pl.program_id(0) < nu_ref[0]

    @pl.when(used)
    def _():
        x = xs_ref[...].astype(BF16)
        hidden = (_silu(_dot(x, wg_ref[0].astype(BF16))) * _dot(x, wu_ref[0].astype(BF16))).astype(BF16)
        ys_ref[...] = _dot(hidden, wd_ref[0].astype(BF16))

    @pl.when(jnp.logical_not(used))
    def _():
        ys_ref[...] = jnp.zeros_like(ys_ref)


def _experts(tile_expert, n_used, xs, wg, wu, wd):
    P, D = xs.shape
    E, _, DE = wg.shape
    T = EXPERT_TILE
    row_map = lambda t, te, nu: (t, 0)
    row_in = lambda t, te, nu: (jnp.minimum(t, nu[0] - 1), 0)
    w_map = lambda t, te, nu: (te[t], 0, 0)
    return pl.pallas_call(
        _expert_kernel,
        grid_spec=pltpu.PrefetchScalarGridSpec(
            num_scalar_prefetch=2,
            grid=(P // T,),
            in_specs=[
                pl.BlockSpec((T, D), row_in),
                pl.BlockSpec((1, D, DE), w_map),
                pl.BlockSpec((1, D, DE), w_map),
                pl.BlockSpec((1, DE, D), w_map),
            ],
            out_specs=pl.BlockSpec((T, D), row_map),
        ),
        out_shape=jax.ShapeDtypeStruct((P, D), F32),
        compiler_params=_cparams("arbitrary"),
        name="moe_experts",
    )(tile_expert, n_used, xs, wg, wu, wd)


def _combine_kernel(off_ref, r_ref, hmid_ref, mod_ref, w1_ref, w2_ref, ys_ref, o_ref, s_vmem, s_smem, y1, y2, sem,
                    rsem, *, tm):
    _load_slots(off_ref, r_ref, s_vmem, s_smem, rsem)
    part = tm // COMBINE_PARTS
    for t in range(tm):
        for k, buf in ((0, y1), (1, y2)):
            pltpu.make_async_copy(ys_ref.at[pl.ds(s_smem[k, t], 1), :], buf.at[pl.ds(t, 1), :],
                                  sem.at[t // part]).start(priority=k)
    D = o_ref.shape[-1]
    for hf in range(COMBINE_PARTS):
        rows = pl.ds(hf * part, part)
        for buf in (y1, y2):
            pltpu.make_async_copy(buf.at[rows, :], buf.at[rows, :], sem.at[hf]).wait()
        w1 = w1_ref[0, rows, :]
        w2 = w2_ref[0, rows, :]
        for c in range(D // ROUTER_LANES):
            sl = slice(c * ROUTER_LANES, (c + 1) * ROUTER_LANES)
            moe = w1 * y1[rows, sl] + w2 * y2[rows, sl]
            o_ref[0, rows, sl] = hmid_ref[0, rows, sl] + mod_ref[0, 5:6, sl] * moe


def _combine(off, route, hmid, mod, w1, w2, ys):
    B, L, D = hmid.shape
    tm = min(L, MOE_TOKEN_TILE)
    nt = L // tm
    tok = lambda b, i, off: (b, i, 0)
    return pl.pallas_call(
        functools.partial(_combine_kernel, tm=tm),
        grid_spec=pltpu.PrefetchScalarGridSpec(
            num_scalar_prefetch=1,
            grid=(B, nt),
            in_specs=[
                pl.BlockSpec((ROUTE_ROWS, tm), lambda b, i, off: (0, b * nt + i)),
                pl.BlockSpec((1, tm, D), tok),
                pl.BlockSpec((1, 6, D), lambda b, i, off: (b, 0, 0)),
                pl.BlockSpec((1, tm, ROUTER_LANES), tok),
                pl.BlockSpec((1, tm, ROUTER_LANES), tok),
                pl.BlockSpec(memory_space=pl.ANY),
            ],
            out_specs=pl.BlockSpec((1, tm, D), tok),
            scratch_shapes=[
                pltpu.VMEM((SLOT_ROWS, tm), jnp.int32),
                pltpu.SMEM((SLOT_ROWS, tm), jnp.int32),
                pltpu.VMEM((tm, D), F32),
                pltpu.VMEM((tm, D), F32),
                pltpu.SemaphoreType.DMA((COMBINE_PARTS,)),
                pltpu.SemaphoreType.DMA(()),
            ],
        ),
        out_shape=jax.ShapeDtypeStruct((B, L, D), F32),
        compiler_params=_cparams("arbitrary", "arbitrary"),
        name="moe_combine",
    )(off, route, hmid, mod, w1, w2, ys)


def _expert_layout(cnt, n_pairs):
    T = EXPERT_TILE
    counts = cnt[EXPERT_ROW0:, 0].astype(jnp.int32)
    ends = jnp.cumsum((counts + T - 1) // T).astype(jnp.int32)
    off = jnp.concatenate([jnp.zeros((1,), jnp.int32), ends * T])
    n_tiles = n_pairs // T + N_EXPERTS
    t_idx = jnp.arange(n_tiles, dtype=jnp.int32)
    tile_expert = jnp.minimum(jnp.sum(t_idx[:, None] >= ends[None, :], axis=1), N_EXPERTS - 1).astype(jnp.int32)
    n_used = ends[-1]
    tile_expert = jnp.where(t_idx < n_used, tile_expert, jnp.take(tile_expert, n_used - 1))
    return off, tile_expert, n_used.reshape(1), n_tiles * T


def _gate_weights(w_gate, b_gate):
    D = w_gate.shape[0]
    wt = w_gate.T.reshape(4, M_HEADS, D).transpose(1, 0, 2)
    wt = jnp.concatenate([wt, wt[:, 1:2], wt[:, 3:4], jnp.zeros((M_HEADS, GATE_ROWS - 6, D), wt.dtype)], axis=1)
    bt = b_gate.reshape(4, M_HEADS).T
    bt = jnp.concatenate([bt, bt[:, 1:2], bt[:, 3:4], jnp.zeros((M_HEADS, GATE_ROWS - 6), bt.dtype)], axis=1)
    return wt.reshape(M_HEADS * GATE_ROWS, D), bt.reshape(M_HEADS * GATE_ROWS)


def kernel(x, c, ctx, c_ctx, w_ada, b_ada, norm1_w, w_in, b_mgate, q_norm_w, k_norm_w, mh_norm_w, w_branch_m,
           w_branch_a, w_out, norm2_w, w_rg, b_rg, w_re, b_re, w_e_gate, w_e_up, w_e_down):
    B, L, D = x.shape
    depth = w_ada.shape[0]
    assert depth == 1, "context-stream update between layers is not implemented"
    l = 0
    mw = M_HEADS * HEAD_DIM
    aq = A_HEADS * HEAD_DIM
    akv = A_KV_HEADS * HEAD_DIM
    o_mq, o_mk, o_mv, o_og = 0, mw, 2 * mw, 3 * mw
    o_g = 4 * mw
    o_aq = o_g + 4 * M_HEADS
    o_ak = o_aq + aq
    o_av = o_ak + akv
    o_bg = o_av + akv
    scale = HEAD_DIM ** -0.5

    rows = ((B + 1 + 7) // 8) * 8
    cc = jnp.concatenate([c, c_ctx[None, :], jnp.zeros((rows - B - 1, D), F32)], axis=0)
    mod = _ada_mod(cc, w_ada[l], b_ada[l]).reshape(rows, 6, D)

    w = w_in[l]
    wb = lambda a, b: w[:, a:b].astype(BF16)
    cos, sin = _rope_tables(L)
    wgt, bgt = _gate_weights(w[:, o_g:o_aq], b_mgate[l])
    wgt, bgt = wgt.astype(BF16), bgt.reshape(-1, 1)
    wkt = w[:, o_mk:o_mv].T.astype(BF16)
    partner = _rope_partner(jnp.arange(HEAD_DIM))
    qw, kw = q_norm_w[l] * scale, k_norm_w[l]
    plain = lambda a, b, act="none", sc=1.0: ("plain", [wb(a, b)], dict(act=act, scale=sc))
    item_mkt = ("kt", [wkt], {})
    item_gates = ("gates", [wgt, bgt], {})

    mq, mkt, mv, og, gt = _fused_proj(
        x, mod, norm1_w[l], lambda b: b,
        [plain(o_mq, o_mk, sc=scale), item_mkt, plain(o_mv, o_og), plain(o_og, o_g, "sigmoid"), item_gates],
        "proj_mlstm")
    qa, ka, va, bg = _fused_proj(
        x, mod, norm1_w[l], lambda b: b,
        [("normrope", [wb(o_aq, o_ak), cos * qw, sin * qw[partner]], dict(rope=True)),
         ("normrope", [wb(o_ak, o_av), cos * kw, sin * kw[partner]], dict(rope=True)),
         plain(o_av, o_bg), plain(o_bg, o_bg + 2 * D, "sigmoid")],
        "proj_attn")

    lctx = ctx.shape[1]
    cmkt, cmv, cgt, cka, cva = _fused_proj(
        ctx, mod, norm1_w[l], lambda b: B,
        [item_mkt, plain(o_mv, o_og), item_gates,
         ("normrope", [wb(o_ak, o_av), jnp.broadcast_to(kw, (lctx, HEAD_DIM))], dict(rope=False)),
         plain(o_av, o_bg)],
        "proj_ctx")

    ga = _gate_prep(gt, "gate_prep")
    gca = _gate_prep(cgt, "gate_prep_ctx")
    ym = _mlstm(mq, mkt, mv, ga, og, cmkt, cmv, gca, mh_norm_w[l])
    oa = _attention(qa, ka, va, cka, cva)

    pad_w = jnp.zeros((EXPERT_ROW0 - N_GROUPS, D), F32)
    pad_b = jnp.zeros((EXPERT_ROW0 - N_GROUPS,), F32)
    wr = jnp.concatenate([w_rg[l].T, pad_w, w_re[l].T], axis=0)
    br = jnp.concatenate([b_rg[l], pad_b, b_re[l]]).reshape(ROUTER_ROWS, 1)
    hmid, f, route, w1, w2, cnt = _merge(ym, oa, bg, x, mod, w_branch_m[l].astype(BF16), w_branch_a[l].astype(BF16),
                                         w_out[l].astype(BF16), norm2_w[l], wr, br)

    off, tile_expert, n_used, n_rows = _expert_layout(cnt, 2 * B * L)
    xs = _dispatch(off, n_used, route, f.reshape(B * L, D), n_rows)
    ys = _experts(tile_expert, n_used, xs, w_e_gate[l], w_e_up[l], w_e_down[l])
    return _combine(off, route, hmid, mod, w1, w2, ys)
```

```python
import functools

import jax
import jax.numpy as jnp
from jax import lax
from jax.experimental import pallas as pl
from jax.experimental.pallas import tpu as pltpu

F32 = jnp.float32
BF16 = jnp.bfloat16

EPS = 1e-6
GRID_W = 64
ROPE_THETA = 10000.0
HEAD_DIM = 128
M_HEADS = 8
A_HEADS = 8
A_KV_HEADS = 2
A_GROUP = A_HEADS // A_KV_HEADS
N_GROUPS = 4
EXPERTS_PER_GROUP = 8
N_EXPERTS = N_GROUPS * EXPERTS_PER_GROUP
M_INIT = -1e30
MLSTM_CHUNK = 128
GATE_ROWS = 8
ROUTER_LANES = 128
EXPERT_ROW0 = 8
ROUTER_ROWS = EXPERT_ROW0 + N_EXPERTS
EXPERT_TILE = 512
MOE_TOKEN_TILE = 1024
DISPATCH_TILE = 2048
EXPERT_BUFS = 3
COMBINE_PARTS = 4
ROUTE_ROWS = 16
SLOT_ROWS = 8
VMEM_LIMIT_BYTES = 48 * 1024 * 1024

NT_DIMS = (((1,), (1,)), ((), ()))


def _cparams(*sem, flags=None):
    return pltpu.CompilerParams(dimension_semantics=sem, vmem_limit_bytes=VMEM_LIMIT_BYTES, flags=flags)


def _dot(a, b):
    return jnp.dot(a, b, preferred_element_type=F32)


def _dot_nt(a, b):
    return lax.dot_general(a, b, NT_DIMS, preferred_element_type=F32)


def _split3(x):
    hi = x.astype(BF16)
    r1 = x - hi.astype(F32)
    mid = r1.astype(BF16)
    lo = (r1 - mid.astype(F32)).astype(BF16)
    return hi, mid, lo


def _silu(x):
    return x * jax.nn.sigmoid(x)


def _ada_kernel(c_ref, w_ref, b_ref, o_ref):
    s = _silu(c_ref[...])
    parts = _split3(s)
    w = w_ref[...]
    w_hi = w.astype(BF16)
    w_lo = (w - w_hi.astype(F32)).astype(BF16)
    acc = _dot(parts[0], w_hi) + _dot(parts[1], w_hi) + _dot(parts[0], w_lo)
    o_ref[...] = acc + b_ref[...]


def _ada_mod(cc, w, b):
    rows, d = cc.shape
    n = w.shape[1]
    tn = min(n, 1536)
    return pl.pallas_call(
        _ada_kernel,
        grid=(n // tn,),
        in_specs=[
            pl.BlockSpec((rows, d), lambda j: (0, 0)),
            pl.BlockSpec((d, tn), lambda j: (0, j)),
            pl.BlockSpec((1, tn), lambda j: (0, j)),
        ],
        out_specs=pl.BlockSpec((rows, tn), lambda j: (0, j)),
        out_shape=jax.ShapeDtypeStruct((rows, n), F32),
        compiler_params=_cparams("parallel"),
        name="ada_mod",
    )(cc, w, b.reshape(1, n))


def _rope_partner(j):
    quarter = HEAD_DIM // 4
    return jnp.where((j % (2 * quarter)) < quarter, j + quarter, j - quarter)


def _split2_lanes(x):
    hi = x.astype(BF16)
    return jnp.concatenate([hi, (x - hi.astype(F32)).astype(BF16)], axis=1)


def _fused_proj_kernel(*refs, plan):
    x_ref, mod_ref, nw_ref = refs[:3]
    n_inputs = sum(n for _, n, _ in plan)
    in_refs = refs[3:3 + n_inputs]
    out_refs = refs[3 + n_inputs:]
    x = x_ref[0]
    var = jnp.mean(x * x, axis=-1, keepdims=True)
    y = x * lax.rsqrt(var + EPS) * nw_ref[...]
    xn = (y * (1.0 + mod_ref[0, 1:2, :]) + mod_ref[0, 0:1, :]).astype(BF16)
    tm = xn.shape[0]
    C = MLSTM_CHUNK
    pos = 0
    for (kind, n, prm), o_ref in zip(plan, out_refs):
        ins = in_refs[pos:pos + n]
        pos += n
        if kind == "plain":
            acc = _dot(xn, ins[0][...])
            if prm["act"] == "sigmoid":
                acc = jax.nn.sigmoid(acc)
            if prm["scale"] != 1.0:
                acc = acc * prm["scale"]
            o_ref[0] = acc.astype(o_ref.dtype)
        elif kind == "kt":
            acc = _dot_nt(ins[0][...], xn)
            for c in range(tm // C):
                o_ref[0, c] = acc[:, c * C:(c + 1) * C].astype(o_ref.dtype)
        elif kind == "gates":
            acc = _dot_nt(ins[0][...], xn) + ins[1][...]
            for h in range(M_HEADS):
                for c in range(tm // C):
                    o_ref[0, h, c] = acc[h * GATE_ROWS:(h + 1) * GATE_ROWS, c * C:(c + 1) * C]
        else:
            acc = _dot(xn, ins[0][...])
            cw = ins[1][...]
            if prm["rope"]:
                sw = ins[2][...]
                lane = lax.broadcasted_iota(jnp.int32, (tm, HEAD_DIM), 1)
                up = _rope_partner(lane) > lane
            for h in range(acc.shape[1] // HEAD_DIM):
                a = acc[:, h * HEAD_DIM:(h + 1) * HEAD_DIM]
                hvar = jnp.mean(a * a, axis=-1, keepdims=True)
                yh = a * cw
                if prm["rope"]:
                    quarter = HEAD_DIM // 4
                    partner = jnp.where(up, pltpu.roll(a, HEAD_DIM - quarter, 1), pltpu.roll(a, quarter, 1))
                    yh = yh + partner * sw
                o_ref[0, :, h * HEAD_DIM:(h + 1) * HEAD_DIM] = (yh * lax.rsqrt(hvar + EPS)).astype(o_ref.dtype)


def _fused_proj(x, mod, nw, mod_row, items, name):
    B, L, D = x.shape
    tm = min(L, 512)
    C = MLSTM_CHUNK
    full2 = lambda b, i: (0, 0)
    in_specs = [
        pl.BlockSpec((1, tm, D), lambda b, i: (b, i, 0)),
        pl.BlockSpec((1, 6, D), lambda b, i: (mod_row(b), 0, 0)),
        pl.BlockSpec((1, D), full2),
    ]
    args = [x, mod, nw.reshape(1, D)]
    out_specs, out_shape, plan = [], [], []
    for kind, arrays, prm in items:
        plan.append((kind, len(arrays), prm))
        args.extend(arrays)
        w = arrays[0]
        if kind in ("plain", "normrope"):
            N = w.shape[1]
            in_specs.append(pl.BlockSpec((D, N), full2))
            for t in arrays[1:]:
                in_specs.append(pl.BlockSpec((tm, HEAD_DIM), lambda b, i: (i, 0)))
            out_specs.append(pl.BlockSpec((1, tm, N), lambda b, i: (b, i, 0)))
            out_shape.append(jax.ShapeDtypeStruct((B, L, N), BF16))
        elif kind == "kt":
            R = w.shape[0]
            in_specs.append(pl.BlockSpec((R, D), full2))
            out_specs.append(pl.BlockSpec((1, tm // C, R, C), lambda b, i: (b, i, 0, 0)))
            out_shape.append(jax.ShapeDtypeStruct((B, L // C, R, C), BF16))
        else:
            R = w.shape[0]
            in_specs.append(pl.BlockSpec((R, D), full2))
            in_specs.append(pl.BlockSpec((R, 1), full2))
            out_specs.append(pl.BlockSpec((1, M_HEADS, tm // C, GATE_ROWS, C), lambda b, i: (b, 0, i, 0, 0)))
            out_shape.append(jax.ShapeDtypeStruct((B, M_HEADS, L // C, GATE_ROWS, C), F32))
    return pl.pallas_call(
        functools.partial(_fused_proj_kernel, plan=tuple(plan)),
        grid=(B, L // tm),
        in_specs=in_specs,
        out_specs=out_specs,
        out_shape=out_shape,
        compiler_params=_cparams("parallel", "parallel"),
        name=name,
    )(*args)


def _rope_tables(L):
    rows = L // GRID_W
    row = jnp.repeat(jnp.arange(rows), GRID_W).astype(F32)
    col = jnp.tile(jnp.arange(GRID_W), rows).astype(F32)
    half = HEAD_DIM // 2
    inv = ROPE_THETA ** (-jnp.arange(0, half, 2, dtype=F32) / half)
    ang_r = row[:, None] * inv[None, :]
    ang_c = col[:, None] * inv[None, :]
    cr, sr, cc, sc = jnp.cos(ang_r), jnp.sin(ang_r), jnp.cos(ang_c), jnp.sin(ang_c)
    return jnp.concatenate([cr, cr, cc, cc], axis=-1), jnp.concatenate([-sr, sr, -sc, sc], axis=-1)


def _log_sigmoid(x):
    return jnp.minimum(x, 0.0) - jnp.log1p(jnp.exp(-jnp.abs(x)))


def _tri_masks(n):
    row_i = lax.broadcasted_iota(jnp.int32, (n, n), 0)
    col_i = lax.broadcasted_iota(jnp.int32, (n, n), 1)
    return row_i <= col_i, row_i >= col_i


def _gate_prep_kernel(g_ref, a_ref):
    _, H, nc, R8, C = g_ref.shape
    rows = H * nc * R8
    g = g_ref[0].reshape(rows, C)
    le, ge = _tri_masks(C)
    upper = jnp.where(le, 1.0, 0.0).astype(BF16)
    lower = jnp.where(ge, 1.0, 0.0).astype(BF16)
    lf = _log_sigmoid(g)
    parts = _split3(lf)
    pre = _dot(parts[0], upper) + _dot(parts[1], upper) + _dot(parts[2], upper)
    suf = _dot(parts[0], lower) + _dot(parts[1], lower) + _dot(parts[2], lower)
    rtype = lax.broadcasted_iota(jnp.int32, (rows, C), 0) % R8
    a = jnp.where(rtype == 1, pre, jnp.where(rtype == 3, suf, jnp.where(rtype >= 4, lf, g)))
    a_ref[0] = a.reshape(H, nc, R8, C)


def _gate_prep(g, name):
    B, H, nc, R8, C = g.shape
    blk = pl.BlockSpec((1, H, nc, R8, C), lambda b: (b, 0, 0, 0, 0))
    return pl.pallas_call(
        _gate_prep_kernel,
        grid=(B,),
        in_specs=[blk],
        out_specs=blk,
        out_shape=jax.ShapeDtypeStruct(g.shape, F32),
        compiler_params=_cparams("parallel"),
        name=name,
    )(g)


def _mlstm_kernel(q_ref, kt_ref, v_ref, a_ref, og_ref, kct_ref, vc_ref, ac_ref, nw_ref, o_ref, cin_ref, min_ref,
                  *, nc, nctx):
    C = MLSTM_CHUNK
    D = HEAD_DIM
    le, ge = _tri_masks(C)
    le_bf = jnp.where(le, 1.0, 0.0).astype(BF16)
    ge_bf = jnp.where(ge, 1.0, 0.0).astype(BF16)
    ones_blk = jnp.ones((C, D), BF16)

    def gates(ref, j, fwd):
        ir, fr = (0, 1) if fwd else (2, 3)
        b_row = ref[0, 0, j, fr:fr + 1, :]
        a_row = ref[0, 0, j, ir:ir + 1, :] - b_row
        return a_row, (b_row[:, C - 1:C] if fwd else b_row[:, 0:1])

    def vaug_of(ref, j):
        return jnp.concatenate([ref[0, j * C:(j + 1) * C, :], ones_blk], axis=1)

    def update(cst, m_prev, kt, vaug, a_row, btot):
        mc = jnp.maximum(jnp.max(a_row, axis=1, keepdims=True), m_prev)
        w_in = jnp.exp(a_row - mc)
        ktw = (kt.astype(F32) * w_in).astype(BF16)
        return jnp.exp(m_prev - mc) * cst + _dot(ktw, vaug), btot + mc

    for d, fwd in ((0, True), (1, False)):
        cst = jnp.zeros((D, 2 * D), F32)
        m = jnp.full((1, 1), M_INIT, F32)
        for j in (range(nctx) if fwd else reversed(range(nctx))):
            a_row, btot = gates(ac_ref, j, fwd)
            cst, m = update(cst, m, kct_ref[0, j], vaug_of(vc_ref, j), a_row, btot)
        order = list(range(nc) if fwd else reversed(range(nc)))
        for idx, j in enumerate(order):
            cin_ref[d, j] = cst.astype(BF16)
            min_ref[d, j] = jnp.broadcast_to(m, min_ref.shape[2:])
            if idx + 1 < nc:
                a_row, btot = gates(a_ref, j, fwd)
                cst, m = update(cst, m, kt_ref[0, j], vaug_of(v_ref, j), a_row, btot)

    def out_step(d, j, fwd, q, kt, vaug):
        a_row, _ = gates(a_ref, j, fwd)
        m_prev = min_ref[d, j][0:1, 0:1]
        nlf = -a_ref[0, 0, j, (4 if fwd else 5):(5 if fwd else 6), :]
        nlf_hi = nlf.astype(BF16)
        nlf_lo = (nlf - nlf_hi.astype(F32)).astype(BF16)
        vis = ge_bf if fwd else le_bf
        nb = _dot_nt(vis, jnp.broadcast_to(nlf_hi, (D, C))) + _dot_nt(vis, jnp.broadcast_to(nlf_lo, (D, C)))
        a_mat = jnp.where(ge if fwd else le, a_row, -jnp.inf)
        m_q = jnp.maximum(jnp.max(a_mat, axis=1, keepdims=True), m_prev)
        bm = jnp.broadcast_to(m_q, (C, D))
        w_intra = jnp.exp(a_mat - bm)
        p = (_dot(q, kt) * w_intra).astype(BF16)
        qc = _dot(q, cin_ref[d, j])
        pv = _dot(p, vaug)
        w_inter = jnp.exp(m_prev - bm)
        num = w_inter * qc[:, :D] + pv[:, :D]
        den = w_inter * qc[:, D:] + pv[:, D:]
        return num / jnp.maximum(jnp.abs(den), jnp.exp(nb - bm))

    nw = nw_ref[...]

    def chunk_out(j, carry):
        sl = pl.ds(pl.multiple_of(j * C, C), C)
        q = q_ref[0, sl, :]
        kt = kt_ref[0, j]
        vaug = jnp.concatenate([v_ref[0, sl, :], ones_blk], axis=1)
        h = out_step(0, j, True, q, kt, vaug) + out_step(1, j, False, q, kt, vaug)
        var = jnp.mean(h * h, axis=-1, keepdims=True)
        y = h * lax.rsqrt(var + EPS) * nw
        o_ref[0, sl, :] = (og_ref[0, sl, :].astype(F32) * y).astype(o_ref.dtype)
        return carry

    lax.fori_loop(0, nc, chunk_out, 0, unroll=nc)


def _mlstm(q, kt, v, a, og, kct, vc, ac, mh_norm_w):
    B, L, W = q.shape
    C = MLSTM_CHUNK
    assert C == HEAD_DIM
    nc = L // C
    lctx = vc.shape[1]
    nctx = lctx // C
    H = W // HEAD_DIM
    return pl.pallas_call(
        functools.partial(_mlstm_kernel, nc=nc, nctx=nctx),
        grid=(B, H),
        in_specs=[
            pl.BlockSpec((1, L, HEAD_DIM), lambda b, h: (b, 0, h)),
            pl.BlockSpec((1, nc, HEAD_DIM, C), lambda b, h: (b, 0, h, 0)),
            pl.BlockSpec((1, L, HEAD_DIM), lambda b, h: (b, 0, h)),
            pl.BlockSpec((1, 1, nc, GATE_ROWS, C), lambda b, h: (b, h, 0, 0, 0)),
            pl.BlockSpec((1, L, HEAD_DIM), lambda b, h: (b, 0, h)),
            pl.BlockSpec((1, nctx, HEAD_DIM, C), lambda b, h: (b, 0, h, 0)),
            pl.BlockSpec((1, lctx, HEAD_DIM), lambda b, h: (b, 0, h)),
            pl.BlockSpec((1, 1, nctx, GATE_ROWS, C), lambda b, h: (b, h, 0, 0, 0)),
            pl.BlockSpec((1, HEAD_DIM), lambda b, h: (0, h)),
        ],
        out_specs=pl.BlockSpec((1, L, HEAD_DIM), lambda b, h: (b, 0, h)),
        out_shape=jax.ShapeDtypeStruct((B, L, W), BF16),
        scratch_shapes=[
            pltpu.VMEM((2, nc, HEAD_DIM, 2 * HEAD_DIM), BF16),
            pltpu.VMEM((2, nc, 8, HEAD_DIM), F32),
        ],
        compiler_params=_cparams("parallel", "parallel"),
        name="mlstm",
    )(q, kt, v, a, og, kct, vc, ac, mh_norm_w.reshape(1, W))


def _attn_kernel(q_ref, k_ref, v_ref, kc_ref, vc_ref, o_ref):
    ones_l = jnp.ones((v_ref.shape[1], HEAD_DIM), BF16)
    ones_c = jnp.ones((vc_ref.shape[1], HEAD_DIM), BF16)
    for g in range(A_KV_HEADS):
        gs = slice(g * HEAD_DIM, (g + 1) * HEAD_DIM)
        k = k_ref[0, :, gs]
        kc = kc_ref[0, :, gs]
        v = jnp.concatenate([v_ref[0, :, gs], ones_l], axis=1)
        vc = jnp.concatenate([vc_ref[0, :, gs], ones_c], axis=1)
        for h in range(g * A_GROUP, (g + 1) * A_GROUP):
            hs = slice(h * HEAD_DIM, (h + 1) * HEAD_DIM)
            q = q_ref[0, :, hs]
            s1 = _dot_nt(q, k)
            s2 = _dot_nt(q, kc)
            m = jnp.maximum(jnp.max(s1, axis=1, keepdims=True), jnp.max(s2, axis=1, keepdims=True))
            ol = _dot(jnp.exp(s1 - m).astype(BF16), v) + _dot(jnp.exp(s2 - m).astype(BF16), vc)
            o_ref[0, :, hs] = (ol[:, :HEAD_DIM] / ol[:, HEAD_DIM:]).astype(o_ref.dtype)


def _attention(q, k, v, kc, vc):
    B, L, W = q.shape
    lctx = kc.shape[1]
    tq = min(L, 1024)
    kvw = A_KV_HEADS * HEAD_DIM
    return pl.pallas_call(
        _attn_kernel,
        grid=(B, L // tq),
        in_specs=[
            pl.BlockSpec((1, tq, W), lambda b, i: (b, i, 0)),
            pl.BlockSpec((1, L, kvw), lambda b, i: (b, 0, 0)),
            pl.BlockSpec((1, L, kvw), lambda b, i: (b, 0, 0)),
            pl.BlockSpec((1, lctx, kvw), lambda b, i: (b, 0, 0)),
            pl.BlockSpec((1, lctx, kvw), lambda b, i: (b, 0, 0)),
        ],
        out_specs=pl.BlockSpec((1, tq, W), lambda b, i: (b, i, 0)),
        out_shape=jax.ShapeDtypeStruct((B, L, W), BF16),
        compiler_params=_cparams("parallel", "parallel"),
        name="attention",
    )(q, k, v, kc, vc)


def _merge_kernel(ym_ref, oa_ref, bg_ref, x_ref, mod_ref, wbm_ref, wba_ref, wo_ref, n2_ref, wr_ref, br_ref,
                  hmid_ref, f_ref, route_ref, w1_ref, w2_ref, cnt_ref, run_ref, fprev_ref):
    step = pl.program_id(0)

    @pl.when(step == 0)
    def _():
        run_ref[...] = jnp.zeros_like(run_ref)
        fprev_ref[...] = jnp.zeros_like(fprev_ref)

    D = x_ref.shape[-1]
    tm = x_ref.shape[1]
    f_prev = fprev_ref[...]
    live = jnp.where(step > 0, 1.0, 0.0)
    run = run_ref[...]

    f_hi = f_prev.astype(BF16)
    f_lo = (f_prev - f_hi.astype(F32)).astype(BF16)
    wr = wr_ref[...]
    wr_hi = wr.astype(BF16)
    wr_lo = (wr - wr_hi.astype(F32)).astype(BF16)
    ym = _dot(ym_ref[0], wbm_ref[...])
    logits = _dot_nt(wr_hi, f_hi) + _dot_nt(wr_hi, f_lo) + _dot_nt(wr_lo, f_hi) + br_ref[...]
    row = lax.broadcasted_iota(jnp.int32, logits.shape, 0)
    big = jnp.int32(ROUTER_ROWS)
    neg = -jnp.inf
    is_grp = row < N_GROUPS
    gl = jnp.where(is_grp, logits, neg)
    gmax = jnp.max(gl, axis=0, keepdims=True)
    gsel = jnp.min(jnp.where(gl == gmax, row, big), axis=0, keepdims=True)
    p_grp = 1.0 / jnp.sum(jnp.where(is_grp, jnp.exp(logits - gmax), 0.0), axis=0, keepdims=True)
    lo = EXPERT_ROW0 + gsel * EXPERTS_PER_GROUP
    in_grp = (row >= lo) & (row < lo + EXPERTS_PER_GROUP)
    el = jnp.where(in_grp, logits, neg)
    v1 = jnp.max(el, axis=0, keepdims=True)
    i1 = jnp.min(jnp.where(el == v1, row, big), axis=0, keepdims=True)
    el2 = jnp.where(row == i1, neg, el)
    v2 = jnp.max(el2, axis=0, keepdims=True)
    i2 = jnp.min(jnp.where(el2 == v2, row, big), axis=0, keepdims=True)
    e21 = jnp.exp(v2 - v1)
    w1 = p_grp / (1.0 + e21)
    w2 = p_grp * e21 / (1.0 + e21)
    ya = _dot(oa_ref[0], wba_ref[...])

    sel = jnp.where((row == i1) | (row == i2), live, 0.0)
    tok_u = lax.broadcasted_iota(jnp.int32, (tm, tm), 0)
    tok_n = lax.broadcasted_iota(jnp.int32, (tm, tm), 1)
    earlier = jnp.where(tok_u < tok_n, 1.0, 0.0).astype(BF16)
    rank_all = run + _dot(sel.astype(BF16), earlier)
    rank1 = jnp.sum(jnp.where(row == i1, rank_all, 0.0), axis=0, keepdims=True)
    rank2 = jnp.sum(jnp.where(row == i2, rank_all, 0.0), axis=0, keepdims=True)
    route_ref[...] = jnp.zeros_like(route_ref)
    route_ref[0:1, :] = i1 - EXPERT_ROW0
    route_ref[1:2, :] = i2 - EXPERT_ROW0
    route_ref[8:9, :] = rank1.astype(jnp.int32)
    route_ref[9:10, :] = rank2.astype(jnp.int32)
    run = run + jnp.sum(sel, axis=1, keepdims=True)
    run_ref[...] = run
    cnt_ref[...] = jnp.broadcast_to(run, cnt_ref.shape)
    g0 = bg_ref[0, :, :D].astype(F32)
    g1 = bg_ref[0, :, D:].astype(F32)
    mix = _dot((g0 * ym + g1 * ya).astype(BF16), wo_ref[...])

    for w_row, w_ref in ((w1, w1_ref), (w2, w2_ref)):
        w_ref[0] = jnp.transpose(jnp.broadcast_to(w_row, (ROUTER_LANES, tm)))

    hmid = x_ref[0] + mod_ref[0, 2:3, :] * mix
    hmid_ref[0] = hmid
    var = jnp.mean(hmid * hmid, axis=-1, keepdims=True)
    y = hmid * lax.rsqrt(var + EPS) * n2_ref[...]
    f = y * (1.0 + mod_ref[0, 4:5, :]) + mod_ref[0, 3:4, :]
    f_ref[0] = f.astype(f_ref.dtype)
    fprev_ref[...] = f


def _merge(ym, oa, bg, x, mod, wbm, wba, wo, n2, wr, br):
    B, L, D = x.shape
    tm = min(L, 512)
    nt = L // tm
    n = B * nt
    full = lambda s: (0, 0)
    main = lambda s: jnp.minimum(s, n - 1)
    tok = lambda s: (main(s) // nt, main(s) % nt, 0)
    prev = lambda s: jnp.maximum(s - 1, 0)
    tok_prev = lambda s: (prev(s) // nt, prev(s) % nt, 0)
    return pl.pallas_call(
        _merge_kernel,
        grid=(n + 1,),
        in_specs=[
            pl.BlockSpec((1, tm, D), tok),
            pl.BlockSpec((1, tm, D), tok),
            pl.BlockSpec((1, tm, 2 * D), tok),
            pl.BlockSpec((1, tm, D), tok),
            pl.BlockSpec((1, 6, D), lambda s: (main(s) // nt, 0, 0)),
            pl.BlockSpec((D, D), full),
            pl.BlockSpec((D, D), full),
            pl.BlockSpec((D, D), full),
            pl.BlockSpec((1, D), full),
            pl.BlockSpec((ROUTER_ROWS, D), full),
            pl.BlockSpec((ROUTER_ROWS, 1), full),
        ],
        out_specs=[
            pl.BlockSpec((1, tm, D), tok),
            pl.BlockSpec((1, tm, D), tok),
            pl.BlockSpec((ROUTE_ROWS, tm), lambda s: (0, prev(s))),
            pl.BlockSpec((1, tm, ROUTER_LANES), tok_prev),
            pl.BlockSpec((1, tm, ROUTER_LANES), tok_prev),
            pl.BlockSpec((ROUTER_ROWS, ROUTER_LANES), full),
        ],
        out_shape=[
            jax.ShapeDtypeStruct((B, L, D), F32),
            jax.ShapeDtypeStruct((B, L, D), F32),
            jax.ShapeDtypeStruct((ROUTE_ROWS, B * L), jnp.int32),
            jax.ShapeDtypeStruct((B, L, ROUTER_LANES), F32),
            jax.ShapeDtypeStruct((B, L, ROUTER_LANES), F32),
            jax.ShapeDtypeStruct((ROUTER_ROWS, ROUTER_LANES), F32),
        ],
        scratch_shapes=[pltpu.VMEM((ROUTER_ROWS, 1), F32), pltpu.VMEM((tm, D), F32)],
        compiler_params=_cparams("arbitrary"),
        name="merge_router",
    )(ym, oa, bg, x, mod, wbm, wba, wo, n2.reshape(1, D), wr, br)


def _load_slots(off_ref, r_ref, s_vmem, s_smem, rsem):
    expert = r_ref[0:SLOT_ROWS, :]
    slot = r_ref[SLOT_ROWS:2 * SLOT_ROWS, :]
    for e in range(1, N_EXPERTS):
        slot = slot + jnp.where(expert == e, off_ref[e], 0)
    s_vmem[...] = slot
    cp = pltpu.make_async_copy(s_vmem, s_smem, rsem)
    cp.start()
    cp.wait()


def _dispatch_kernel(off_ref, nu_ref, r_ref, f_ref, xs_ref, s_vmem, s_smem, zbuf, sem, rsem, *, tm):
    T = EXPERT_TILE

    @pl.when(pl.program_id(0) == 0)
    def _():
        zbuf[...] = jnp.zeros_like(zbuf)

        def zero_copy(row0):
            return pltpu.make_async_copy(zbuf, xs_ref.at[pl.ds(pl.multiple_of(row0, T), T), :], sem)

        def for_each_zero_tile(fn):
            def unused(t, carry):
                fn(zero_copy(t * T))
                return carry

            lax.fori_loop(nu_ref[0], xs_ref.shape[0] // T, unused, 0)
            for e in range(N_EXPERTS):
                @pl.when(off_ref[e + 1] > off_ref[e])
                def _():
                    fn(zero_copy(off_ref[e + 1] - T))

        for_each_zero_tile(lambda cp: cp.start())
        for_each_zero_tile(lambda cp: cp.wait())

    _load_slots(off_ref, r_ref, s_vmem, s_smem, rsem)
    for t in range(tm):
        for k in range(2):
            pltpu.make_async_copy(f_ref.at[pl.ds(t, 1), :], xs_ref.at[pl.ds(s_smem[k, t], 1), :], sem).start(priority=k)
    for _ in range(2):
        pltpu.make_async_copy(f_ref, f_ref, sem).wait()


def _dispatch(off, n_used, route, f, n_rows):
    N, D = f.shape
    tm = min(N, DISPATCH_TILE)
    return pl.pallas_call(
        functools.partial(_dispatch_kernel, tm=tm),
        grid_spec=pltpu.PrefetchScalarGridSpec(
            num_scalar_prefetch=2,
            grid=(N // tm,),
            in_specs=[
                pl.BlockSpec((ROUTE_ROWS, tm), lambda i, off, nu: (0, i)),
                pl.BlockSpec((tm, D), lambda i, off, nu: (i, 0)),
            ],
            out_specs=pl.BlockSpec(memory_space=pl.ANY),
            scratch_shapes=[
                pltpu.VMEM((SLOT_ROWS, tm), jnp.int32),
                pltpu.SMEM((SLOT_ROWS, tm), jnp.int32),
                pltpu.VMEM((EXPERT_TILE, D), F32),
                pltpu.SemaphoreType.DMA(()),
                pltpu.SemaphoreType.DMA(()),
            ],
        ),
        out_shape=jax.ShapeDtypeStruct((n_rows, D), F32),
        compiler_params=_cparams("arbitrary"),
        name="moe_dispatch",
    )(off, n_used, route, f)


def _expert_kernel(te_ref, nu_ref, xs_ref, wg_ref, wu_ref, wd_ref, ys_ref, xbuf, xsem):
    t = pl.program_id(0)
    n_used = nu_ref[0]
    used = t < n_used
    T = EXPERT_TILE

    def fetch(tile):
        slot = tile % EXPERT_BUFS
        return pltpu.make_async_copy(xs_ref.at[pl.ds(pl.multiple_of(tile * T, T), T), :], xbuf.at[slot],
                                     xsem.at[slot])

    @pl.when(t == 0)
    def _():
        for ahead in range(EXPERT_BUFS - 1):
            @pl.when(ahead < n_used)
            def _():
                fetch(ahead).start()

    @pl.when(t + EXPERT_BUFS - 1 < n_used)
    def _():
        fetch(t + EXPERT_BUFS - 1).start()

    @pl.when(used)
    def _():
        fetch(t).wait()
        x = xbuf[t % EXPERT_BUFS].astype(BF16)
        hidden = (_silu(_dot(x, wg_ref[0].astype(BF16))) * _dot(x, wu_ref[0].astype(BF16))).astype(BF16)
        ys_ref[...] = _dot(hidden, wd_ref[0].astype(BF16))

    @pl.when(jnp.logical_not(used))
    def _():
        ys_ref[...] = jnp.zeros_like(ys_ref)


def _experts(tile_expert, n_used, xs, wg, wu, wd):
    P, D = xs.shape
    E, _, DE = wg.shape
    T = EXPERT_TILE
    row_map = lambda t, te, nu: (t, 0)
    w_map = lambda t, te, nu: (te[t], 0, 0)
    return pl.pallas_call(
        _expert_kernel,
        grid_spec=pltpu.PrefetchScalarGridSpec(
            num_scalar_prefetch=2,
            grid=(P // T,),
            in_specs=[
                pl.BlockSpec(memory_space=pl.ANY),
                pl.BlockSpec((1, D, DE), w_map),
                pl.BlockSpec((1, D, DE), w_map),
                pl.BlockSpec((1, DE, D), w_map),
            ],
            out_specs=pl.BlockSpec((T, D), row_map),
            scratch_shapes=[pltpu.VMEM((EXPERT_BUFS, T, D), F32), pltpu.SemaphoreType.DMA((EXPERT_BUFS,))],
        ),
        out_shape=jax.ShapeDtypeStruct((P, D), F32),
        compiler_params=_cparams("arbitrary"),
        name="moe_experts",
    )(tile_expert, n_used, xs, wg, wu, wd)


def _combine_kernel(off_ref, r_ref, hmid_ref, mod_ref, w1_ref, w2_ref, ys_ref, o_ref, s_vmem, s_smem, y1, y2, sem,
                    rsem, *, tm):
    _load_slots(off_ref, r_ref, s_vmem, s_smem, rsem)
    part = tm // COMBINE_PARTS
    for t in range(tm):
        for k, buf in ((0, y1), (1, y2)):
            pltpu.make_async_copy(ys_ref.at[pl.ds(s_smem[k, t], 1), :], buf.at[pl.ds(t, 1), :],
                                  sem.at[t // part]).start(priority=k)
    D = o_ref.shape[-1]
    for hf in range(COMBINE_PARTS):
        rows = pl.ds(hf * part, part)
        for buf in (y1, y2):
            pltpu.make_async_copy(buf.at[rows, :], buf.at[rows, :], sem.at[hf]).wait()
        w1 = w1_ref[0, rows, :]
        w2 = w2_ref[0, rows, :]
        for c in range(D // ROUTER_LANES):
            sl = slice(c * ROUTER_LANES, (c + 1) * ROUTER_LANES)
            moe = w1 * y1[rows, sl] + w2 * y2[rows, sl]
            o_ref[0, rows, sl] = hmid_ref[0, rows, sl] + mod_ref[0, 5:6, sl] * moe


def _combine(off, route, hmid, mod, w1, w2, ys):
    B, L, D = hmid.shape
    tm = min(L, MOE_TOKEN_TILE)
    nt = L // tm
    tok = lambda b, i, off: (b, i, 0)
    return pl.pallas_call(
        functools.partial(_combine_kernel, tm=tm),
        grid_spec=pltpu.PrefetchScalarGridSpec(
            num_scalar_prefetch=1,
            grid=(B, nt),
            in_specs=[
                pl.BlockSpec((ROUTE_ROWS, tm), lambda b, i, off: (0, b * nt + i)),
                pl.BlockSpec((1, tm, D), tok),
                pl.BlockSpec((1, 6, D), lambda b, i, off: (b, 0, 0)),
                pl.BlockSpec((1, tm, ROUTER_LANES), tok),
                pl.BlockSpec((1, tm, ROUTER_LANES), tok),
                pl.BlockSpec(memory_space=pl.ANY),
            ],
            out_specs=pl.BlockSpec((1, tm, D), tok),
            scratch_shapes=[
                pltpu.VMEM((SLOT_ROWS, tm), jnp.int32),
                pltpu.SMEM((SLOT_ROWS, tm), jnp.int32),
                pltpu.VMEM((tm, D), F32),
                pltpu.VMEM((tm, D), F32),
                pltpu.SemaphoreType.DMA((COMBINE_PARTS,)),
                pltpu.SemaphoreType.DMA(()),
            ],
        ),
        out_shape=jax.ShapeDtypeStruct((B, L, D), F32),
        compiler_params=_cparams("arbitrary", "arbitrary"),
        name="moe_combine",
    )(off, route, hmid, mod, w1, w2, ys)


def _expert_layout(cnt, n_pairs):
    T = EXPERT_TILE
    counts = cnt[EXPERT_ROW0:, 0].astype(jnp.int32)
    ends = jnp.cumsum((counts + T - 1) // T).astype(jnp.int32)
    off = jnp.concatenate([jnp.zeros((1,), jnp.int32), ends * T])
    n_tiles = n_pairs // T + N_EXPERTS
    t_idx = jnp.arange(n_tiles, dtype=jnp.int32)
    tile_expert = jnp.minimum(jnp.sum(t_idx[:, None] >= ends[None, :], axis=1), N_EXPERTS - 1).astype(jnp.int32)
    n_used = ends[-1]
    tile_expert = jnp.where(t_idx < n_used, tile_expert, jnp.take(tile_expert, n_used - 1))
    return off, tile_expert, n_used.reshape(1), n_tiles * T


def _gate_weights(w_gate, b_gate):
    D = w_gate.shape[0]
    wt = w_gate.T.reshape(4, M_HEADS, D).transpose(1, 0, 2)
    wt = jnp.concatenate([wt, wt[:, 1:2], wt[:, 3:4], jnp.zeros((M_HEADS, GATE_ROWS - 6, D), wt.dtype)], axis=1)
    bt = b_gate.reshape(4, M_HEADS).T
    bt = jnp.concatenate([bt, bt[:, 1:2], bt[:, 3:4], jnp.zeros((M_HEADS, GATE_ROWS - 6), bt.dtype)], axis=1)
    return wt.reshape(M_HEADS * GATE_ROWS, D), bt.reshape(M_HEADS * GATE_ROWS)


def kernel(x, c, ctx, c_ctx, w_ada, b_ada, norm1_w, w_in, b_mgate, q_norm_w, k_norm_w, mh_norm_w, w_branch_m,
           w_branch_a, w_out, norm2_w, w_rg, b_rg, w_re, b_re, w_e_gate, w_e_up, w_e_down):
    B, L, D = x.shape
    depth = w_ada.shape[0]
    assert depth == 1, "context-stream update between layers is not implemented"
    l = 0
    mw = M_HEADS * HEAD_DIM
    aq = A_HEADS * HEAD_DIM
    akv = A_KV_HEADS * HEAD_DIM
    o_mq, o_mk, o_mv, o_og = 0, mw, 2 * mw, 3 * mw
    o_g = 4 * mw
    o_aq = o_g + 4 * M_HEADS
    o_ak = o_aq + aq
    o_av = o_ak + akv
    o_bg = o_av + akv
    scale = HEAD_DIM ** -0.5

    rows = ((B + 1 + 7) // 8) * 8
    cc = jnp.concatenate([c, c_ctx[None, :], jnp.zeros((rows - B - 1, D), F32)], axis=0)
    mod = _ada_mod(cc, w_ada[l], b_ada[l]).reshape(rows, 6, D)

    w = w_in[l]
    wb = lambda a, b: w[:, a:b].astype(BF16)
    cos, sin = _rope_tables(L)
    wgt, bgt = _gate_weights(w[:, o_g:o_aq], b_mgate[l])
    wgt, bgt = wgt.astype(BF16), bgt.reshape(-1, 1)
    wkt = w[:, o_mk:o_mv].T.astype(BF16)
    partner = _rope_partner(jnp.arange(HEAD_DIM))
    qw, kw = q_norm_w[l] * scale, k_norm_w[l]
    plain = lambda a, b, act="none", sc=1.0: ("plain", [wb(a, b)], dict(act=act, scale=sc))
    item_mkt = ("kt", [wkt], {})
    item_gates = ("gates", [wgt, bgt], {})

    mq, mkt, mv, og, gt = _fused_proj(
        x, mod, norm1_w[l], lambda b: b,
        [plain(o_mq, o_mk, sc=scale), item_mkt, plain(o_mv, o_og), plain(o_og, o_g, "sigmoid"), item_gates],
        "proj_mlstm")
    qa, ka, va, bg = _fused_proj(
        x, mod, norm1_w[l], lambda b: b,
        [("normrope", [wb(o_aq, o_ak), cos * qw, sin * qw[partner]], dict(rope=True)),
         ("normrope", [wb(o_ak, o_av), cos * kw, sin * kw[partner]], dict(rope=True)),
         plain(o_av, o_bg), plain(o_bg, o_bg + 2 * D, "sigmoid")],
        "proj_attn")

    lctx = ctx.shape[1]
    cmkt, cmv, cgt, cka, cva = _fused_proj(
        ctx, mod, norm1_w[l], lambda b: B,
        [item_mkt, plain(o_mv, o_og), item_gates,
         ("normrope", [wb(o_ak, o_av), jnp.broadcast_to(kw, (lctx, HEAD_DIM))], dict(rope=False)),
         plain(o_av, o_bg)],
        "proj_ctx")

    ga = _gate_prep(gt, "gate_prep")
    gca = _gate_prep(cgt, "gate_prep_ctx")
    ym = _mlstm(mq, mkt, mv, ga, og, cmkt, cmv, gca, mh_norm_w[l])
    oa = _attention(qa, ka, va, cka, cva)

    pad_w = jnp.zeros((EXPERT_ROW0 - N_GROUPS, D), F32)
    pad_b = jnp.zeros((EXPERT_ROW0 - N_GROUPS,), F32)
    wr = jnp.concatenate([w_rg[l].T, pad_w, w_re[l].T], axis=0)
    br = jnp.concatenate([b_rg[l], pad_b, b_re[l]]).reshape(ROUTER_ROWS, 1)
    hmid, f, route, w1, w2, cnt = _merge(ym, oa, bg, x, mod, w_branch_m[l].astype(BF16), w_branch_a[l].astype(BF16),
                                         w_out[l].astype(BF16), norm2_w[l], wr, br)

    off, tile_expert, n_used, n_rows = _expert_layout(cnt, 2 * B * L)
    xs = _dispatch(off, n_used, route, f.reshape(B * L, D), n_rows)
    ys = _experts(tile_expert, n_used, xs, w_e_gate[l], w_e_up[l], w_e_down[l])
    return _combine(off, route, hmid, mod, w1, w2, ys)
```
